```python
import math
import jax
import jax.numpy as jnp
from jax import lax
import numpy as np

D_MODEL = 1024
BATCH = 16
SEQ = 4096
DEPTH = 2

CTX_LEN = 256
GRID_W = 64

FNET_HEADS = 4
FNET_HEAD_DIM = 64
D_FNET = FNET_HEADS * FNET_HEAD_DIM
HYENA_HEADS = 6
HYENA_HEAD_DIM = 64
D_HYENA = HYENA_HEADS * HYENA_HEAD_DIM
HYENA_ORDER = 2
HYENA_EMB = 33
HYENA_BANDS = (HYENA_EMB - 1) // 2
HYENA_FILTER_HIDDEN = 64
HYENA_FAST_DECAY_PCT = 0.3
HYENA_SLOW_DECAY_PCT = 1.5
HYENA_DECAY_TARGET = 0.01
S5_GROUP = 16
S5_GROUPS = 24
D_S5 = S5_GROUPS * S5_GROUP
S5_STATE = 64
S5_DT_MIN = 0.001
S5_DT_MAX = 0.1
D_MIX = D_FNET + D_HYENA + D_S5
HYENA_OFF = D_FNET
S5_OFF = D_FNET + (HYENA_ORDER + 1) * D_HYENA
D_IN_PROJ = S5_OFF + D_S5
N_MOD = 6
D_FF = 2816
N_EXPERTS = 8
TOP_K = 2
D_FF_EXPERT = 3584
N_DENSE = (DEPTH + 1) // 2
N_MOE = DEPTH // 2
RMS_EPS = 1e-6

kernel_name = 'hybrid_fnet_hyena_s5_moe_dit'


def rmsnorm(x, g):
    xf = x.astype(jnp.float32)
    y = xf * lax.rsqrt(jnp.mean(xf * xf, axis=-1, keepdims=True) + RMS_EPS)
    return (y * g.astype(jnp.float32)).astype(x.dtype)


def modulate(h, shift, scale):
    return h * (1.0 + scale) + shift


def fourier_mix(u, w_f):
    b, l, _ = u.shape
    uh = u.astype(jnp.float32).reshape(b, l, FNET_HEADS, FNET_HEAD_DIM)
    f = jnp.fft.fft2(uh, axes=(1, 3), norm='ortho').real
    y = jnp.einsum('blhd,hde->blhe', f, w_f.astype(jnp.float32))
    return y.reshape(b, l, D_FNET).astype(u.dtype)


def short_conv_seq(u, w, bias):
    up = jnp.pad(u, ((0, 0), (1, 1), (0, 0)))
    return up[:, :-2] * w[0] + up[:, 1:-1] * w[1] + up[:, 2:] * w[2] + bias


def short_conv_grid(u, w, bias):
    b, l, ch = u.shape
    rows = l // GRID_W
    ug = jnp.pad(u.reshape(b, rows, GRID_W, ch), ((0, 0), (0, 0), (1, 1), (0, 0)))
    y = ug[:, :, :-2] * w[0] + ug[:, :, 1:-1] * w[1] + ug[:, :, 2:] * w[2] + bias
    return y.reshape(b, l, ch)


def hyena_filters(l, fw1, fb1, freq1, fw2, fb2, freq2, fw3):
    f32 = jnp.float32
    t = jnp.linspace(0.0, 1.0, l, dtype=f32)[:, None]
    w = (2.0 * math.pi / l) * jnp.arange(l, dtype=f32)[:, None]
    bands = jnp.linspace(1e-4, HYENA_BANDS - 1, HYENA_BANDS, dtype=f32)[None, :]
    z = jnp.concatenate([t, jnp.cos(bands * w), -jnp.sin(bands * w)], axis=-1)
    h = jnp.sin(freq1.astype(f32) * (z @ fw1.astype(f32) + fb1.astype(f32)))
    h = jnp.sin(freq2.astype(f32) * (h @ fw2.astype(f32) + fb2.astype(f32)))
    h = (h @ fw3.astype(f32)).reshape(l, 2, HYENA_ORDER, D_HYENA)
    max_decay = math.log(HYENA_DECAY_TARGET) / HYENA_FAST_DECAY_PCT
    min_decay = math.log(HYENA_DECAY_TARGET) / HYENA_SLOW_DECAY_PCT
    deltas = jnp.abs(jnp.linspace(min_decay, max_decay, D_HYENA, dtype=f32))
    h = h * jnp.exp(-t[:, :, None, None] * deltas)
    h_fwd, h_bwd = h[:, 0], h[:, 1]
    k = jnp.concatenate([h_fwd, jnp.zeros_like(h_fwd[:1]), h_bwd[:0:-1]], axis=0)
    return k / jnp.sum(jnp.abs(k), axis=0, keepdims=True)


def fft_long_conv(u, k, bias):
    l = u.shape[1]
    uf = jnp.fft.rfft(u, n=2 * l, axis=1)
    kf = jnp.fft.rfft(k, n=2 * l, axis=0)
    y = jnp.fft.irfft(uf * kf[None], n=2 * l, axis=1)[:, :l]
    return y + u * bias


def hyena_mix(p, grid, conv_w, conv_b, k, f_bias):
    f32 = jnp.float32
    conv = short_conv_grid if grid else short_conv_seq
    q = conv(p.astype(f32), conv_w.astype(f32), conv_b.astype(f32))
    parts = jnp.split(q, HYENA_ORDER + 1, axis=-1)
    z = parts[0]
    for o in range(HYENA_ORDER):
        z = parts[o + 1] * fft_long_conv(z, k[:, o], f_bias[o].astype(f32))
    return z.astype(p.dtype)


def _ssm_combine(e1, e2):
    a1, b1 = e1
    a2, b2 = e2
    return a2 * a1, a2 * b1 + b2


def s5_discretize(a_re, a_im, log_dt, b_re, b_im):
    f32 = jnp.float32
    lam = lax.complex(a_re.astype(f32), a_im.astype(f32))
    dt = jnp.exp(log_dt.astype(f32))[:, None]
    a_bar = jnp.exp(lam * dt)
    b_c = lax.complex(b_re.astype(f32), b_im.astype(f32))
    b_bar = ((a_bar - 1.0) / lam)[..., None] * b_c
    return a_bar, b_bar


def s5_scan(bu, a_bar, s0, reverse):
    l = bu.shape[1]
    if s0 is not None:
        edge = l - 1 if reverse else 0
        bu = bu.at[:, edge].add(a_bar * s0)
    a = jnp.broadcast_to(a_bar, (1, l) + a_bar.shape)
    _, h = lax.associative_scan(_ssm_combine, (a, bu), axis=1, reverse=reverse)
    return h


def s5_trajectories(u, a_re, a_im, log_dt, b_re, b_im, s0_fwd, s0_bwd):
    b, l, _ = u.shape
    ug = u.astype(jnp.float32).reshape(b, l, S5_GROUPS, S5_GROUP)
    states = []
    for d, (s0, rev) in enumerate(((s0_fwd, False), (s0_bwd, True))):
        a_bar, b_bar = s5_discretize(a_re[d], a_im[d], log_dt[d], b_re[d], b_im[d])
        bu = lax.complex(jnp.einsum('blgh,gph->blgp', ug, b_bar.real),
                         jnp.einsum('blgh,gph->blgp', ug, b_bar.imag))
        states.append(s5_scan(bu, a_bar, s0, rev))
    return states[0], states[1]


def s5_readout(u, h_fwd, h_bwd, c_re, c_im, d_skip, glu_w, glu_b):
    f32 = jnp.float32
    b, l, _ = u.shape
    y = u.astype(f32) * d_skip.astype(f32)
    for d, h in enumerate((h_fwd, h_bwd)):
        yd = (jnp.einsum('blgp,ghp->blgh', h.real, c_re[d].astype(f32))
              - jnp.einsum('blgp,ghp->blgh', h.imag, c_im[d].astype(f32)))
        y = y + yd.reshape(b, l, D_S5)
    y = jax.nn.gelu(y)
    y = y * jax.nn.sigmoid(y @ glu_w.astype(f32) + glu_b.astype(f32))
    return y.astype(u.dtype)


def swiglu(h, w_gate, w_up, w_down):
    return (jax.nn.silu(h @ w_gate) * (h @ w_up)) @ w_down


def moe_swiglu(h, router_w, router_b, w_gate, w_up, w_down):
    logits = (h @ router_w).astype(jnp.float32) + router_b.astype(jnp.float32)
    top_vals, top_idx = lax.top_k(logits, TOP_K)
    top_gates = jax.nn.softmax(top_vals, axis=-1)
    combine = jnp.sum(top_gates[..., None] * jax.nn.one_hot(top_idx, N_EXPERTS, dtype=jnp.float32), axis=-2)
    out = jnp.zeros_like(h)
    for e in range(N_EXPERTS):
        out = out + combine[..., e:e + 1].astype(h.dtype) * swiglu(h, w_gate[e], w_up[e], w_down[e])
    return out


def setup_inputs(seed: int = 0) -> dict:
    key = jax.random.key(seed)
    keys = iter(jax.random.split(key, 64))
    f32 = jnp.float32

    def nrm(shape, std):
        return std * jax.random.normal(next(keys), shape, f32)

    def gain(shape):
        return 1.0 + 0.01 * jax.random.normal(next(keys), shape, f32)

    s5_n = jnp.arange(S5_STATE, dtype=f32)
    hy_proj = (HYENA_ORDER + 1) * D_HYENA
    return {
        'x': nrm((BATCH, SEQ, D_MODEL), 1.0),
        'c': nrm((BATCH, D_MODEL), 1.0),
        'ctx': nrm((BATCH, CTX_LEN, D_MODEL), 1.0),
        'c_ctx': nrm((D_MODEL,), 1.0),
        'ada_w': nrm((DEPTH, D_MODEL, N_MOD * D_MODEL), 0.5 / math.sqrt(D_MODEL)),
        'ada_b': nrm((DEPTH, N_MOD * D_MODEL), 0.01),
        'norm_mix_g': gain((DEPTH, D_MODEL)),
        'norm_ffn_g': gain((DEPTH, D_MODEL)),
        'w_in': nrm((DEPTH, D_MODEL, D_IN_PROJ), D_MODEL ** -0.5),
        'w_out': nrm((DEPTH, D_MIX, D_MODEL), D_MIX ** -0.5),
        'fnet_w': nrm((DEPTH, FNET_HEADS, FNET_HEAD_DIM, FNET_HEAD_DIM), FNET_HEAD_DIM ** -0.5),
        'hy_conv_w': nrm((DEPTH, 3, hy_proj), 3 ** -0.5),
        'hy_conv_b': nrm((DEPTH, hy_proj), 0.01),
        'hy_fw1': nrm((DEPTH, HYENA_EMB, HYENA_FILTER_HIDDEN), HYENA_EMB ** -0.5),
        'hy_fb1': nrm((DEPTH, HYENA_FILTER_HIDDEN), 0.1),
        'hy_freq1': gain((DEPTH, HYENA_FILTER_HIDDEN)),
        'hy_fw2': nrm((DEPTH, HYENA_FILTER_HIDDEN, HYENA_FILTER_HIDDEN), HYENA_FILTER_HIDDEN ** -0.5),
        'hy_fb2': nrm((DEPTH, HYENA_FILTER_HIDDEN), 0.1),
        'hy_freq2': gain((DEPTH, HYENA_FILTER_HIDDEN)),
        'hy_fw3': nrm((DEPTH, HYENA_FILTER_HIDDEN, 2 * HYENA_ORDER * D_HYENA), HYENA_FILTER_HIDDEN ** -0.5),
        'hy_bias': 1.0 + nrm((DEPTH, HYENA_ORDER, D_HYENA), 0.1),
        's5_a_re': -0.5 + nrm((DEPTH, 2, S5_GROUPS, S5_STATE), 0.01),
        's5_a_im': math.pi * s5_n + nrm((DEPTH, 2, S5_GROUPS, S5_STATE), 0.01),
        's5_log_dt': jax.random.uniform(next(keys), (DEPTH, 2, S5_GROUPS), f32,
                                        math.log(S5_DT_MIN), math.log(S5_DT_MAX)),
        's5_b_re': nrm((DEPTH, 2, S5_GROUPS, S5_STATE, S5_GROUP), (2 * S5_GROUP) ** -0.5),
        's5_b_im': nrm((DEPTH, 2, S5_GROUPS, S5_STATE, S5_GROUP), (2 * S5_GROUP) ** -0.5),
        's5_c_re': nrm((DEPTH, 2, S5_GROUPS, S5_GROUP, S5_STATE), 0.25),
        's5_c_im': nrm((DEPTH, 2, S5_GROUPS, S5_GROUP, S5_STATE), 0.25),
        's5_d': nrm((DEPTH, D_S5), 1.0),
        's5_glu_w': nrm((DEPTH, D_S5, D_S5), D_S5 ** -0.5),
        's5_glu_b': nrm((DEPTH, D_S5), 0.01),
        'ffn_w_gate': nrm((N_DENSE, D_MODEL, D_FF), D_MODEL ** -0.5),
        'ffn_w_up': nrm((N_DENSE, D_MODEL, D_FF), D_MODEL ** -0.5),
        'ffn_w_down': nrm((N_DENSE, D_FF, D_MODEL), D_FF ** -0.5),
        'moe_router_w': nrm((N_MOE, D_MODEL, N_EXPERTS), D_MODEL ** -0.5),
        'moe_router_b': nrm((N_MOE, N_EXPERTS), 0.01),
        'moe_w_gate': nrm((N_MOE, N_EXPERTS, D_MODEL, D_FF_EXPERT), D_MODEL ** -0.5),
        'moe_w_up': nrm((N_MOE, N_EXPERTS, D_MODEL, D_FF_EXPERT), D_MODEL ** -0.5),
        'moe_w_down': nrm((N_MOE, N_EXPERTS, D_FF_EXPERT, D_MODEL), D_FF_EXPERT ** -0.5),
        'final_g': gain((D_MODEL,)),
    }


def reference(x, c, ctx, c_ctx, ada_w, ada_b, norm_mix_g, norm_ffn_g, w_in, w_out, fnet_w,
              hy_conv_w, hy_conv_b, hy_fw1, hy_fb1, hy_freq1, hy_fw2, hy_fb2, hy_freq2, hy_fw3, hy_bias,
              s5_a_re, s5_a_im, s5_log_dt, s5_b_re, s5_b_im, s5_c_re, s5_c_im, s5_d, s5_glu_w, s5_glu_b,
              ffn_w_gate, ffn_w_up, ffn_w_down, moe_router_w, moe_router_b, moe_w_gate, moe_w_up,
              moe_w_down, final_g):

    def mix(l, p, grid, s0_fwd, s0_bwd):
        n_tok = p.shape[1]
        p_f, p_h, p_s = jnp.split(p, [HYENA_OFF, S5_OFF], axis=-1)
        y_f = fourier_mix(p_f, fnet_w[l])
        k = hyena_filters(n_tok, hy_fw1[l], hy_fb1[l], hy_freq1[l], hy_fw2[l], hy_fb2[l], hy_freq2[l], hy_fw3[l])
        y_h = hyena_mix(p_h, grid, hy_conv_w[l], hy_conv_b[l], k, hy_bias[l])
        h_fwd, h_bwd = s5_trajectories(p_s, s5_a_re[l], s5_a_im[l], s5_log_dt[l], s5_b_re[l], s5_b_im[l],
                                       s0_fwd, s0_bwd)
        y_s = s5_readout(p_s, h_fwd, h_bwd, s5_c_re[l], s5_c_im[l], s5_d[l], s5_glu_w[l], s5_glu_b[l])
        y = jnp.concatenate([y_f, y_h, y_s], axis=-1) @ w_out[l]
        return y, h_fwd, h_bwd

    def channel_mix(l, h):
        i = l // 2
        if l % 2 == 0:
            return swiglu(h, ffn_w_gate[i], ffn_w_up[i], ffn_w_down[i])
        return moe_swiglu(h, moe_router_w[i], moe_router_b[i], moe_w_gate[i], moe_w_up[i], moe_w_down[i])

    ctx_s = ctx
    for l in range(DEPTH):
        last = l == DEPTH - 1
        mod_lat = (jax.nn.silu(c) @ ada_w[l] + ada_b[l])[:, None, :]
        mod_ctx = (jax.nn.silu(c_ctx) @ ada_w[l] + ada_b[l])[None, None, :]
        sh_m, sc_m, g_m, sh_f, sc_f, g_f = jnp.split(mod_lat, N_MOD, axis=-1)
        csh_m, csc_m, cg_m, csh_f, csc_f, cg_f = jnp.split(mod_ctx, N_MOD, axis=-1)

        hc = modulate(rmsnorm(ctx_s, norm_mix_g[l]), csh_m, csc_m)
        if last:
            hf_c, hb_c = s5_trajectories(hc @ w_in[l][:, S5_OFF:], s5_a_re[l], s5_a_im[l], s5_log_dt[l],
                                         s5_b_re[l], s5_b_im[l], None, None)
        else:
            yc, hf_c, hb_c = mix(l, hc @ w_in[l], False, None, None)
            ctx_s = ctx_s + cg_m * yc
            hcf = modulate(rmsnorm(ctx_s, norm_ffn_g[l]), csh_f, csc_f)
            ctx_s = ctx_s + cg_f * channel_mix(l, hcf)

        hx = modulate(rmsnorm(x, norm_mix_g[l]), sh_m, sc_m)
        yx, _, _ = mix(l, hx @ w_in[l], True, hf_c[:, -1], hb_c[:, 0])
        x = x + g_m * yx
        hxf = modulate(rmsnorm(x, norm_ffn_g[l]), sh_f, sc_f)
        x = x + g_f * channel_mix(l, hxf)

    return rmsnorm(x, final_g)
```

```python
import functools
import math

import numpy as np
import jax
import jax.numpy as jnp
from jax import lax
from jax.experimental import pallas as pl
from jax.experimental.pallas import tpu as pltpu

F32 = jnp.float32
BF16 = jnp.bfloat16
HI = lax.Precision.HIGHEST

LANES = 128
GRID_ROW = 64
FNET_HEADS, FNET_HEAD_DIM = 4, 64
D_FNET = FNET_HEADS * FNET_HEAD_DIM
D_HYENA = 384
HYENA_ORDER = 2
HYENA_EMB = 33
HYENA_BANDS = (HYENA_EMB - 1) // 2
HYENA_FAST_DECAY_PCT, HYENA_SLOW_DECAY_PCT, HYENA_DECAY_TARGET = 0.3, 1.5, 0.01
S5_GROUP, S5_GROUPS, S5_STATE = 16, 24, 64
D_S5 = S5_GROUP * S5_GROUPS
HY_OFF = D_FNET
S5_OFF = D_FNET + (HYENA_ORDER + 1) * D_HYENA
N_MOD = 6
TOP_K = 2
RMS_EPS = 1e-6
S5_CHUNK = 8
GROUPS_PER_BLOCK = LANES // S5_GROUP
N_S5_BLOCKS = D_S5 // LANES
STATE_HALF = GROUPS_PER_BLOCK * S5_STATE
NEG_BIG = float(np.finfo(np.float32).min)
VMEM_LIMIT = 56 * 1024 * 1024


def _cp(sem, vmem=None):
    return pltpu.CompilerParams(dimension_semantics=sem, vmem_limit_bytes=vmem)


def _dot(a, b):
    return jnp.dot(a, b, preferred_element_type=F32)


def _split_bf16(a):
    hi = a.astype(BF16)
    lo = (a - hi.astype(F32)).astype(BF16)
    return hi, lo


def _sigmoid(x):
    return 1.0 / (1.0 + jnp.exp(-x))


def _rms_mod(x, g, sh, sc):
    ms = jnp.mean(x * x, axis=-1, keepdims=True)
    return (x * lax.rsqrt(ms + RMS_EPS) * g) * (1.0 + sc) + sh


def _ada_kernel(c_ref, w_ref, b_ref, o_ref):
    c = c_ref[...]
    s = c * _sigmoid(c)
    s_hi, s_lo = _split_bf16(s)
    w_hi, w_lo = _split_bf16(w_ref[...])
    o_ref[...] = _dot(s_hi, w_hi) + _dot(s_hi, w_lo) + _dot(s_lo, w_hi) + b_ref[...]


def _ada(cc, w, b):
    r, d = cc.shape
    n = w.shape[1]
    tn = 512
    return pl.pallas_call(
        _ada_kernel,
        out_shape=jax.ShapeDtypeStruct((r, n), F32),
        grid=(n // tn,),
        in_specs=[pl.BlockSpec((r, d), lambda j: (0, 0)),
                  pl.BlockSpec((d, tn), lambda j: (0, j)),
                  pl.BlockSpec((1, tn), lambda j: (0, j))],
        out_specs=pl.BlockSpec((r, tn), lambda j: (0, j)),
        compiler_params=_cp(("parallel",)),
        name="ada_mod",
    )(cc, w, b.reshape(1, n))


def _expand_kernel(a1_ref, a2_ref, b1_ref, b2_ref, o_ref, *, nh):
    a2 = a2_ref[...]
    b2 = b2_ref[...]
    for h in range(nh):
        blk = a1_ref[:, h:h + 1] * a2 + b1_ref[:, h:h + 1] * b2
        o_ref[:, h * LANES:(h + 1) * LANES] = blk.astype(o_ref.dtype)


def _expand(a1, a2, b1, b2):
    r, nh = a1.shape
    tr = min(r, 256)
    return pl.pallas_call(
        functools.partial(_expand_kernel, nh=nh),
        out_shape=jax.ShapeDtypeStruct((r, nh * LANES), BF16),
        grid=(r // tr,),
        in_specs=[pl.BlockSpec((tr, nh), lambda i: (i, 0)),
                  pl.BlockSpec((tr, LANES), lambda i: (i, 0)),
                  pl.BlockSpec((tr, nh), lambda i: (i, 0)),
                  pl.BlockSpec((tr, LANES), lambda i: (i, 0))],
        out_specs=pl.BlockSpec((tr, nh * LANES), lambda i: (i, 0)),
        compiler_params=_cp(("parallel",)),
        name="dft_expand",
    )(a1, a2, b1, b2)


def _trig_tables(n_rows, period):
    nh = n_rows // LANES
    r = np.arange(n_rows, dtype=np.int64)[:, None]
    hi = (r * (np.arange(nh, dtype=np.int64)[None, :] * LANES)) % period
    lo = (r * np.arange(LANES, dtype=np.int64)[None, :]) % period
    ang_hi = 2.0 * np.pi * hi / period
    ang_lo = 2.0 * np.pi * lo / period
    return np.cos(ang_hi), np.sin(ang_hi), np.cos(ang_lo), np.sin(ang_lo)


def _dft_cos_sin(n, period):
    c1, s1, c2, s2 = (jnp.asarray(t, F32) for t in _trig_tables(n, period))
    return _expand(c1, c2, -s1, s2), _expand(s1, c2, c1, s2)


def _hyena_dft(n):
    c1, s1, c2, s2 = _trig_tables(n, 2 * n)
    a1, a2, b1 = -s1, c2.copy(), -c1
    a1[0, :] = 1.0
    a2[0, :] = (-1.0) ** np.arange(LANES)
    b1[0, :] = 0.0
    f = lambda t: jnp.asarray(t, F32)
    mc = _expand(f(c1), f(c2), f(-s1), f(s2))
    sf = _expand(f(a1), f(a2), f(b1), f(s2))
    return mc, sf


def _inproj_kernel(x_ref, g_ref, sh_ref, sc_ref, w_ref, cw_ref, cb_ref, *out_refs, tm, grid_mode, full):
    h = _rms_mod(x_ref[...], g_ref[...], sh_ref[...], sc_ref[...])
    acc = _dot(h.astype(BF16), w_ref[...])
    if not full:
        out_refs[0][...] = acc
        return
    pf_ref, qv_ref, qx1_ref, qx2_ref, ps_ref = out_refs
    pf_ref[...] = acc[:, :HY_OFF]
    ps_ref[...] = acc[:, S5_OFF:]
    hy = acc[:, HY_OFF:S5_OFF]
    row = lax.broadcasted_iota(jnp.int32, (tm, 1), 0)
    if grid_mode:
        first = (row % GRID_ROW) == 0
        last = (row % GRID_ROW) == GRID_ROW - 1
    else:
        first = row == 0
        last = row == tm - 1
    prev = jnp.where(first, 0.0, pltpu.roll(hy, 1, 0))
    nxt = jnp.where(last, 0.0, pltpu.roll(hy, tm - 1, 0))
    q = prev * cw_ref[0:1, :] + hy * cw_ref[1:2, :] + nxt * cw_ref[2:3, :] + cb_ref[...]
    qv_ref[...] = q[:, :D_HYENA]
    qx1_ref[...] = q[:, D_HYENA:2 * D_HYENA]
    qx2_ref[...] = q[:, 2 * D_HYENA:]


def _inproj(x, g, sh, sc, w, cw, cb, *, grid_mode, full):
    b, l, d = x.shape
    n = w.shape[1]
    tm = min(l, 512)
    if not grid_mode:
        assert tm == l
    else:
        assert tm % GRID_ROW == 0
    row = lambda bi, i: (bi, i, 0)
    vec = lambda bi, i: (bi, 0, 0)
    const = lambda bi, i: (0, 0)
    widths = (D_FNET, D_HYENA, D_HYENA, D_HYENA, D_S5) if full else (n,)
    outs = tuple(jax.ShapeDtypeStruct((b, l, wd), F32) for wd in widths)
    return pl.pallas_call(
        functools.partial(_inproj_kernel, tm=tm, grid_mode=grid_mode, full=full),
        out_shape=outs,
        grid=(b, l // tm),
        in_specs=[pl.BlockSpec((None, tm, d), row),
                  pl.BlockSpec((1, d), const),
                  pl.BlockSpec((None, 1, d), vec),
                  pl.BlockSpec((None, 1, d), vec),
                  pl.BlockSpec((d, n), const),
                  pl.BlockSpec(cw.shape, const),
                  pl.BlockSpec(cb.shape, const)],
        out_specs=tuple(pl.BlockSpec((None, tm, wd), row) for wd in widths),
        compiler_params=_cp(("parallel", "parallel"), VMEM_LIMIT),
        name="inproj",
    )(x, g, sh, sc, w, cw, cb)


def _fnet_kernel(x_ref, mc_ref, ms_ref, a_ref, b_ref, o_ref, accc, accs, *, nb):
    k = pl.program_id(2)

    @pl.when(k == 0)
    def _():
        accc[...] = jnp.zeros_like(accc)
        accs[...] = jnp.zeros_like(accs)

    mc = mc_ref[...]
    ms = ms_ref[...]
    for bi in range(nb):
        xb = x_ref[bi].astype(BF16)
        accc[bi] += _dot(mc, xb)
        accs[bi] += _dot(ms, xb)

    @pl.when(k == pl.num_programs(2) - 1)
    def _():
        for bi in range(nb):
            o_ref[bi] = (_dot(accc[bi].astype(BF16), a_ref[...])
                         + _dot(accs[bi].astype(BF16), b_ref[...]))


def _fnet(pf, mc, ms, amat, bmat):
    b, l, c = pf.shape
    nb = min(b, 2)
    t = min(l, 1024)
    return pl.pallas_call(
        functools.partial(_fnet_kernel, nb=nb),
        out_shape=jax.ShapeDtypeStruct((b, l, c), F32),
        grid=(b // nb, l // t, l // t),
        in_specs=[pl.BlockSpec((nb, t, c), lambda bb, i, k: (bb, k, 0)),
                  pl.BlockSpec((t, t), lambda bb, i, k: (i, k)),
                  pl.BlockSpec((t, t), lambda bb, i, k: (i, k)),
                  pl.BlockSpec((c, c), lambda bb, i, k: (0, 0)),
                  pl.BlockSpec((c, c), lambda bb, i, k: (0, 0))],
        out_specs=pl.BlockSpec((nb, t, c), lambda bb, i, k: (bb, i, 0)),
        scratch_shapes=[pltpu.VMEM((nb, t, c), F32), pltpu.VMEM((nb, t, c), F32)],
        compiler_params=_cp(("parallel", "parallel", "arbitrary"), VMEM_LIMIT),
        name="fnet_dft",
    )(pf, mc, ms, amat, bmat)


def _hy_fwd_kernel(z_ref, mc_ref, sf_ref, kr_ref, ki_ref, kd_ref, pr_ref, pi_ref, accr, acci, *, nb):
    k = pl.program_id(2)

    @pl.when(k == 0)
    def _():
        accr[...] = jnp.zeros_like(accr)
        acci[...] = jnp.zeros_like(acci)

    mc = mc_ref[...]
    sf = sf_ref[...]
    for bi in range(nb):
        zb = z_ref[bi].astype(BF16)
        accr[bi] += _dot(mc, zb)
        acci[bi] += _dot(sf, zb)

    @pl.when(k == pl.num_programs(2) - 1)
    def _():
        kr, ki, kd = kr_ref[...], ki_ref[...], kd_ref[...]
        for bi in range(nb):
            zr, zi = accr[bi], acci[bi]
            pr_ref[bi] = (zr * kr - zi * ki).astype(pr_ref.dtype)
            pi_ref[bi] = (zr * ki + zi * kd).astype(pi_ref.dtype)


def _hy_fwd(z, mc, sf, kr, ki, kd):
    b, l, c = z.shape
    nb = min(b, 2)
    t = min(l, 1024)
    zspec = pl.BlockSpec((nb, t, c), lambda bb, i, k: (bb, k, 0))
    mspec = pl.BlockSpec((t, t), lambda bb, i, k: (i, k))
    fspec = pl.BlockSpec((t, c), lambda bb, i, k: (i, 0))
    ospec = pl.BlockSpec((nb, t, c), lambda bb, i, k: (bb, i, 0))
    return pl.pallas_call(
        functools.partial(_hy_fwd_kernel, nb=nb),
        out_shape=(jax.ShapeDtypeStruct((b, l, c), BF16), jax.ShapeDtypeStruct((b, l, c), BF16)),
        grid=(b // nb, l // t, l // t),
        in_specs=[zspec, mspec, mspec, fspec, fspec, fspec],
        out_specs=(ospec, ospec),
        scratch_shapes=[pltpu.VMEM((nb, t, c), F32), pltpu.VMEM((nb, t, c), F32)],
        compiler_params=_cp(("parallel", "parallel", "arbitrary"), VMEM_LIMIT),
        name="hyena_fwd_dft",
    )(z, mc, sf, kr, ki, kd)


def _hy_inv_kernel(pr_ref, pi_ref, mc_ref, si_ref, xg_ref, zin_ref, bias_ref, o_ref, acc, *, nb):
    k = pl.program_id(2)

    @pl.when(k == 0)
    def _():
        acc[...] = jnp.zeros_like(acc)

    mc = mc_ref[...]
    si = si_ref[...]
    for bi in range(nb):
        acc[bi] += _dot(mc, pr_ref[bi]) + _dot(si, pi_ref[bi])

    @pl.when(k == pl.num_programs(2) - 1)
    def _():
        for bi in range(nb):
            o_ref[bi] = xg_ref[bi] * (acc[bi] + zin_ref[bi] * bias_ref[...])


def _hy_inv(pr, pi, mc, si, xg, zin, bias):
    b, l, c = xg.shape
    nb = min(b, 2)
    t = min(l, 1024)
    pspec = pl.BlockSpec((nb, t, c), lambda bb, i, k: (bb, k, 0))
    mspec = pl.BlockSpec((t, t), lambda bb, i, k: (i, k))
    ospec = pl.BlockSpec((nb, t, c), lambda bb, i, k: (bb, i, 0))
    return pl.pallas_call(
        functools.partial(_hy_inv_kernel, nb=nb),
        out_shape=jax.ShapeDtypeStruct((b, l, c), F32),
        grid=(b // nb, l // t, l // t),
        in_specs=[pspec, pspec, mspec, mspec, ospec, ospec,
                  pl.BlockSpec((1, c), lambda bb, i, k: (0, 0))],
        out_specs=ospec,
        scratch_shapes=[pltpu.VMEM((nb, t, c), F32)],
        compiler_params=_cp(("parallel", "parallel", "arbitrary"), VMEM_LIMIT),
        name="hyena_inv_dft",
    )(pr, pi, mc, si, xg, zin, bias.reshape(1, c))


def _cmul(a, h):
    ar, ai = a[:, :STATE_HALF], a[:, STATE_HALF:]
    hr, hi = h[:, :STATE_HALF], h[:, STATE_HALF:]
    return jnp.concatenate([ar * hr - ai * hi, ar * hi + ai * hr], axis=1)


def _s5_kernel(u_ref, s0f_ref, s0b_ref, mi_ref, mof_ref, mob_ref, mnf_ref, mnb_ref, atf_ref, atb_ref,
               d_ref, y_ref, hf_ref, hb_ref, x_scr, sf_scr, sb_scr, hinf_scr, hinb_scr, *, nc, t_chunk):
    for s in range(t_chunk):
        x_scr[:, s * LANES:(s + 1) * LANES] = u_ref[pl.ds(s, nc, stride=t_chunk), :].astype(BF16)
    x = x_scr[...]
    sf_scr[...] = _dot(x, mof_ref[...])
    sb_scr[...] = _dot(x, mob_ref[...])
    atf = atf_ref[...]
    atb = atb_ref[...]

    def body(c, carry):
        hf, hb = carry
        cb = nc - 1 - c
        hinf_scr[pl.ds(c, 1), :] = hf
        hinb_scr[pl.ds(cb, 1), :] = hb
        hf = _cmul(atf, hf) + sf_scr[pl.ds(c, 1), :]
        hb = _cmul(atb, hb) + sb_scr[pl.ds(cb, 1), :]
        return hf, hb

    hf, hb = lax.fori_loop(0, nc, body, (s0f_ref[...], s0b_ref[...]))
    hf_ref[...] = hf
    hb_ref[...] = hb
    y = (_dot(x, mi_ref[...])
         + _dot(hinf_scr[...].astype(BF16), mnf_ref[...])
         + _dot(hinb_scr[...].astype(BF16), mnb_ref[...]))
    d = d_ref[...]
    for t in range(t_chunk):
        rows = pl.ds(t, nc, stride=t_chunk)
        y_ref[rows, :] = y[:, t * LANES:(t + 1) * LANES] + d * u_ref[rows, :]


def _s5(u, s0f, s0b, mats, d_skip):
    b, l, _ = u.shape
    tc = S5_CHUNK
    nc = l // tc
    kin = tc * LANES
    ns = 2 * STATE_HALF
    mi, mof, mob, mnf, mnb, atf, atb = mats
    st_spec = pl.BlockSpec((None, None, 1, ns), lambda j, bi: (bi, j, 0, 0))
    w3 = lambda s1, s2: pl.BlockSpec((None, s1, s2), lambda j, bi: (j, 0, 0))
    u_spec = pl.BlockSpec((None, l, LANES), lambda j, bi: (bi, 0, j))
    return pl.pallas_call(
        functools.partial(_s5_kernel, nc=nc, t_chunk=tc),
        out_shape=(jax.ShapeDtypeStruct((b, l, D_S5), F32),
                   jax.ShapeDtypeStruct((b, N_S5_BLOCKS, 1, ns), F32),
                   jax.ShapeDtypeStruct((b, N_S5_BLOCKS, 1, ns), F32)),
        grid=(N_S5_BLOCKS, b),
        in_specs=[u_spec, st_spec, st_spec,
                  w3(kin, kin), w3(kin, ns), w3(kin, ns), w3(ns, kin), w3(ns, kin),
                  w3(1, ns), w3(1, ns),
                  pl.BlockSpec((1, LANES), lambda j, bi: (0, j))],
        out_specs=(u_spec, st_spec, st_spec),
        scratch_shapes=[pltpu.VMEM((nc, kin), BF16),
                        pltpu.VMEM((nc, ns), F32), pltpu.VMEM((nc, ns), F32),
                        pltpu.VMEM((nc, ns), F32), pltpu.VMEM((nc, ns), F32)],
        compiler_params=_cp(("parallel", "parallel"), VMEM_LIMIT),
        name="s5_scan",
    )(u, s0f, s0b, mi, mof, mob, mnf, mnb, atf, atb, d_skip.reshape(1, D_S5))


def _s5_matrices(a_re, a_im, log_dt, b_re, b_im, c_re, c_im):
    tc = S5_CHUNK
    g, p, hd = S5_GROUPS, S5_STATE, S5_GROUP
    lam = lax.complex(a_re.astype(F32), a_im.astype(F32))
    dt = jnp.exp(log_dt.astype(F32))[..., None]
    a_bar = jnp.exp(lam * dt)
    b_bar = ((a_bar - 1.0) / lam)[..., None] * lax.complex(b_re.astype(F32), b_im.astype(F32))
    cc = lax.complex(c_re.astype(F32), c_im.astype(F32))
    ks = jnp.arange(tc + 1, dtype=F32)
    pw = jnp.exp((lam * dt)[:, :, None, :] * ks[None, None, :, None])
    kern = jnp.real(jnp.einsum('dgip,dgkp,dgpj->dgkij', cc, pw[:, :, :tc], b_bar, precision=HI))
    s_idx = jnp.arange(tc)[:, None]
    t_idx = jnp.arange(tc)[None, :]
    lag = t_idx - s_idx
    kf = jnp.where((lag >= 0)[None, :, :, None, None], kern[0][:, jnp.clip(lag, 0, tc - 1)], 0.0)
    kb = jnp.where((lag <= 0)[None, :, :, None, None], kern[1][:, jnp.clip(-lag, 0, tc - 1)], 0.0)
    ktot = kf + kb
    eye = jnp.eye(GROUPS_PER_BLOCK, dtype=F32)
    kt = ktot.reshape(N_S5_BLOCKS, GROUPS_PER_BLOCK, tc, tc, hd, hd)
    mi = jnp.einsum('Jgstij,gh->Jsgjthi', kt, eye).reshape(N_S5_BLOCKS, tc * LANES, tc * LANES)

    def parts(z):
        return jnp.stack([jnp.real(z), jnp.imag(z)], axis=0)

    def out_mat(d, powers):
        z = powers[:, :, :, None] * b_bar[d][:, None, :, :]
        z = parts(z).reshape(2, N_S5_BLOCKS, GROUPS_PER_BLOCK, tc, p, hd)
        m = jnp.einsum('cJgspj,gh->Jsgjchp', z, eye)
        return m.reshape(N_S5_BLOCKS, tc * LANES, 2 * STATE_HALF)

    def in_mat(d, powers):
        z = cc[d][:, None, :, :] * powers[:, :, None, :]
        z = jnp.stack([jnp.real(z), -jnp.imag(z)], axis=0)
        z = z.reshape(2, N_S5_BLOCKS, GROUPS_PER_BLOCK, tc, hd, p)
        m = jnp.einsum('cJgtip,gh->Jcgpthi', z, eye)
        return m.reshape(N_S5_BLOCKS, 2 * STATE_HALF, tc * LANES)

    rev = jnp.arange(tc - 1, -1, -1)
    mof = out_mat(0, pw[0][:, rev])
    mob = out_mat(1, pw[1][:, :tc])
    mnf = in_mat(0, pw[0][:, 1:tc + 1])
    mnb = in_mat(1, pw[1][:, tc - jnp.arange(tc)])

    def step(d):
        z = parts(pw[d][:, tc]).reshape(2, N_S5_BLOCKS, 1, STATE_HALF)
        return jnp.transpose(z, (1, 2, 0, 3)).reshape(N_S5_BLOCKS, 1, 2 * STATE_HALF)

    bf = lambda m: m.astype(BF16)
    return bf(mi), bf(mof), bf(mob), bf(mnf), bf(mnb), step(0), step(1)


def _outproj_kernel(yf_ref, yh_ref, ys_ref, x_ref, gate_ref, w1_ref, w2_ref, w3_ref, gw_ref, gb_ref, o_ref):
    y = ys_ref[...]
    y = 0.5 * y * (1.0 + jnp.tanh(math.sqrt(2.0 / math.pi) * (y + 0.044715 * (y * y * y))))
    y = y * _sigmoid(_dot(y.astype(BF16), gw_ref[...]) + gb_ref[...])
    acc = (_dot(yf_ref[...].astype(BF16), w1_ref[...])
           + _dot(yh_ref[...].astype(BF16), w2_ref[...])
           + _dot(y.astype(BF16), w3_ref[...]))
    o_ref[...] = x_ref[...] + gate_ref[...] * acc


def _outproj(yf, yh, ys, x, gate, w1, w2, w3, gw, gb):
    b, l, d = x.shape
    tm = min(l, 512)
    row = lambda bi, i: (bi, i, 0)
    const = lambda bi, i: (0, 0)
    full = lambda a: pl.BlockSpec(a.shape, const)
    return pl.pallas_call(
        _outproj_kernel,
        out_shape=jax.ShapeDtypeStruct((b, l, d), F32),
        grid=(b, l // tm),
        in_specs=[pl.BlockSpec((None, tm, yf.shape[2]), row),
                  pl.BlockSpec((None, tm, yh.shape[2]), row),
                  pl.BlockSpec((None, tm, ys.shape[2]), row),
                  pl.BlockSpec((None, tm, d), row),
                  pl.BlockSpec((None, 1, d), lambda bi, i: (bi, 0, 0)),
                  full(w1), full(w2), full(w3), full(gw), full(gb)],
        out_specs=pl.BlockSpec((None, tm, d), row),
        compiler_params=_cp(("parallel", "parallel"), VMEM_LIMIT),
        name="outproj",
    )(yf, yh, ys, x, gate, w1, w2, w3, gw, gb)


def _ffn_kernel(x_ref, g_ref, sh_ref, sc_ref, gate_ref, rwh_ref, rwl_ref, rb_ref, wg_ref, wu_ref, wd_ref,
                fg_ref, o_ref, xn_scr, acc_scr, comb_scr, *, moe, final_norm):
    e = pl.program_id(2)
    f = pl.program_id(3)
    first = jnp.logical_and(e == 0, f == 0)
    last = jnp.logical_and(e == pl.num_programs(2) - 1, f == pl.num_programs(3) - 1)

    @pl.when(first)
    def _():
        h = _rms_mod(x_ref[...], g_ref[...], sh_ref[...], sc_ref[...])
        xn_scr[...] = h.astype(BF16)
        acc_scr[...] = jnp.zeros_like(acc_scr)
        if moe:
            h_hi, h_lo = _split_bf16(h)
            rwh = rwh_ref[...]
            lg = _dot(h_hi, rwh) + _dot(h_hi, rwl_ref[...]) + _dot(h_lo, rwh) + rb_ref[...]
            lane = lax.broadcasted_iota(jnp.int32, lg.shape, 1)
            m1 = jnp.max(lg, axis=-1, keepdims=True)
            i1 = jnp.min(jnp.where(lg == m1, lane, LANES), axis=-1, keepdims=True)
            lg2 = jnp.where(lane == i1, NEG_BIG, lg)
            m2 = jnp.max(lg2, axis=-1, keepdims=True)
            i2 = jnp.min(jnp.where(lg2 == m2, lane, LANES), axis=-1, keepdims=True)
            ex = jnp.exp(m2 - m1)
            den = 1.0 + ex
            comb_scr[...] = (jnp.where(lane == i1, 1.0 / den, 0.0)
                             + jnp.where(lane == i2, ex / den, 0.0))

    xn = xn_scr[...]
    gt = _dot(xn, wg_ref[...])
    up = _dot(xn, wu_ref[...])
    hmid = gt * _sigmoid(gt) * up
    if moe:
        comb = comb_scr[...]
        lane = lax.broadcasted_iota(jnp.int32, comb.shape, 1)
        hmid = hmid * jnp.sum(jnp.where(lane == e, comb, 0.0), axis=-1, keepdims=True)
    acc_scr[...] += _dot(hmid.astype(BF16), wd_ref[...])

    @pl.when(last)
    def _():
        y = x_ref[...] + gate_ref[...] * acc_scr[...]
        if final_norm:
            ms = jnp.mean(y * y, axis=-1, keepdims=True)
            y = y * lax.rsqrt(ms + RMS_EPS) * fg_ref[...]
        o_ref[...] = y


def _ffn(x, g, sh, sc, gate, router, wg, wu, wd, final_g, *, tf):
    b, l, d = x.shape
    ne, _, ff = wg.shape
    moe = router is not None
    final_norm = final_g is not None
    tm = min(l, 1024)
    if moe:
        rwh, rwl, rb = router
    else:
        rwh = rwl = jnp.zeros((d, LANES), BF16)
        rb = jnp.zeros((1, LANES), F32)
    fg = final_g if final_norm else jnp.ones((1, d), F32)
    row = lambda bi, i, e, f: (bi, i, 0)
    vec = lambda bi, i, e, f: (bi, 0, 0)
    const = lambda bi, i, e, f: (0, 0)
    return pl.pallas_call(
        functools.partial(_ffn_kernel, moe=moe, final_norm=final_norm),
        out_shape=jax.ShapeDtypeStruct((b, l, d), F32),
        grid=(b, l // tm, ne, ff // tf),
        in_specs=[pl.BlockSpec((None, tm, d), row),
                  pl.BlockSpec((1, d), const),
                  pl.BlockSpec((None, 1, d), vec),
                  pl.BlockSpec((None, 1, d), vec),
                  pl.BlockSpec((None, 1, d), vec),
                  pl.BlockSpec((d, LANES), const),
                  pl.BlockSpec((d, LANES), const),
                  pl.BlockSpec((1, LANES), const),
                  pl.BlockSpec((None, d, tf), lambda bi, i, e, f: (e, 0, f)),
                  pl.BlockSpec((None, d, tf), lambda bi, i, e, f: (e, 0, f)),
                  pl.BlockSpec((None, tf, d), lambda bi, i, e, f: (e, f, 0)),
                  pl.BlockSpec((1, d), const)],
        out_specs=pl.BlockSpec((None, tm, d), row),
        scratch_shapes=[pltpu.VMEM((tm, d), BF16), pltpu.VMEM((tm, d), F32), pltpu.VMEM((tm, LANES), F32)],
        compiler_params=_cp(("parallel", "parallel", "arbitrary", "arbitrary"), VMEM_LIMIT),
        name="moe_ffn" if moe else "dense_ffn",
    )(x, g, sh, sc, gate, rwh, rwl, rb, wg, wu, wd, fg)


def _hyena_filters(l, fw1, fb1, freq1, fw2, fb2, freq2, fw3):
    t = jnp.linspace(0.0, 1.0, l, dtype=F32)[:, None]
    w = (2.0 * math.pi / l) * jnp.arange(l, dtype=F32)[:, None]
    bands = jnp.linspace(1e-4, HYENA_BANDS - 1, HYENA_BANDS, dtype=F32)[None, :]
    z = jnp.concatenate([t, jnp.cos(bands * w), -jnp.sin(bands * w)], axis=-1)
    h = jnp.sin(freq1.astype(F32) * (jnp.dot(z, fw1.astype(F32), precision=HI) + fb1.astype(F32)))
    h = jnp.sin(freq2.astype(F32) * (jnp.dot(h, fw2.astype(F32), precision=HI) + fb2.astype(F32)))
    h = jnp.dot(h, fw3.astype(F32), precision=HI).reshape(l, 2, HYENA_ORDER, D_HYENA)
    max_decay = math.log(HYENA_DECAY_TARGET) / HYENA_FAST_DECAY_PCT
    min_decay = math.log(HYENA_DECAY_TARGET) / HYENA_SLOW_DECAY_PCT
    deltas = jnp.abs(jnp.linspace(min_decay, max_decay, D_HYENA, dtype=F32))
    h = h * jnp.exp(-t[:, :, None, None] * deltas)
    h_fwd, h_bwd = h[:, 0], h[:, 1]
    k = jnp.concatenate([h_fwd, jnp.zeros_like(h_fwd[:1]), h_bwd[:0:-1]], axis=0)
    return k / jnp.sum(jnp.abs(k), axis=0, keepdims=True)


def _hyena_spectra(l, *filter_params):
    k = _hyena_filters(l, *filter_params)
    kf = jnp.fft.rfft(k, axis=0)
    n = 2 * l
    wk = jnp.where(jnp.arange(l) == 0, 1.0, 2.0).astype(F32)[:, None] / n
    out = []
    for o in range(HYENA_ORDER):
        re = jnp.real(kf[:l, o]).astype(F32)
        im = jnp.imag(kf[:l, o]).astype(F32)
        nyq = jnp.real(kf[l, o]).astype(F32)
        kr = re * wk
        ki = (im * wk).at[0].set(0.0)
        kd = kr.at[0].set(nyq / n)
        out.append((kr, ki, kd))
    return out


def _fnet_maps(w_f, l):
    d = FNET_HEAD_DIM
    idx = np.arange(d)
    ang = 2.0 * np.pi * ((idx[:, None] * idx[None, :]) % d) / d
    scale = 1.0 / math.sqrt(d * l)
    cd = jnp.asarray(np.cos(ang) * scale, F32)
    sd = jnp.asarray(np.sin(ang) * scale, F32)
    wf = w_f.astype(F32)
    a = jnp.einsum('de,hef->hdf', cd, wf, precision=HI)
    bm = -jnp.einsum('de,hef->hdf', sd, wf, precision=HI)
    eye = jnp.eye(FNET_HEADS, dtype=F32)
    blk = lambda m: jnp.einsum('hdf,hg->hdgf', m, eye).reshape(D_FNET, D_FNET).astype(BF16)
    return blk(a), blk(bm)


def kernel(x, c, ctx, c_ctx, ada_w, ada_b, norm_mix_g, norm_ffn_g, w_in, w_out, fnet_w, hy_conv_w, hy_conv_b,
           hy_fw1, hy_fb1, hy_freq1, hy_fw2, hy_fb2, hy_freq2, hy_fw3, hy_bias, s5_a_re, s5_a_im, s5_log_dt,
           s5_b_re, s5_b_im, s5_c_re, s5_c_im, s5_d, s5_glu_w, s5_glu_b, ffn_w_gate, ffn_w_up, ffn_w_down,
           moe_router_w, moe_router_b, moe_w_gate, moe_w_up, moe_w_down, final_g):
    b, l, d = x.shape
    lc = ctx.shape[1]
    depth = ada_w.shape[0]
    assert l % LANES == 0 and lc % LANES == 0 and d % LANES == 0

    dft = {}
    for n in sorted({l, lc}):
        mc, sf = _hyena_dft(n)
        dft[n] = dict(hy_c=mc, hy_sf=sf, hy_si=sf.T, fn=_dft_cos_sin(n, n))

    cc = jnp.concatenate([c, c_ctx[None, :]], axis=0).astype(F32)
    rows = -(-(b + 1) // 8) * 8
    cc = jnp.pad(cc, ((0, rows - (b + 1)), (0, 0)))

    def mix(li, xs, sh, sc, grid_mode, s0f, s0b, s5m, states_only):
        n_tok = xs.shape[1]
        g_mix = norm_mix_g[li].reshape(1, d)
        if states_only:
            (ps,) = _inproj(xs, g_mix, sh, sc, w_in[li][:, S5_OFF:].astype(BF16),
                            jnp.zeros((3, LANES), F32), jnp.zeros((1, LANES), F32),
                            grid_mode=grid_mode, full=False)
            _, hf, hb = _s5(ps, s0f, s0b, s5m, s5_d[li])
            return None, hf, hb
        pf, qv, qx1, qx2, ps = _inproj(
            xs, g_mix, sh, sc, w_in[li].astype(BF16), hy_conv_w[li].astype(F32),
            hy_conv_b[li].astype(F32).reshape(1, -1), grid_mode=grid_mode, full=True)
        tabs = dft[n_tok]
        amat, bmat = _fnet_maps(fnet_w[li], n_tok)
        yf = _fnet(pf, tabs['fn'][0], tabs['fn'][1], amat, bmat)
        spectra = _hyena_spectra(n_tok, hy_fw1[li], hy_fb1[li], hy_freq1[li], hy_fw2[li], hy_fb2[li],
                                 hy_freq2[li], hy_fw3[li])
        z = qv
        for o, xg in enumerate((qx1, qx2)):
            pr, pi = _hy_fwd(z, tabs['hy_c'], tabs['hy_sf'], *spectra[o])
            z = _hy_inv(pr, pi, tabs['hy_c'], tabs['hy_si'], xg, z, hy_bias[li][o].astype(F32))
        ys, hf, hb = _s5(ps, s0f, s0b, s5m, s5_d[li])
        return (yf, z, ys), hf, hb

    def out_proj(li, parts, xs, gate):
        wo = w_out[li].astype(BF16)
        return _outproj(parts[0], parts[1], parts[2], xs, gate,
                        wo[:D_FNET], wo[D_FNET:D_FNET + D_HYENA], wo[D_FNET + D_HYENA:],
                        s5_glu_w[li].astype(BF16), s5_glu_b[li].astype(F32).reshape(1, -1))

    def channel_mix(li, xs, sh, sc, gate, fin):
        i = li // 2
        g_ffn = norm_ffn_g[li].reshape(1, d)
        if li % 2 == 0:
            return _ffn(xs, g_ffn, sh, sc, gate, None, ffn_w_gate[i][None].astype(BF16),
                        ffn_w_up[i][None].astype(BF16), ffn_w_down[i][None].astype(BF16), fin, tf=1408)
        ne = moe_router_w.shape[-1]
        rw = jnp.pad(moe_router_w[i].astype(F32), ((0, 0), (0, LANES - ne)))
        rwh, rwl = _split_bf16(rw)
        rb = jnp.pad(moe_router_b[i].astype(F32), (0, LANES - ne), constant_values=NEG_BIG).reshape(1, LANES)
        return _ffn(xs, g_ffn, sh, sc, gate, (rwh, rwl, rb), moe_w_gate[i].astype(BF16),
                    moe_w_up[i].astype(BF16), moe_w_down[i].astype(BF16), fin, tf=896)

    zero_state = jnp.zeros((b, N_S5_BLOCKS, 1, 2 * STATE_HALF), F32)
    ctx_s = ctx
    for li in range(depth):
        last = li == depth - 1
        mod = _ada(cc, ada_w[li].astype(F32), ada_b[li].astype(F32))
        lat = [m.reshape(b, 1, d) for m in jnp.split(mod[:b], N_MOD, axis=-1)]
        cm = [jnp.broadcast_to(m.reshape(1, 1, d), (b, 1, d)) for m in jnp.split(mod[b:b + 1], N_MOD, axis=-1)]
        s5m = _s5_matrices(s5_a_re[li], s5_a_im[li], s5_log_dt[li], s5_b_re[li], s5_b_im[li],
                           s5_c_re[li], s5_c_im[li])

        parts, hf_c, hb_c = mix(li, ctx_s, cm[0], cm[1], False, zero_state, zero_state, s5m, last)
        if not last:
            ctx_s = out_proj(li, parts, ctx_s, cm[2])
            ctx_s = channel_mix(li, ctx_s, cm[3], cm[4], cm[5], None)

        parts, _, _ = mix(li, x, lat[0], lat[1], True, hf_c, hb_c, s5m, False)
        x = out_proj(li, parts, x, lat[2])
        x = channel_mix(li, x, lat[3], lat[4], lat[5], final_g.reshape(1, d) if last else None)

    if depth == 0:
        raise ValueError("depth must be positive")
    return x
```

```python
import functools
import math

import numpy as np
import jax
import jax.numpy as jnp
from jax import lax
from jax.experimental import pallas as pl
from jax.experimental.pallas import tpu as pltpu

F32 = jnp.float32
BF16 = jnp.bfloat16
HI = lax.Precision.HIGHEST

LANES = 128
GRID_ROW = 64
FNET_HEADS, FNET_HEAD_DIM = 4, 64
D_FNET = FNET_HEADS * FNET_HEAD_DIM
D_HYENA = 384
HYENA_ORDER = 2
HYENA_EMB = 33
HYENA_BANDS = (HYENA_EMB - 1) // 2
HYENA_FAST_DECAY_PCT, HYENA_SLOW_DECAY_PCT, HYENA_DECAY_TARGET = 0.3, 1.5, 0.01
S5_GROUP, S5_GROUPS, S5_STATE = 16, 24, 64
D_S5 = S5_GROUP * S5_GROUPS
HY_OFF = D_FNET
S5_OFF = D_FNET + (HYENA_ORDER + 1) * D_HYENA
N_MOD = 6
TOP_K = 2
RMS_EPS = 1e-6
S5_CHUNK = 8
GROUPS_PER_BLOCK = LANES // S5_GROUP
N_S5_BLOCKS = D_S5 // LANES
STATE_HALF = GROUPS_PER_BLOCK * S5_STATE
NEG_BIG = float(np.finfo(np.float32).min)
VMEM_LIMIT = 56 * 1024 * 1024


def _cp(sem, vmem=None):
    return pltpu.CompilerParams(dimension_semantics=sem, vmem_limit_bytes=vmem)


def _dot(a, b):
    return jnp.dot(a, b, preferred_element_type=F32)


def _split_bf16(a):
    hi = a.astype(BF16)
    lo = (a - hi.astype(F32)).astype(BF16)
    return hi, lo


def _sigmoid(x):
    return 1.0 / (1.0 + jnp.exp(-x))


def _rms_mod(x, g, sh, sc):
    ms = jnp.mean(x * x, axis=-1, keepdims=True)
    return (x * lax.rsqrt(ms + RMS_EPS) * g) * (1.0 + sc) + sh


def _ada_kernel(c_ref, w_ref, b_ref, o_ref):
    c = c_ref[...]
    s = c * _sigmoid(c)
    s_hi, s_lo = _split_bf16(s)
    w_hi, w_lo = _split_bf16(w_ref[...])
    o_ref[...] = _dot(s_hi, w_hi) + _dot(s_hi, w_lo) + _dot(s_lo, w_hi) + b_ref[...]


def _ada(cc, w, b):
    r, d = cc.shape
    n = w.shape[1]
    tn = 512
    return pl.pallas_call(
        _ada_kernel,
        out_shape=jax.ShapeDtypeStruct((r, n), F32),
        grid=(n // tn,),
        in_specs=[pl.BlockSpec((r, d), lambda j: (0, 0)),
                  pl.BlockSpec((d, tn), lambda j: (0, j)),
                  pl.BlockSpec((1, tn), lambda j: (0, j))],
        out_specs=pl.BlockSpec((r, tn), lambda j: (0, j)),
        compiler_params=_cp(("parallel",)),
        name="ada_mod",
    )(cc, w, b.reshape(1, n))


def _expand_kernel(a1_ref, a2_ref, b1_ref, b2_ref, o_ref, *, nh):
    a2 = a2_ref[...]
    b2 = b2_ref[...]
    for h in range(nh):
        blk = a1_ref[:, h:h + 1] * a2 + b1_ref[:, h:h + 1] * b2
        o_ref[:, h * LANES:(h + 1) * LANES] = blk.astype(o_ref.dtype)


def _expand(a1, a2, b1, b2):
    r, nh = a1.shape
    tr = min(r, 256)
    return pl.pallas_call(
        functools.partial(_expand_kernel, nh=nh),
        out_shape=jax.ShapeDtypeStruct((r, nh * LANES), BF16),
        grid=(r // tr,),
        in_specs=[pl.BlockSpec((tr, nh), lambda i: (i, 0)),
                  pl.BlockSpec((tr, LANES), lambda i: (i, 0)),
                  pl.BlockSpec((tr, nh), lambda i: (i, 0)),
                  pl.BlockSpec((tr, LANES), lambda i: (i, 0))],
        out_specs=pl.BlockSpec((tr, nh * LANES), lambda i: (i, 0)),
        compiler_params=_cp(("parallel",)),
        name="dft_expand",
    )(a1, a2, b1, b2)


def _trig_tables(n_rows, period):
    nh = n_rows // LANES
    r = np.arange(n_rows, dtype=np.int64)[:, None]
    hi = (r * (np.arange(nh, dtype=np.int64)[None, :] * LANES)) % period
    lo = (r * np.arange(LANES, dtype=np.int64)[None, :]) % period
    ang_hi = 2.0 * np.pi * hi / period
    ang_lo = 2.0 * np.pi * lo / period
    return np.cos(ang_hi), np.sin(ang_hi), np.cos(ang_lo), np.sin(ang_lo)


def _dft_cos_sin(n, period):
    c1, s1, c2, s2 = (jnp.asarray(t, F32) for t in _trig_tables(n, period))
    return _expand(c1, c2, -s1, s2), _expand(s1, c2, c1, s2)


def _hyena_dft(n):
    c1, s1, c2, s2 = _trig_tables(n, 2 * n)
    a1, a2, b1 = -s1, c2.copy(), -c1
    a1[0, :] = 1.0
    a2[0, :] = (-1.0) ** np.arange(LANES)
    b1[0, :] = 0.0
    f = lambda t: jnp.asarray(t, F32)
    mc = _expand(f(c1), f(c2), f(-s1), f(s2))
    sf = _expand(f(a1), f(a2), f(b1), f(s2))
    return mc, sf


def _inproj_kernel(x_ref, g_ref, sh_ref, sc_ref, w_ref, cw_ref, cb_ref, *out_refs, tm, grid_mode, full):
    h = _rms_mod(x_ref[...], g_ref[...], sh_ref[...], sc_ref[...])
    acc = _dot(h.astype(BF16), w_ref[...])
    if not full:
        out_refs[0][...] = acc
        return
    pf_ref, qv_ref, qx1_ref, qx2_ref, ps_ref = out_refs
    pf_ref[...] = acc[:, :HY_OFF]
    ps_ref[...] = acc[:, S5_OFF:]
    hy = acc[:, HY_OFF:S5_OFF]
    row = lax.broadcasted_iota(jnp.int32, (tm, 1), 0)
    if grid_mode:
        first = (row % GRID_ROW) == 0
        last = (row % GRID_ROW) == GRID_ROW - 1
    else:
        first = row == 0
        last = row == tm - 1
    prev = jnp.where(first, 0.0, pltpu.roll(hy, 1, 0))
    nxt = jnp.where(last, 0.0, pltpu.roll(hy, tm - 1, 0))
    q = prev * cw_ref[0:1, :] + hy * cw_ref[1:2, :] + nxt * cw_ref[2:3, :] + cb_ref[...]
    qv_ref[...] = q[:, :D_HYENA]
    qx1_ref[...] = q[:, D_HYENA:2 * D_HYENA]
    qx2_ref[...] = q[:, 2 * D_HYENA:]


def _inproj(x, g, sh, sc, w, cw, cb, *, grid_mode, full):
    b, l, d = x.shape
    n = w.shape[1]
    tm = min(l, 512)
    if not grid_mode:
        assert tm == l
    else:
        assert tm % GRID_ROW == 0
    row = lambda bi, i: (bi, i, 0)
    vec = lambda bi, i: (bi, 0, 0)
    const = lambda bi, i: (0, 0)
    widths = (D_FNET, D_HYENA, D_HYENA, D_HYENA, D_S5) if full else (n,)
    outs = tuple(jax.ShapeDtypeStruct((b, l, wd), F32) for wd in widths)
    return pl.pallas_call(
        functools.partial(_inproj_kernel, tm=tm, grid_mode=grid_mode, full=full),
        out_shape=outs,
        grid=(b, l // tm),
        in_specs=[pl.BlockSpec((None, tm, d), row),
                  pl.BlockSpec((1, d), const),
                  pl.BlockSpec((None, 1, d), vec),
                  pl.BlockSpec((None, 1, d), vec),
                  pl.BlockSpec((d, n), const),
                  pl.BlockSpec(cw.shape, const),
                  pl.BlockSpec(cb.shape, const)],
        out_specs=tuple(pl.BlockSpec((None, tm, wd), row) for wd in widths),
        compiler_params=_cp(("parallel", "parallel"), VMEM_LIMIT),
        name="inproj",
    )(x, g, sh, sc, w, cw, cb)


def _fnet_kernel(x_ref, mc_ref, ms_ref, a_ref, b_ref, o_ref, accc, accs, *, nb):
    k = pl.program_id(2)

    @pl.when(k == 0)
    def _():
        accc[...] = jnp.zeros_like(accc)
        accs[...] = jnp.zeros_like(accs)

    mc = mc_ref[...]
    ms = ms_ref[...]
    for bi in range(nb):
        xb = x_ref[bi].astype(BF16)
        accc[bi] += _dot(mc, xb)
        accs[bi] += _dot(ms, xb)

    @pl.when(k == pl.num_programs(2) - 1)
    def _():
        for bi in range(nb):
            o_ref[bi] = (_dot(accc[bi].astype(BF16), a_ref[...])
                         + _dot(accs[bi].astype(BF16), b_ref[...]))


def _fnet(pf, mc, ms, amat, bmat):
    b, l, c = pf.shape
    nb = min(b, 2)
    t = min(l, 1024)
    return pl.pallas_call(
        functools.partial(_fnet_kernel, nb=nb),
        out_shape=jax.ShapeDtypeStruct((b, l, c), F32),
        grid=(b // nb, l // t, l // t),
        in_specs=[pl.BlockSpec((nb, t, c), lambda bb, i, k: (bb, k, 0)),
                  pl.BlockSpec((t, t), lambda bb, i, k: (i, k)),
                  pl.BlockSpec((t, t), lambda bb, i, k: (i, k)),
                  pl.BlockSpec((c, c), lambda bb, i, k: (0, 0)),
                  pl.BlockSpec((c, c), lambda bb, i, k: (0, 0))],
        out_specs=pl.BlockSpec((nb, t, c), lambda bb, i, k: (bb, i, 0)),
        scratch_shapes=[pltpu.VMEM((nb, t, c), F32), pltpu.VMEM((nb, t, c), F32)],
        compiler_params=_cp(("parallel", "parallel", "arbitrary"), VMEM_LIMIT),
        name="fnet_dft",
    )(pf, mc, ms, amat, bmat)


def _hy_fwd_kernel(z_ref, mc_ref, sf_ref, kr_ref, ki_ref, kd_ref, pr_ref, pi_ref, accr, acci, *, nb):
    k = pl.program_id(2)

    @pl.when(k == 0)
    def _():
        accr[...] = jnp.zeros_like(accr)
        acci[...] = jnp.zeros_like(acci)

    mc = mc_ref[...]
    sf = sf_ref[...]
    for bi in range(nb):
        zb = z_ref[bi].astype(BF16)
        accr[bi] += _dot(mc, zb)
        acci[bi] += _dot(sf, zb)

    @pl.when(k == pl.num_programs(2) - 1)
    def _():
        kr, ki, kd = kr_ref[...], ki_ref[...], kd_ref[...]
        for bi in range(nb):
            zr, zi = accr[bi], acci[bi]
            pr_ref[bi] = (zr * kr - zi * ki).astype(pr_ref.dtype)
            pi_ref[bi] = (zr * ki + zi * kd).astype(pi_ref.dtype)


def _hy_fwd(z, mc, sf, kr, ki, kd):
    b, l, c = z.shape
    nb = min(b, 2)
    t = min(l, 1024)
    zspec = pl.BlockSpec((nb, t, c), lambda bb, i, k: (bb, k, 0))
    mspec = pl.BlockSpec((t, t), lambda bb, i, k: (i, k))
    fspec = pl.BlockSpec((t, c), lambda bb, i, k: (i, 0))
    ospec = pl.BlockSpec((nb, t, c), lambda bb, i, k: (bb, i, 0))
    return pl.pallas_call(
        functools.partial(_hy_fwd_kernel, nb=nb),
        out_shape=(jax.ShapeDtypeStruct((b, l, c), BF16), jax.ShapeDtypeStruct((b, l, c), BF16)),
        grid=(b // nb, l // t, l // t),
        in_specs=[zspec, mspec, mspec, fspec, fspec, fspec],
        out_specs=(ospec, ospec),
        scratch_shapes=[pltpu.VMEM((nb, t, c), F32), pltpu.VMEM((nb, t, c), F32)],
        compiler_params=_cp(("parallel", "parallel", "arbitrary"), VMEM_LIMIT),
        name="hyena_fwd_dft",
    )(z, mc, sf, kr, ki, kd)


def _hy_inv_kernel(pr_ref, pi_ref, mc_ref, si_ref, xg_ref, zin_ref, bias_ref, o_ref, acc, *, nb):
    k = pl.program_id(2)

    @pl.when(k == 0)
    def _():
        acc[...] = jnp.zeros_like(acc)

    mc = mc_ref[...]
    si = si_ref[...]
    for bi in range(nb):
        acc[bi] += _dot(mc, pr_ref[bi]) + _dot(si, pi_ref[bi])

    @pl.when(k == pl.num_programs(2) - 1)
    def _():
        for bi in range(nb):
            o_ref[bi] = xg_ref[bi] * (acc[bi] + zin_ref[bi] * bias_ref[...])


def _hy_inv(pr, pi, mc, si, xg, zin, bias):
    b, l, c = xg.shape
    nb = min(b, 2)
    t = min(l, 1024)
    pspec = pl.BlockSpec((nb, t, c), lambda bb, i, k: (bb, k, 0))
    mspec = pl.BlockSpec((t, t), lambda bb, i, k: (i, k))
    ospec = pl.BlockSpec((nb, t, c), lambda bb, i, k: (bb, i, 0))
    return pl.pallas_call(
        functools.partial(_hy_inv_kernel, nb=nb),
        out_shape=jax.ShapeDtypeStruct((b, l, c), F32),
        grid=(b // nb, l // t, l // t),
        in_specs=[pspec, pspec, mspec, mspec, ospec, ospec,
                  pl.BlockSpec((1, c), lambda bb, i, k: (0, 0))],
        out_specs=ospec,
        scratch_shapes=[pltpu.VMEM((nb, t, c), F32)],
        compiler_params=_cp(("parallel", "parallel", "arbitrary"), VMEM_LIMIT),
        name="hyena_inv_dft",
    )(pr, pi, mc, si, xg, zin, bias.reshape(1, c))


def _cmul(a, h):
    ar, ai = a[:, :STATE_HALF], a[:, STATE_HALF:]
    hr, hi = h[:, :STATE_HALF], h[:, STATE_HALF:]
    return jnp.concatenate([ar * hr - ai * hi, ar * hi + ai * hr], axis=1)


def _s5_kernel(u_ref, s0f_ref, s0b_ref, mi_ref, mof_ref, mob_ref, mnf_ref, mnb_ref, atf_ref, atb_ref,
               d_ref, y_ref, hf_ref, hb_ref, x_scr, sf_scr, sb_scr, hinf_scr, hinb_scr, *, nc, t_chunk):
    for s in range(t_chunk):
        x_scr[:, s * LANES:(s + 1) * LANES] = u_ref[pl.ds(s, nc, stride=t_chunk), :].astype(BF16)
    x = x_scr[...]
    sf_scr[...] = _dot(x, mof_ref[...])
    sb_scr[...] = _dot(x, mob_ref[...])
    atf = atf_ref[...]
    atb = atb_ref[...]

    def body(c, carry):
        hf, hb = carry
        cb = nc - 1 - c
        hinf_scr[pl.ds(c, 1), :] = hf
        hinb_scr[pl.ds(cb, 1), :] = hb
        hf = _cmul(atf, hf) + sf_scr[pl.ds(c, 1), :]
        hb = _cmul(atb, hb) + sb_scr[pl.ds(cb, 1), :]
        return hf, hb

    hf, hb = lax.fori_loop(0, nc, body, (s0f_ref[...], s0b_ref[...]))
    hf_ref[...] = hf
    hb_ref[...] = hb
    y = (_dot(x, mi_ref[...])
         + _dot(hinf_scr[...].astype(BF16), mnf_ref[...])
         + _dot(hinb_scr[...].astype(BF16), mnb_ref[...]))
    d = d_ref[...]
    for t in range(t_chunk):
        rows = pl.ds(t, nc, stride=t_chunk)
        y_ref[rows, :] = y[:, t * LANES:(t + 1) * LANES] + d * u_ref[rows, :]


def _s5(u, s0f, s0b, mats, d_skip):
    b, l, _ = u.shape
    tc = S5_CHUNK
    nc = l // tc
    kin = tc * LANES
    ns = 2 * STATE_HALF
    mi, mof, mob, mnf, mnb, atf, atb = mats
    st_spec = pl.BlockSpec((None, None, 1, ns), lambda j, bi: (bi, j, 0, 0))
    w3 = lambda s1, s2: pl.BlockSpec((None, s1, s2), lambda j, bi: (j, 0, 0))
    u_spec = pl.BlockSpec((None, l, LANES), lambda j, bi: (bi, 0, j))
    return pl.pallas_call(
        functools.partial(_s5_kernel, nc=nc, t_chunk=tc),
        out_shape=(jax.ShapeDtypeStruct((b, l, D_S5), F32),
                   jax.ShapeDtypeStruct((b, N_S5_BLOCKS, 1, ns), F32),
                   jax.ShapeDtypeStruct((b, N_S5_BLOCKS, 1, ns), F32)),
        grid=(N_S5_BLOCKS, b),
        in_specs=[u_spec, st_spec, st_spec,
                  w3(kin, kin), w3(kin, ns), w3(kin, ns), w3(ns, kin), w3(ns, kin),
                  w3(1, ns), w3(1, ns),
                  pl.BlockSpec((1, LANES), lambda j, bi: (0, j))],
        out_specs=(u_spec, st_spec, st_spec),
        scratch_shapes=[pltpu.VMEM((nc, kin), BF16),
                        pltpu.VMEM((nc, ns), F32), pltpu.VMEM((nc, ns), F32),
                        pltpu.VMEM((nc, ns), F32), pltpu.VMEM((nc, ns), F32)],
        compiler_params=_cp(("parallel", "parallel"), VMEM_LIMIT),
        name="s5_scan",
    )(u, s0f, s0b, mi, mof, mob, mnf, mnb, atf, atb, d_skip.reshape(1, D_S5))


def _s5_matrices(a_re, a_im, log_dt, b_re, b_im, c_re, c_im):
    tc = S5_CHUNK
    g, p, hd = S5_GROUPS, S5_STATE, S5_GROUP
    lam = lax.complex(a_re.astype(F32), a_im.astype(F32))
    dt = jnp.exp(log_dt.astype(F32))[..., None]
    a_bar = jnp.exp(lam * dt)
    b_bar = ((a_bar - 1.0) / lam)[..., None] * lax.complex(b_re.astype(F32), b_im.astype(F32))
    cc = lax.complex(c_re.astype(F32), c_im.astype(F32))
    ks = jnp.arange(tc + 1, dtype=F32)
    pw = jnp.exp((lam * dt)[:, :, None, :] * ks[None, None, :, None])
    kern = jnp.real(jnp.einsum('dgip,dgkp,dgpj->dgkij', cc, pw[:, :, :tc], b_bar, precision=HI))
    s_idx = jnp.arange(tc)[:, None]
    t_idx = jnp.arange(tc)[None, :]
    lag = t_idx - s_idx
    kf = jnp.where((lag >= 0)[None, :, :, None, None], kern[0][:, jnp.clip(lag, 0, tc - 1)], 0.0)
    kb = jnp.where((lag <= 0)[None, :, :, None, None], kern[1][:, jnp.clip(-lag, 0, tc - 1)], 0.0)
    ktot = kf + kb
    eye = jnp.eye(GROUPS_PER_BLOCK, dtype=F32)
    kt = ktot.reshape(N_S5_BLOCKS, GROUPS_PER_BLOCK, tc, tc, hd, hd)
    mi = jnp.einsum('Jgstij,gh->Jsgjthi', kt, eye).reshape(N_S5_BLOCKS, tc * LANES, tc * LANES)

    def parts(z):
        return jnp.stack([jnp.real(z), jnp.imag(z)], axis=0)

    def out_mat(d, powers):
        z = powers[:, :, :, None] * b_bar[d][:, None, :, :]
        z = parts(z).reshape(2, N_S5_BLOCKS, GROUPS_PER_BLOCK, tc, p, hd)
        m = jnp.einsum('cJgspj,gh->Jsgjchp', z, eye)
        return m.reshape(N_S5_BLOCKS, tc * LANES, 2 * STATE_HALF)

    def in_mat(d, powers):
        z = cc[d][:, None, :, :] * powers[:, :, None, :]
        z = jnp.stack([jnp.real(z), -jnp.imag(z)], axis=0)
        z = z.reshape(2, N_S5_BLOCKS, GROUPS_PER_BLOCK, tc, hd, p)
        m = jnp.einsum('cJgtip,gh->Jcgpthi', z, eye)
        return m.reshape(N_S5_BLOCKS, 2 * STATE_HALF, tc * LANES)

    rev = jnp.arange(tc - 1, -1, -1)
    mof = out_mat(0, pw[0][:, rev])
    mob = out_mat(1, pw[1][:, :tc])
    mnf = in_mat(0, pw[0][:, 1:tc + 1])
    mnb = in_mat(1, pw[1][:, tc - jnp.arange(tc)])

    def step(d):
        z = parts(pw[d][:, tc]).reshape(2, N_S5_BLOCKS, 1, STATE_HALF)
        return jnp.transpose(z, (1, 2, 0, 3)).reshape(N_S5_BLOCKS, 1, 2 * STATE_HALF)

    bf = lambda m: m.astype(BF16)
    return bf(mi), bf(mof), bf(mob), bf(mnf), bf(mnb), step(0), step(1)


def _outproj_kernel(yf_ref, yh_ref, ys_ref, x_ref, gate_ref, w1_ref, w2_ref, w3_ref, gw_ref, gb_ref, o_ref):
    y = ys_ref[...]
    y = 0.5 * y * (1.0 + jnp.tanh(math.sqrt(2.0 / math.pi) * (y + 0.044715 * (y * y * y))))
    y = y * _sigmoid(_dot(y.astype(BF16), gw_ref[...]) + gb_ref[...])
    acc = (_dot(yf_ref[...].astype(BF16), w1_ref[...])
           + _dot(yh_ref[...].astype(BF16), w2_ref[...])
           + _dot(y.astype(BF16), w3_ref[...]))
    o_ref[...] = x_ref[...] + gate_ref[...] * acc


def _outproj(yf, yh, ys, x, gate, w1, w2, w3, gw, gb):
    b, l, d = x.shape
    tm = min(l, 512)
    row = lambda bi, i: (bi, i, 0)
    const = lambda bi, i: (0, 0)
    full = lambda a: pl.BlockSpec(a.shape, const)
    return pl.pallas_call(
        _outproj_kernel,
        out_shape=jax.ShapeDtypeStruct((b, l, d), F32),
        grid=(b, l // tm),
        in_specs=[pl.BlockSpec((None, tm, yf.shape[2]), row),
                  pl.BlockSpec((None, tm, yh.shape[2]), row),
                  pl.BlockSpec((None, tm, ys.shape[2]), row),
                  pl.BlockSpec((None, tm, d), row),
                  pl.BlockSpec((None, 1, d), lambda bi, i: (bi, 0, 0)),
                  full(w1), full(w2), full(w3), full(gw), full(gb)],
        out_specs=pl.BlockSpec((None, tm, d), row),
        compiler_params=_cp(("parallel", "parallel"), VMEM_LIMIT),
        name="outproj",
    )(yf, yh, ys, x, gate, w1, w2, w3, gw, gb)


def _ffn_kernel(x_ref, g_ref, sh_ref, sc_ref, gate_ref, rwh_ref, rwl_ref, rb_ref, wg_ref, wu_ref, wd_ref,
                fg_ref, o_ref, xn_scr, acc_scr, comb_scr, *, moe, final_norm):
    e = pl.program_id(2)
    f = pl.program_id(3)
    first = jnp.logical_and(e == 0, f == 0)
    last = jnp.logical_and(e == pl.num_programs(2) - 1, f == pl.num_programs(3) - 1)

    @pl.when(first)
    def _():
        h = _rms_mod(x_ref[...], g_ref[...], sh_ref[...], sc_ref[...])
        xn_scr[...] = h.astype(BF16)
        acc_scr[...] = jnp.zeros_like(acc_scr)
        if moe:
            h_hi, h_lo = _split_bf16(h)
            rwh = rwh_ref[...]
            lg = _dot(h_hi, rwh) + _dot(h_hi, rwl_ref[...]) + _dot(h_lo, rwh) + rb_ref[...]
            lane = lax.broadcasted_iota(jnp.int32, lg.shape, 1)
            m1 = jnp.max(lg, axis=-1, keepdims=True)
            i1 = jnp.min(jnp.where(lg == m1, lane, LANES), axis=-1, keepdims=True)
            lg2 = jnp.where(lane == i1, NEG_BIG, lg)
            m2 = jnp.max(lg2, axis=-1, keepdims=True)
            i2 = jnp.min(jnp.where(lg2 == m2, lane, LANES), axis=-1, keepdims=True)
            ex = jnp.exp(m2 - m1)
            den = 1.0 + ex
            comb_scr[...] = (jnp.where(lane == i1, 1.0 / den, 0.0)
                             + jnp.where(lane == i2, ex / den, 0.0))

    xn = xn_scr[...]
    gt = _dot(xn, wg_ref[...])
    up = _dot(xn, wu_ref[...])
    hmid = gt * _sigmoid(gt) * up
    if moe:
        comb = comb_scr[...]
        lane = lax.broadcasted_iota(jnp.int32, comb.shape, 1)
        hmid = hmid * jnp.sum(jnp.where(lane == e, comb, 0.0), axis=-1, keepdims=True)
    acc_scr[...] += _dot(hmid.astype(BF16), wd_ref[...])

    @pl.when(last)
    def _():
        y = x_ref[...] + gate_ref[...] * acc_scr[...]
        if final_norm:
            ms = jnp.mean(y * y, axis=-1, keepdims=True)
            y = y * lax.rsqrt(ms + RMS_EPS) * fg_ref[...]
        o_ref[...] = y


def _ffn(x, g, sh, sc, gate, router, wg, wu, wd, final_g, *, tf):
    b, l, d = x.shape
    ne, _, ff = wg.shape
    moe = router is not None
    final_norm = final_g is not None
    tm = min(l, 1024)
    if moe:
        rwh, rwl, rb = router
    else:
        rwh = rwl = jnp.zeros((d, LANES), BF16)
        rb = jnp.zeros((1, LANES), F32)
    fg = final_g if final_norm else jnp.ones((1, d), F32)
    row = lambda bi, i, e, f: (bi, i, 0)
    vec = lambda bi, i, e, f: (bi, 0, 0)
    const = lambda bi, i, e, f: (0, 0)
    return pl.pallas_call(
        functools.partial(_ffn_kernel, moe=moe, final_norm=final_norm),
        out_shape=jax.ShapeDtypeStruct((b, l, d), F32),
        grid=(b, l // tm, ne, ff // tf),
        in_specs=[pl.BlockSpec((None, tm, d), row),
                  pl.BlockSpec((1, d), const),
                  pl.BlockSpec((None, 1, d), vec),
                  pl.BlockSpec((None, 1, d), vec),
                  pl.BlockSpec((None, 1, d), vec),
                  pl.BlockSpec((d, LANES), const),
                  pl.BlockSpec((d, LANES), const),
                  pl.BlockSpec((1, LANES), const),
                  pl.BlockSpec((None, d, tf), lambda bi, i, e, f: (e, 0, f)),
                  pl.BlockSpec((None, d, tf), lambda bi, i, e, f: (e, 0, f)),
                  pl.BlockSpec((None, tf, d), lambda bi, i, e, f: (e, f, 0)),
                  pl.BlockSpec((1, d), const)],
        out_specs=pl.BlockSpec((None, tm, d), row),
        scratch_shapes=[pltpu.VMEM((tm, d), BF16), pltpu.VMEM((tm, d), F32), pltpu.VMEM((tm, LANES), F32)],
        compiler_params=_cp(("parallel", "parallel", "arbitrary", "arbitrary"), VMEM_LIMIT),
        name="moe_ffn" if moe else "dense_ffn",
    )(x, g, sh, sc, gate, rwh, rwl, rb, wg, wu, wd, fg)


SLAB = 8
ROUTE_TM = 512
GROUP_TM = 1024


def _to_slabs(ref, val, n):
    for s in range(SLAB):
        ref[pl.ds(s, n, stride=SLAB), :] = val[:, s * LANES:(s + 1) * LANES]


def _from_slabs(ref, n):
    return jnp.concatenate([ref[pl.ds(s, n, stride=SLAB), :] for s in range(SLAB)], axis=1)


def _route_kernel(x_ref, g_ref, sh_ref, sc_ref, rwh_ref, rwl_ref, rb_ref, tri_ref,
                  xs_ref, meta_ref, gate_ref, cnt_ref, carry_scr, *, tm):
    step = pl.program_id(0) * pl.num_programs(1) + pl.program_id(1)

    @pl.when(step == 0)
    def _():
        carry_scr[...] = jnp.zeros_like(carry_scr)

    h = _rms_mod(x_ref[...], g_ref[...], sh_ref[...], sc_ref[...])
    _to_slabs(xs_ref, h, tm)
    h_hi, h_lo = _split_bf16(h)
    rwh = rwh_ref[...]
    lg = _dot(h_hi, rwh) + _dot(h_hi, rwl_ref[...]) + _dot(h_lo, rwh) + rb_ref[...]
    lane = lax.broadcasted_iota(jnp.int32, lg.shape, 1)
    m1 = jnp.max(lg, axis=-1, keepdims=True)
    i1 = jnp.min(jnp.where(lg == m1, lane, LANES), axis=-1, keepdims=True)
    lg2 = jnp.where(lane == i1, NEG_BIG, lg)
    m2 = jnp.max(lg2, axis=-1, keepdims=True)
    i2 = jnp.min(jnp.where(lg2 == m2, lane, LANES), axis=-1, keepdims=True)
    ex = jnp.exp(m2 - m1)
    den = 1.0 + ex
    gate_ref[...] = jnp.where(lane == 0, 1.0 / den, jnp.where(lane == 1, ex / den, 0.0))
    sel1 = lane == i1
    sel2 = lane == i2
    tri = tri_ref[...]
    p1 = _dot(tri, sel1.astype(BF16))
    p2 = _dot(tri, sel2.astype(BF16))
    tot1 = p1[tm - 1:tm, :]
    tot2 = p2[tm - 1:tm, :]
    carry = carry_scr[...]
    r1 = jnp.sum(jnp.where(sel1, carry + p1 - 1.0, 0.0), axis=-1, keepdims=True)
    r2 = jnp.sum(jnp.where(sel2, carry + tot1 + p2 - 1.0, 0.0), axis=-1, keepdims=True)
    carry = carry + tot1 + tot2
    carry_scr[...] = carry
    cnt_ref[...] = carry.astype(jnp.int32)
    meta_ref[...] = jnp.where(lane == 0, i1, jnp.where(lane == 1, i2, jnp.where(
        lane == 2, r1.astype(jnp.int32), jnp.where(lane == 3, r2.astype(jnp.int32), 0))))


def _route(x, g, sh, sc, rwh, rwl, rb):
    b, l, d = x.shape
    tm = min(l, ROUTE_TM)
    n = b * l
    nl = l // tm
    tri = jnp.tril(jnp.ones((tm, tm), F32)).astype(BF16)
    row = lambda bi, i: (bi, i, 0)
    vec = lambda bi, i: (bi, 0, 0)
    const = lambda bi, i: (0, 0)
    flat = lambda bi, i: (bi * nl + i, 0)
    return pl.pallas_call(
        functools.partial(_route_kernel, tm=tm),
        out_shape=(jax.ShapeDtypeStruct((n * SLAB, LANES), F32),
                   jax.ShapeDtypeStruct((n, LANES), jnp.int32),
                   jax.ShapeDtypeStruct((n, LANES), F32),
                   jax.ShapeDtypeStruct((1, LANES), jnp.int32)),
        grid=(b, nl),
        in_specs=[pl.BlockSpec((None, tm, d), row),
                  pl.BlockSpec((1, d), const),
                  pl.BlockSpec((None, 1, d), vec),
                  pl.BlockSpec((None, 1, d), vec),
                  pl.BlockSpec((d, LANES), const),
                  pl.BlockSpec((d, LANES), const),
                  pl.BlockSpec((1, LANES), const),
                  pl.BlockSpec((tm, tm), const)],
        out_specs=(pl.BlockSpec((tm * SLAB, LANES), flat),
                   pl.BlockSpec((tm, LANES), flat),
                   pl.BlockSpec((tm, LANES), flat),
                   pl.BlockSpec((1, LANES), const)),
        scratch_shapes=[pltpu.VMEM((1, LANES), F32)],
        compiler_params=_cp(("arbitrary", "arbitrary"), VMEM_LIMIT),
        name="moe_route",
    )(x, g, sh, sc, rwh, rwl, rb, tri)


def _dispatch_kernel(pos_hbm, xs_ref, init_hbm, out_hbm, idx_smem, isem, sem, *, tm):
    del init_hbm
    step = pl.program_id(0)
    icopy = pltpu.make_async_copy(pos_hbm.at[pl.ds(step * 2 * tm, 2 * tm)], idx_smem, isem)
    icopy.start()
    icopy.wait()

    def row_copy(i, k):
        return pltpu.make_async_copy(xs_ref.at[pl.ds(i * SLAB, SLAB), :], out_hbm.at[idx_smem[2 * i + k]], sem)

    def issue(i, carry):
        row_copy(i, 0).start()
        row_copy(i, 1).start()
        return carry

    def drain(i, carry):
        row_copy(i, 0).wait()
        row_copy(i, 1).wait()
        return carry

    lax.fori_loop(0, tm, issue, 0, unroll=8)
    lax.fori_loop(0, tm, drain, 0, unroll=8)


def _dispatch(xs, pos, n_rows):
    n = xs.shape[0] // SLAB
    tm = min(n, ROUTE_TM)
    init = jnp.zeros((n_rows, SLAB, LANES), F32)
    return pl.pallas_call(
        functools.partial(_dispatch_kernel, tm=tm),
        out_shape=jax.ShapeDtypeStruct((n_rows, SLAB, LANES), F32),
        grid=(n // tm,),
        in_specs=[pl.BlockSpec(memory_space=pl.ANY),
                  pl.BlockSpec((tm * SLAB, LANES), lambda i: (i, 0)),
                  pl.BlockSpec(memory_space=pl.ANY)],
        out_specs=pl.BlockSpec(memory_space=pl.ANY),
        scratch_shapes=[pltpu.SMEM((2 * tm,), jnp.int32), pltpu.SemaphoreType.DMA, pltpu.SemaphoreType.DMA],
        input_output_aliases={2: 0},
        compiler_params=_cp(("arbitrary",), VMEM_LIMIT),
        name="moe_dispatch",
    )(pos.reshape(-1), xs, init)


def _group_ffn_kernel(te_ref, x_ref, wg_ref, wu_ref, wd_ref, o_ref, xn_scr, acc_scr, *, tm, nt):
    t = pl.program_id(0)
    f = pl.program_id(1)
    valid = t < te_ref[nt]
    last = f == pl.num_programs(1) - 1

    @pl.when(jnp.logical_and(valid, f == 0))
    def _():
        xn_scr[...] = _from_slabs(x_ref, tm).astype(BF16)
        acc_scr[...] = jnp.zeros_like(acc_scr)

    @pl.when(valid)
    def _():
        xn = xn_scr[...]
        gt = _dot(xn, wg_ref[...])
        up = _dot(xn, wu_ref[...])
        acc_scr[...] += _dot((gt * _sigmoid(gt) * up).astype(BF16), wd_ref[...])

    @pl.when(jnp.logical_and(valid, last))
    def _():
        _to_slabs(o_ref, acc_scr[...], tm)

    @pl.when(jnp.logical_and(jnp.logical_not(valid), last))
    def _():
        o_ref[...] = jnp.zeros_like(o_ref)


def _group_ffn(xs, tile_expert, wg, wu, wd, *, tf):
    n_rows = xs.shape[0]
    tm = GROUP_TM
    nt = n_rows // tm
    _, d, ff = wg.shape
    x2 = xs.reshape(n_rows * SLAB, LANES)
    rows = pl.BlockSpec((tm * SLAB, LANES), lambda t, f, te: (t, 0))
    out = pl.pallas_call(
        functools.partial(_group_ffn_kernel, tm=tm, nt=nt),
        out_shape=jax.ShapeDtypeStruct((n_rows * SLAB, LANES), F32),
        grid_spec=pltpu.PrefetchScalarGridSpec(
            num_scalar_prefetch=1,
            grid=(nt, ff // tf),
            in_specs=[rows,
                      pl.BlockSpec((None, d, tf), lambda t, f, te: (te[t], 0, f)),
                      pl.BlockSpec((None, d, tf), lambda t, f, te: (te[t], 0, f)),
                      pl.BlockSpec((None, tf, d), lambda t, f, te: (te[t], f, 0))],
            out_specs=rows,
            scratch_shapes=[pltpu.VMEM((tm, d), BF16), pltpu.VMEM((tm, d), F32)]),
        compiler_params=_cp(("arbitrary", "arbitrary"), VMEM_LIMIT),
        name="moe_group_ffn",
    )(tile_expert, x2, wg, wu, wd)
    return out.reshape(n_rows, SLAB, LANES)


def _combine_kernel(pos_hbm, ys_hbm, x_ref, gate_ref, w_ref, fg_ref, o_ref, idx_smem, ybuf, isem, sem,
                    *, tm, final_norm):
    step = pl.program_id(0) * pl.num_programs(1) + pl.program_id(1)
    icopy = pltpu.make_async_copy(pos_hbm.at[pl.ds(step * 2 * tm, 2 * tm)], idx_smem, isem)
    icopy.start()
    icopy.wait()

    def row_copy(i, k):
        return pltpu.make_async_copy(ys_hbm.at[idx_smem[2 * i + k]], ybuf.at[k, pl.ds(i * SLAB, SLAB), :], sem)

    def issue(i, carry):
        row_copy(i, 0).start()
        row_copy(i, 1).start()
        return carry

    def drain(i, carry):
        row_copy(i, 0).wait()
        row_copy(i, 1).wait()
        return carry

    lax.fori_loop(0, tm, issue, 0, unroll=8)
    lax.fori_loop(0, tm, drain, 0, unroll=8)
    w = w_ref[...]
    moe = w[:, 0:1] * _from_slabs(ybuf.at[0], tm) + w[:, 1:2] * _from_slabs(ybuf.at[1], tm)
    y = x_ref[...] + gate_ref[...] * moe
    if final_norm:
        ms = jnp.mean(y * y, axis=-1, keepdims=True)
        y = y * lax.rsqrt(ms + RMS_EPS) * fg_ref[...]
    o_ref[...] = y


def _combine(ys, pos, x, gate, w, final_g):
    b, l, d = x.shape
    tm = min(l, ROUTE_TM)
    nl = l // tm
    final_norm = final_g is not None
    fg = final_g if final_norm else jnp.ones((1, d), F32)
    row = lambda bi, i: (bi, i, 0)
    return pl.pallas_call(
        functools.partial(_combine_kernel, tm=tm, final_norm=final_norm),
        out_shape=jax.ShapeDtypeStruct((b, l, d), F32),
        grid=(b, nl),
        in_specs=[pl.BlockSpec(memory_space=pl.ANY),
                  pl.BlockSpec(memory_space=pl.ANY),
                  pl.BlockSpec((None, tm, d), row),
                  pl.BlockSpec((None, 1, d), lambda bi, i: (bi, 0, 0)),
                  pl.BlockSpec((tm, LANES), lambda bi, i: (bi * nl + i, 0)),
                  pl.BlockSpec((1, d), lambda bi, i: (0, 0))],
        out_specs=pl.BlockSpec((None, tm, d), row),
        scratch_shapes=[pltpu.SMEM((2 * tm,), jnp.int32), pltpu.VMEM((2, tm * SLAB, LANES), F32),
                        pltpu.SemaphoreType.DMA, pltpu.SemaphoreType.DMA],
        compiler_params=_cp(("arbitrary", "arbitrary"), VMEM_LIMIT),
        name="moe_combine",
    )(pos.reshape(-1), ys, x, gate, w, fg)


def _moe(x, g, sh, sc, gate, router_w, router_b, wg, wu, wd, final_g):
    b, l, d = x.shape
    n = b * l
    ne = router_w.shape[-1]
    rw = jnp.pad(router_w.astype(F32), ((0, 0), (0, LANES - ne)))
    rwh, rwl = _split_bf16(rw)
    rb = jnp.pad(router_b.astype(F32), (0, LANES - ne), constant_values=NEG_BIG).reshape(1, LANES)
    xs, meta, w, counts = _route(x, g, sh, sc, rwh, rwl, rb)
    counts = counts[0, :ne]
    padded = (counts + GROUP_TM - 1) // GROUP_TM * GROUP_TM
    ends = jnp.cumsum(padded)
    offs = ends - padded
    experts = jnp.arange(ne, dtype=jnp.int32)
    off_of = lambda e: jnp.sum(jnp.where(e[:, None] == experts[None, :], offs[None, :], 0), axis=1)
    pos = jnp.stack([off_of(meta[:, 0]) + meta[:, 2], off_of(meta[:, 1]) + meta[:, 3]], axis=1).astype(jnp.int32)
    n_rows = TOP_K * n + ne * GROUP_TM
    nt = n_rows // GROUP_TM
    tile_start = jnp.arange(nt, dtype=jnp.int32) * GROUP_TM
    tile_e = jnp.sum(tile_start[:, None] >= ends[None, :], axis=1)
    n_live = ends[-1] // GROUP_TM
    last_e = jnp.sum((n_live - 1) * GROUP_TM >= ends)
    tile_e = jnp.where(tile_start < ends[-1], tile_e, last_e)
    tile_expert = jnp.concatenate([tile_e, n_live[None]]).astype(jnp.int32)
    xd = _dispatch(xs, pos, n_rows)
    ys = _group_ffn(xd, tile_expert, wg, wu, wd, tf=896)
    return _combine(ys, pos, x, gate, w, final_g)


def _hyena_filters(l, fw1, fb1, freq1, fw2, fb2, freq2, fw3):
    t = jnp.linspace(0.0, 1.0, l, dtype=F32)[:, None]
    w = (2.0 * math.pi / l) * jnp.arange(l, dtype=F32)[:, None]
    bands = jnp.linspace(1e-4, HYENA_BANDS - 1, HYENA_BANDS, dtype=F32)[None, :]
    z = jnp.concatenate([t, jnp.cos(bands * w), -jnp.sin(bands * w)], axis=-1)
    h = jnp.sin(freq1.astype(F32) * (jnp.dot(z, fw1.astype(F32), precision=HI) + fb1.astype(F32)))
    h = jnp.sin(freq2.astype(F32) * (jnp.dot(h, fw2.astype(F32), precision=HI) + fb2.astype(F32)))
    h = jnp.dot(h, fw3.astype(F32), precision=HI).reshape(l, 2, HYENA_ORDER, D_HYENA)
    max_decay = math.log(HYENA_DECAY_TARGET) / HYENA_FAST_DECAY_PCT
    min_decay = math.log(HYENA_DECAY_TARGET) / HYENA_SLOW_DECAY_PCT
    deltas = jnp.abs(jnp.linspace(min_decay, max_decay, D_HYENA, dtype=F32))
    h = h * jnp.exp(-t[:, :, None, None] * deltas)
    h_fwd, h_bwd = h[:, 0], h[:, 1]
    k = jnp.concatenate([h_fwd, jnp.zeros_like(h_fwd[:1]), h_bwd[:0:-1]], axis=0)
    return k / jnp.sum(jnp.abs(k), axis=0, keepdims=True)


def _hyena_spectra(l, *filter_params):
    k = _hyena_filters(l, *filter_params)
    kf = jnp.fft.rfft(k, axis=0)
    n = 2 * l
    wk = jnp.where(jnp.arange(l) == 0, 1.0, 2.0).astype(F32)[:, None] / n
    out = []
    for o in range(HYENA_ORDER):
        re = jnp.real(kf[:l, o]).astype(F32)
        im = jnp.imag(kf[:l, o]).astype(F32)
        nyq = jnp.real(kf[l, o]).astype(F32)
        kr = re * wk
        ki = (im * wk).at[0].set(0.0)
        kd = kr.at[0].set(nyq / n)
        out.append((kr, ki, kd))
    return out


def _fnet_maps(w_f, l):
    d = FNET_HEAD_DIM
    idx = np.arange(d)
    ang = 2.0 * np.pi * ((idx[:, None] * idx[None, :]) % d) / d
    scale = 1.0 / math.sqrt(d * l)
    cd = jnp.asarray(np.cos(ang) * scale, F32)
    sd = jnp.asarray(np.sin(ang) * scale, F32)
    wf = w_f.astype(F32)
    a = jnp.einsum('de,hef->hdf', cd, wf, precision=HI)
    bm = -jnp.einsum('de,hef->hdf', sd, wf, precision=HI)
    eye = jnp.eye(FNET_HEADS, dtype=F32)
    blk = lambda m: jnp.einsum('hdf,hg->hdgf', m, eye).reshape(D_FNET, D_FNET).astype(BF16)
    return blk(a), blk(bm)


def kernel(x, c, ctx, c_ctx, ada_w, ada_b, norm_mix_g, norm_ffn_g, w_in, w_out, fnet_w, hy_conv_w, hy_conv_b,
           hy_fw1, hy_fb1, hy_freq1, hy_fw2, hy_fb2, hy_freq2, hy_fw3, hy_bias, s5_a_re, s5_a_im, s5_log_dt,
           s5_b_re, s5_b_im, s5_c_re, s5_c_im, s5_d, s5_glu_w, s5_glu_b, ffn_w_gate, ffn_w_up, ffn_w_down,
           moe_router_w, moe_router_b, moe_w_gate, moe_w_up, moe_w_down, final_g):
    b, l, d = x.shape
    lc = ctx.shape[1]
    depth = ada_w.shape[0]
    assert l % LANES == 0 and lc % LANES == 0 and d % LANES == 0

    dft = {}
    for n in sorted({l, lc}):
        mc, sf = _hyena_dft(n)
        dft[n] = dict(hy_c=mc, hy_sf=sf, hy_si=sf.T, fn=_dft_cos_sin(n, n))

    cc = jnp.concatenate([c, c_ctx[None, :]], axis=0).astype(F32)
    rows = -(-(b + 1) // 8) * 8
    cc = jnp.pad(cc, ((0, rows - (b + 1)), (0, 0)))

    def mix(li, xs, sh, sc, grid_mode, s0f, s0b, s5m, states_only):
        n_tok = xs.shape[1]
        g_mix = norm_mix_g[li].reshape(1, d)
        if states_only:
            (ps,) = _inproj(xs, g_mix, sh, sc, w_in[li][:, S5_OFF:].astype(BF16),
                            jnp.zeros((3, LANES), F32), jnp.zeros((1, LANES), F32),
                            grid_mode=grid_mode, full=False)
            _, hf, hb = _s5(ps, s0f, s0b, s5m, s5_d[li])
            return None, hf, hb
        pf, qv, qx1, qx2, ps = _inproj(
            xs, g_mix, sh, sc, w_in[li].astype(BF16), hy_conv_w[li].astype(F32),
            hy_conv_b[li].astype(F32).reshape(1, -1), grid_mode=grid_mode, full=True)
        tabs = dft[n_tok]
        amat, bmat = _fnet_maps(fnet_w[li], n_tok)
        yf = _fnet(pf, tabs['fn'][0], tabs['fn'][1], amat, bmat)
        spectra = _hyena_spectra(n_tok, hy_fw1[li], hy_fb1[li], hy_freq1[li], hy_fw2[li], hy_fb2[li],
                                 hy_freq2[li], hy_fw3[li])
        z = qv
        for o, xg in enumerate((qx1, qx2)):
            pr, pi = _hy_fwd(z, tabs['hy_c'], tabs['hy_sf'], *spectra[o])
            z = _hy_inv(pr, pi, tabs['hy_c'], tabs['hy_si'], xg, z, hy_bias[li][o].astype(F32))
        ys, hf, hb = _s5(ps, s0f, s0b, s5m, s5_d[li])
        return (yf, z, ys), hf, hb

    def out_proj(li, parts, xs, gate):
        wo = w_out[li].astype(BF16)
        return _outproj(parts[0], parts[1], parts[2], xs, gate,
                        wo[:D_FNET], wo[D_FNET:D_FNET + D_HYENA], wo[D_FNET + D_HYENA:],
                        s5_glu_w[li].astype(BF16), s5_glu_b[li].astype(F32).reshape(1, -1))

    def channel_mix(li, xs, sh, sc, gate, fin):
        i = li // 2
        g_ffn = norm_ffn_g[li].reshape(1, d)
        if li % 2 == 0:
            return _ffn(xs, g_ffn, sh, sc, gate, None, ffn_w_gate[i][None].astype(BF16),
                        ffn_w_up[i][None].astype(BF16), ffn_w_down[i][None].astype(BF16), fin, tf=1408)
        return _moe(xs, g_ffn, sh, sc, gate, moe_router_w[i], moe_router_b[i], moe_w_gate[i].astype(BF16),
                    moe_w_up[i].astype(BF16), moe_w_down[i].astype(BF16), fin)

    zero_state = jnp.zeros((b, N_S5_BLOCKS, 1, 2 * STATE_HALF), F32)
    ctx_s = ctx
    for li in range(depth):
        last = li == depth - 1
        mod = _ada(cc, ada_w[li].astype(F32), ada_b[li].astype(F32))
        lat = [m.reshape(b, 1, d) for m in jnp.split(mod[:b], N_MOD, axis=-1)]
        cm = [jnp.broadcast_to(m.reshape(1, 1, d), (b, 1, d)) for m in jnp.split(mod[b:b + 1], N_MOD, axis=-1)]
        s5m = _s5_matrices(s5_a_re[li], s5_a_im[li], s5_log_dt[li], s5_b_re[li], s5_b_im[li],
                           s5_c_re[li], s5_c_im[li])

        parts, hf_c, hb_c = mix(li, ctx_s, cm[0], cm[1], False, zero_state, zero_state, s5m, last)
        if not last:
            ctx_s = out_proj(li, parts, ctx_s, cm[2])
            ctx_s = channel_mix(li, ctx_s, cm[3], cm[4], cm[5], None)

        parts, _, _ = mix(li, x, lat[0], lat[1], True, hf_c, hb_c, s5m, False)
        x = out_proj(li, parts, x, lat[2])
        x = channel_mix(li, x, lat[3], lat[4], lat[5], final_g.reshape(1, d) if last else None)

    if depth == 0:
        raise ValueError("depth must be positive")
    return x
```

```python
import functools
import math

import numpy as np
import jax
import jax.numpy as jnp
from jax import lax
from jax.experimental import pallas as pl
from jax.experimental.pallas import tpu as pltpu

F32 = jnp.float32
BF16 = jnp.bfloat16
HI = lax.Precision.HIGHEST

LANES = 128
GRID_ROW = 64
FNET_HEADS, FNET_HEAD_DIM = 4, 64
D_FNET = FNET_HEADS * FNET_HEAD_DIM
D_HYENA = 384
HYENA_ORDER = 2
HYENA_EMB = 33
HYENA_BANDS = (HYENA_EMB - 1) // 2
HYENA_FAST_DECAY_PCT, HYENA_SLOW_DECAY_PCT, HYENA_DECAY_TARGET = 0.3, 1.5, 0.01
S5_GROUP, S5_GROUPS, S5_STATE = 16, 24, 64
D_S5 = S5_GROUP * S5_GROUPS
HY_OFF = D_FNET
S5_OFF = D_FNET + (HYENA_ORDER + 1) * D_HYENA
N_MOD = 6
TOP_K = 2
RMS_EPS = 1e-6
S5_CHUNK = 8
GROUPS_PER_BLOCK = LANES // S5_GROUP
N_S5_BLOCKS = D_S5 // LANES
STATE_HALF = GROUPS_PER_BLOCK * S5_STATE
NEG_BIG = float(np.finfo(np.float32).min)
VMEM_LIMIT = 56 * 1024 * 1024


def _cp(sem, vmem=None):
    return pltpu.CompilerParams(dimension_semantics=sem, vmem_limit_bytes=vmem)


def _dot(a, b):
    return jnp.dot(a, b, preferred_element_type=F32)


def _split_bf16(a):
    hi = a.astype(BF16)
    lo = (a - hi.astype(F32)).astype(BF16)
    return hi, lo


def _sigmoid(x):
    return 1.0 / (1.0 + jnp.exp(-x))


def _rms_mod(x, g, sh, sc):
    ms = jnp.mean(x * x, axis=-1, keepdims=True)
    return (x * lax.rsqrt(ms + RMS_EPS) * g) * (1.0 + sc) + sh


def _ada_kernel(c_ref, w_ref, b_ref, o_ref):
    c = c_ref[...]
    s = c * _sigmoid(c)
    s_hi, s_lo = _split_bf16(s)
    w_hi, w_lo = _split_bf16(w_ref[...])
    o_ref[...] = _dot(s_hi, w_hi) + _dot(s_hi, w_lo) + _dot(s_lo, w_hi) + b_ref[...]


def _ada(cc, w, b):
    r, d = cc.shape
    n = w.shape[1]
    tn = 512
    return pl.pallas_call(
        _ada_kernel,
        out_shape=jax.ShapeDtypeStruct((r, n), F32),
        grid=(n // tn,),
        in_specs=[pl.BlockSpec((r, d), lambda j: (0, 0)),
                  pl.BlockSpec((d, tn), lambda j: (0, j)),
                  pl.BlockSpec((1, tn), lambda j: (0, j))],
        out_specs=pl.BlockSpec((r, tn), lambda j: (0, j)),
        compiler_params=_cp(("parallel",)),
        name="ada_mod",
    )(cc, w, b.reshape(1, n))


def _expand_kernel(a1_ref, a2_ref, b1_ref, b2_ref, o_ref, *, nh):
    a2 = a2_ref[...]
    b2 = b2_ref[...]
    for h in range(nh):
        blk = a1_ref[:, h:h + 1] * a2 + b1_ref[:, h:h + 1] * b2
        o_ref[:, h * LANES:(h + 1) * LANES] = blk.astype(o_ref.dtype)


def _expand(a1, a2, b1, b2):
    r, nh = a1.shape
    tr = min(r, 256)
    return pl.pallas_call(
        functools.partial(_expand_kernel, nh=nh),
        out_shape=jax.ShapeDtypeStruct((r, nh * LANES), BF16),
        grid=(r // tr,),
        in_specs=[pl.BlockSpec((tr, nh), lambda i: (i, 0)),
                  pl.BlockSpec((tr, LANES), lambda i: (i, 0)),
                  pl.BlockSpec((tr, nh), lambda i: (i, 0)),
                  pl.BlockSpec((tr, LANES), lambda i: (i, 0))],
        out_specs=pl.BlockSpec((tr, nh * LANES), lambda i: (i, 0)),
        compiler_params=_cp(("parallel",)),
        name="dft_expand",
    )(a1, a2, b1, b2)


def _trig_tables(n_rows, period):
    nh = n_rows // LANES
    r = np.arange(n_rows, dtype=np.int64)[:, None]
    hi = (r * (np.arange(nh, dtype=np.int64)[None, :] * LANES)) % period
    lo = (r * np.arange(LANES, dtype=np.int64)[None, :]) % period
    ang_hi = 2.0 * np.pi * hi / period
    ang_lo = 2.0 * np.pi * lo / period
    return np.cos(ang_hi), np.sin(ang_hi), np.cos(ang_lo), np.sin(ang_lo)


def _dft_cos_sin(n, period):
    c1, s1, c2, s2 = (jnp.asarray(t, F32) for t in _trig_tables(n, period))
    return _expand(c1, c2, -s1, s2), _expand(s1, c2, c1, s2)


def _hyena_dft(l):
    h = l // 2
    nh = h // LANES
    period = 4 * l
    kp = np.arange(h, dtype=np.int64)[:, None]
    f = lambda t: jnp.asarray(t, F32)
    out = {}
    for name, k in (("e", 2 * kp), ("o", 2 * kp + 1)):
        qa = (k * (2 * LANES) * np.arange(nh, dtype=np.int64)[None, :]) % period
        qb = (k * (2 * np.arange(LANES, dtype=np.int64)[None, :] + 1)) % period
        aa = 2.0 * np.pi * qa / period
        ab = 2.0 * np.pi * qb / period
        c1, s1, c2, s2 = np.cos(aa), np.sin(aa), np.cos(ab), np.sin(ab)
        a1, a2, b1 = -s1, c2.copy(), -c1
        if name == "e":
            a1[0, :] = 1.0
            a2[0, :] = (-1.0) ** np.arange(LANES)
            b1[0, :] = 0.0
        out["c" + name] = _expand(f(c1), f(c2), f(-s1), f(s2))
        out["s" + name] = _expand(f(a1), f(a2), f(b1), f(s2))
    return out


def _flip_matrix(n):
    return jnp.asarray(np.eye(n, dtype=np.float32)[::-1], BF16)


def _flip_rows(v, j):
    sb = j.shape[0]
    nblk = v.shape[0] // sb
    v_hi, v_lo = _split_bf16(v)
    parts = []
    for s in range(nblk):
        r = slice((nblk - 1 - s) * sb, (nblk - s) * sb)
        parts.append(_dot(j, v_hi[r]) + _dot(j, v_lo[r]))
    return parts[0] if nblk == 1 else jnp.concatenate(parts, axis=0)


def _inproj_kernel(x_ref, g_ref, sh_ref, sc_ref, w_ref, cw_ref, cb_ref, j_ref, *out_refs, tm, nt, grid_mode, full):
    h = _rms_mod(x_ref[...], g_ref[...], sh_ref[...], sc_ref[...])
    acc = _dot(h.astype(BF16), w_ref[...])
    if not full:
        out_refs[0][...] = acc
        return
    pf_ref, qv_ref, qx1_ref, qx2_ref, ps_ref = out_refs
    pf_ref[...] = acc[:, :HY_OFF]
    ps_ref[...] = acc[:, S5_OFF:]
    hy = acc[:, HY_OFF:S5_OFF]
    row = lax.broadcasted_iota(jnp.int32, (tm, 1), 0)
    if grid_mode:
        first = (row % GRID_ROW) == 0
        last = (row % GRID_ROW) == GRID_ROW - 1
    else:
        first = row == 0
        last = row == tm - 1
    prev = jnp.where(first, 0.0, pltpu.roll(hy, 1, 0))
    nxt = jnp.where(last, 0.0, pltpu.roll(hy, tm - 1, 0))
    q = prev * cw_ref[0:1, :] + hy * cw_ref[1:2, :] + nxt * cw_ref[2:3, :] + cb_ref[...]
    folded = (qv_ref, qx1_ref, qx2_ref)
    cols = [slice(n * D_HYENA, (n + 1) * D_HYENA) for n in range(len(folded))]
    if nt == 1:
        half = tm // 2
        q_hi = _flip_rows(q[half:], j_ref[...])
        for r, c in zip(folded, cols):
            r[0] = q[:half, c]
            r[1] = q_hi[:, c]
        return
    i = pl.program_id(1)

    @pl.when(i < nt // 2)
    def _():
        for r, c in zip(folded, cols):
            r[...] = q[:, c]

    @pl.when(i >= nt // 2)
    def _():
        qf = _flip_rows(q, j_ref[...])
        for r, c in zip(folded, cols):
            r[...] = qf[:, c]


def _fold_spec(tm, nt, width):
    if nt == 1:
        return pl.BlockSpec((None, 2, tm // 2, width), lambda bi, i: (bi, 0, 0, 0))
    hn = nt // 2
    return pl.BlockSpec((None, None, tm, width),
                        lambda bi, i: (bi, i // hn, jnp.where(i < hn, i, nt - 1 - i), 0))


def _inproj(x, g, sh, sc, w, cw, cb, *, grid_mode, full):
    b, l, d = x.shape
    n = w.shape[1]
    tm = min(l, 512)
    nt = l // tm
    if not grid_mode:
        assert tm == l
    else:
        assert tm % GRID_ROW == 0
    assert nt == 1 or nt % 2 == 0
    row = lambda bi, i: (bi, i, 0)
    vec = lambda bi, i: (bi, 0, 0)
    const = lambda bi, i: (0, 0)
    flip = _flip_matrix(min(256, tm // 2 if nt == 1 else tm))
    plain = lambda wd: (jax.ShapeDtypeStruct((b, l, wd), F32), pl.BlockSpec((None, tm, wd), row))
    fold = lambda wd: (jax.ShapeDtypeStruct((b, 2, l // 2, wd), F32), _fold_spec(tm, nt, wd))
    outs = ([plain(D_FNET), fold(D_HYENA), fold(D_HYENA), fold(D_HYENA), plain(D_S5)] if full else [plain(n)])
    return pl.pallas_call(
        functools.partial(_inproj_kernel, tm=tm, nt=nt, grid_mode=grid_mode, full=full),
        out_shape=tuple(o[0] for o in outs),
        grid=(b, nt),
        in_specs=[pl.BlockSpec((None, tm, d), row),
                  pl.BlockSpec((1, d), const),
                  pl.BlockSpec((None, 1, d), vec),
                  pl.BlockSpec((None, 1, d), vec),
                  pl.BlockSpec((d, n), const),
                  pl.BlockSpec(cw.shape, const),
                  pl.BlockSpec(cb.shape, const),
                  pl.BlockSpec(flip.shape, const)],
        out_specs=tuple(o[1] for o in outs),
        compiler_params=_cp(("parallel", "parallel"), VMEM_LIMIT),
        name="inproj",
    )(x, g, sh, sc, w, cw, cb, flip)


def _fnet_kernel(x_ref, mc_ref, ms_ref, a_ref, b_ref, o_ref, accc, accs, *, nb):
    k = pl.program_id(2)

    @pl.when(k == 0)
    def _():
        accc[...] = jnp.zeros_like(accc)
        accs[...] = jnp.zeros_like(accs)

    mc = mc_ref[...]
    ms = ms_ref[...]
    for bi in range(nb):
        xb = x_ref[bi].astype(BF16)
        accc[bi] += _dot(mc, xb)
        accs[bi] += _dot(ms, xb)

    @pl.when(k == pl.num_programs(2) - 1)
    def _():
        for bi in range(nb):
            o_ref[bi] = (_dot(accc[bi].astype(BF16), a_ref[...])
                         + _dot(accs[bi].astype(BF16), b_ref[...]))


def _fnet(pf, mc, ms, amat, bmat):
    b, l, c = pf.shape
    nb = min(b, 2)
    t = min(l, 1024)
    return pl.pallas_call(
        functools.partial(_fnet_kernel, nb=nb),
        out_shape=jax.ShapeDtypeStruct((b, l, c), F32),
        grid=(b // nb, l // t, l // t),
        in_specs=[pl.BlockSpec((nb, t, c), lambda bb, i, k: (bb, k, 0)),
                  pl.BlockSpec((t, t), lambda bb, i, k: (i, k)),
                  pl.BlockSpec((t, t), lambda bb, i, k: (i, k)),
                  pl.BlockSpec((c, c), lambda bb, i, k: (0, 0)),
                  pl.BlockSpec((c, c), lambda bb, i, k: (0, 0))],
        out_specs=pl.BlockSpec((nb, t, c), lambda bb, i, k: (bb, i, 0)),
        scratch_shapes=[pltpu.VMEM((nb, t, c), F32), pltpu.VMEM((nb, t, c), F32)],
        compiler_params=_cp(("parallel", "parallel", "arbitrary"), VMEM_LIMIT),
        name="fnet_dft",
    )(pf, mc, ms, amat, bmat)


def _hy_fwd_kernel(z_ref, ce_ref, se_ref, co_ref, so_ref, kf_ref, pr_ref, pi_ref, acc, *, nb):
    k = pl.program_id(2)

    @pl.when(k == 0)
    def _():
        acc[...] = jnp.zeros_like(acc)

    ce, se, co, so = ce_ref[...], se_ref[...], co_ref[...], so_ref[...]
    for bi in range(nb):
        lo = z_ref[bi, 0]
        hi = z_ref[bi, 1]
        ev = (lo + hi).astype(BF16)
        od = (lo - hi).astype(BF16)
        acc[bi, 0] += _dot(ce, ev)
        acc[bi, 1] += _dot(se, od)
        acc[bi, 2] += _dot(co, od)
        acc[bi, 3] += _dot(so, ev)

    @pl.when(k == pl.num_programs(2) - 1)
    def _():
        for bi in range(nb):
            for p in range(2):
                zr, zi = acc[bi, 2 * p], acc[bi, 2 * p + 1]
                kr, ki, kd = kf_ref[p, 0], kf_ref[p, 1], kf_ref[p, 2]
                pr_ref[bi, p] = (zr * kr - zi * ki).astype(pr_ref.dtype)
                pi_ref[bi, p] = (zr * ki + zi * kd).astype(pi_ref.dtype)


def _hy_tiles(h):
    return min(h, 512), min(h, 1024)


def _hy_fwd(z, mats, kf):
    b, _, h, c = z.shape
    nb = min(b, 2)
    ti, tk = _hy_tiles(h)
    mspec = pl.BlockSpec((ti, tk), lambda bb, i, k: (i, k))
    ospec = pl.BlockSpec((nb, 2, ti, c), lambda bb, i, k: (bb, 0, i, 0))
    return pl.pallas_call(
        functools.partial(_hy_fwd_kernel, nb=nb),
        out_shape=(jax.ShapeDtypeStruct((b, 2, h, c), BF16), jax.ShapeDtypeStruct((b, 2, h, c), BF16)),
        grid=(b // nb, h // ti, h // tk),
        in_specs=[pl.BlockSpec((nb, 2, tk, c), lambda bb, i, k: (bb, 0, k, 0)),
                  mspec, mspec, mspec, mspec,
                  pl.BlockSpec((2, 3, ti, c), lambda bb, i, k: (0, 0, i, 0))],
        out_specs=(ospec, ospec),
        scratch_shapes=[pltpu.VMEM((nb, 4, ti, c), F32)],
        compiler_params=_cp(("parallel", "parallel", "arbitrary"), VMEM_LIMIT),
        name="hyena_fwd_dft",
    )(z, mats["ce"], mats["se"], mats["co"], mats["so"], kf)


def _hy_inv_kernel(pr_ref, pi_ref, ce_ref, se_ref, co_ref, so_ref, xg_ref, zin_ref, bias_ref, o_ref, acc, *, nb):
    k = pl.program_id(2)

    @pl.when(k == 0)
    def _():
        acc[...] = jnp.zeros_like(acc)

    ce, se, co, so = ce_ref[...], se_ref[...], co_ref[...], so_ref[...]
    for bi in range(nb):
        acc[bi, 0] += _dot(ce, pr_ref[bi, 0]) + _dot(so, pi_ref[bi, 1])
        acc[bi, 1] += _dot(se, pi_ref[bi, 0]) + _dot(co, pr_ref[bi, 1])

    @pl.when(k == pl.num_programs(2) - 1)
    def _():
        bias = bias_ref[...]
        for bi in range(nb):
            u, v = acc[bi, 0], acc[bi, 1]
            o_ref[bi, 0] = xg_ref[bi, 0] * (u + v + zin_ref[bi, 0] * bias)
            o_ref[bi, 1] = xg_ref[bi, 1] * (u - v + zin_ref[bi, 1] * bias)


def _hy_inv(pr, pi, mats_t, xg, zin, bias):
    b, _, h, c = xg.shape
    nb = min(b, 2)
    ti, tk = _hy_tiles(h)
    pspec = pl.BlockSpec((nb, 2, tk, c), lambda bb, i, k: (bb, 0, k, 0))
    mspec = pl.BlockSpec((ti, tk), lambda bb, i, k: (i, k))
    ospec = pl.BlockSpec((nb, 2, ti, c), lambda bb, i, k: (bb, 0, i, 0))
    return pl.pallas_call(
        functools.partial(_hy_inv_kernel, nb=nb),
        out_shape=jax.ShapeDtypeStruct((b, 2, h, c), F32),
        grid=(b // nb, h // ti, h // tk),
        in_specs=[pspec, pspec, mspec, mspec, mspec, mspec, ospec, ospec,
                  pl.BlockSpec((1, c), lambda bb, i, k: (0, 0))],
        out_specs=ospec,
        scratch_shapes=[pltpu.VMEM((nb, 2, ti, c), F32)],
        compiler_params=_cp(("parallel", "parallel", "arbitrary"), VMEM_LIMIT),
        name="hyena_inv_dft",
    )(pr, pi, mats_t["ce"], mats_t["se"], mats_t["co"], mats_t["so"], xg, zin, bias.reshape(1, c))


def _cmul(a, h):
    ar, ai = a[:, :STATE_HALF], a[:, STATE_HALF:]
    hr, hi = h[:, :STATE_HALF], h[:, STATE_HALF:]
    return jnp.concatenate([ar * hr - ai * hi, ar * hi + ai * hr], axis=1)


def _s5_kernel(u_ref, s0f_ref, s0b_ref, mi_ref, mof_ref, mob_ref, mnf_ref, mnb_ref, atf_ref, atb_ref,
               d_ref, y_ref, hf_ref, hb_ref, x_scr, sf_scr, sb_scr, hinf_scr, hinb_scr, *, nc, t_chunk):
    for s in range(t_chunk):
        x_scr[:, s * LANES:(s + 1) * LANES] = u_ref[pl.ds(s, nc, stride=t_chunk), :].astype(BF16)
    x = x_scr[...]
    sf_scr[...] = _dot(x, mof_ref[...])
    sb_scr[...] = _dot(x, mob_ref[...])
    atf = atf_ref[...]
    atb = atb_ref[...]

    def body(c, carry):
        hf, hb = carry
        cb = nc - 1 - c
        hinf_scr[pl.ds(c, 1), :] = hf
        hinb_scr[pl.ds(cb, 1), :] = hb
        hf = _cmul(atf, hf) + sf_scr[pl.ds(c, 1), :]
        hb = _cmul(atb, hb) + sb_scr[pl.ds(cb, 1), :]
        return hf, hb

    hf, hb = lax.fori_loop(0, nc, body, (s0f_ref[...], s0b_ref[...]))
    hf_ref[...] = hf
    hb_ref[...] = hb
    y = (_dot(x, mi_ref[...])
         + _dot(hinf_scr[...].astype(BF16), mnf_ref[...])
         + _dot(hinb_scr[...].astype(BF16), mnb_ref[...]))
    d = d_ref[...]
    for t in range(t_chunk):
        rows = pl.ds(t, nc, stride=t_chunk)
        y_ref[rows, :] = y[:, t * LANES:(t + 1) * LANES] + d * u_ref[rows, :]


def _s5(u, s0f, s0b, mats, d_skip):
    b, l, _ = u.shape
    tc = S5_CHUNK
    nc = l // tc
    kin = tc * LANES
    ns = 2 * STATE_HALF
    mi, mof, mob, mnf, mnb, atf, atb = mats
    st_spec = pl.BlockSpec((None, None, 1, ns), lambda j, bi: (bi, j, 0, 0))
    w3 = lambda s1, s2: pl.BlockSpec((None, s1, s2), lambda j, bi: (j, 0, 0))
    u_spec = pl.BlockSpec((None, l, LANES), lambda j, bi: (bi, 0, j))
    return pl.pallas_call(
        functools.partial(_s5_kernel, nc=nc, t_chunk=tc),
        out_shape=(jax.ShapeDtypeStruct((b, l, D_S5), F32),
                   jax.ShapeDtypeStruct((b, N_S5_BLOCKS, 1, ns), F32),
                   jax.ShapeDtypeStruct((b, N_S5_BLOCKS, 1, ns), F32)),
        grid=(N_S5_BLOCKS, b),
        in_specs=[u_spec, st_spec, st_spec,
                  w3(kin, kin), w3(kin, ns), w3(kin, ns), w3(ns, kin), w3(ns, kin),
                  w3(1, ns), w3(1, ns),
                  pl.BlockSpec((1, LANES), lambda j, bi: (0, j))],
        out_specs=(u_spec, st_spec, st_spec),
        scratch_shapes=[pltpu.VMEM((nc, kin), BF16),
                        pltpu.VMEM((nc, ns), F32), pltpu.VMEM((nc, ns), F32),
                        pltpu.VMEM((nc, ns), F32), pltpu.VMEM((nc, ns), F32)],
        compiler_params=_cp(("parallel", "parallel"), VMEM_LIMIT),
        name="s5_scan",
    )(u, s0f, s0b, mi, mof, mob, mnf, mnb, atf, atb, d_skip.reshape(1, D_S5))


def _s5_matrices(a_re, a_im, log_dt, b_re, b_im, c_re, c_im):
    tc = S5_CHUNK
    g, p, hd = S5_GROUPS, S5_STATE, S5_GROUP
    lam = lax.complex(a_re.astype(F32), a_im.astype(F32))
    dt = jnp.exp(log_dt.astype(F32))[..., None]
    a_bar = jnp.exp(lam * dt)
    b_bar = ((a_bar - 1.0) / lam)[..., None] * lax.complex(b_re.astype(F32), b_im.astype(F32))
    cc = lax.complex(c_re.astype(F32), c_im.astype(F32))
    ks = jnp.arange(tc + 1, dtype=F32)
    pw = jnp.exp((lam * dt)[:, :, None, :] * ks[None, None, :, None])
    kern = jnp.real(jnp.einsum('dgip,dgkp,dgpj->dgkij', cc, pw[:, :, :tc], b_bar, precision=HI))
    s_idx = jnp.arange(tc)[:, None]
    t_idx = jnp.arange(tc)[None, :]
    lag = t_idx - s_idx
    kf = jnp.where((lag >= 0)[None, :, :, None, None], kern[0][:, jnp.clip(lag, 0, tc - 1)], 0.0)
    kb = jnp.where((lag <= 0)[None, :, :, None, None], kern[1][:, jnp.clip(-lag, 0, tc - 1)], 0.0)
    ktot = kf + kb
    eye = jnp.eye(GROUPS_PER_BLOCK, dtype=F32)
    kt = ktot.reshape(N_S5_BLOCKS, GROUPS_PER_BLOCK, tc, tc, hd, hd)
    mi = jnp.einsum('Jgstij,gh->Jsgjthi', kt, eye).reshape(N_S5_BLOCKS, tc * LANES, tc * LANES)

    def parts(z):
        return jnp.stack([jnp.real(z), jnp.imag(z)], axis=0)

    def out_mat(d, powers):
        z = powers[:, :, :, None] * b_bar[d][:, None, :, :]
        z = parts(z).reshape(2, N_S5_BLOCKS, GROUPS_PER_BLOCK, tc, p, hd)
        m = jnp.einsum('cJgspj,gh->Jsgjchp', z, eye)
        return m.reshape(N_S5_BLOCKS, tc * LANES, 2 * STATE_HALF)

    def in_mat(d, powers):
        z = cc[d][:, None, :, :] * powers[:, :, None, :]
        z = jnp.stack([jnp.real(z), -jnp.imag(z)], axis=0)
        z = z.reshape(2, N_S5_BLOCKS, GROUPS_PER_BLOCK, tc, hd, p)
        m = jnp.einsum('cJgtip,gh->Jcgpthi', z, eye)
        return m.reshape(N_S5_BLOCKS, 2 * STATE_HALF, tc * LANES)

    rev = jnp.arange(tc - 1, -1, -1)
    mof = out_mat(0, pw[0][:, rev])
    mob = out_mat(1, pw[1][:, :tc])
    mnf = in_mat(0, pw[0][:, 1:tc + 1])
    mnb = in_mat(1, pw[1][:, tc - jnp.arange(tc)])

    def step(d):
        z = parts(pw[d][:, tc]).reshape(2, N_S5_BLOCKS, 1, STATE_HALF)
        return jnp.transpose(z, (1, 2, 0, 3)).reshape(N_S5_BLOCKS, 1, 2 * STATE_HALF)

    bf = lambda m: m.astype(BF16)
    return bf(mi), bf(mof), bf(mob), bf(mnf), bf(mnb), step(0), step(1)


def _outproj_kernel(yf_ref, yh_ref, ys_ref, x_ref, gate_ref, w1_ref, w2_ref, w3_ref, gw_ref, gb_ref, j_ref, o_ref,
                    *, nt):
    y = ys_ref[...]
    y = 0.5 * y * (1.0 + jnp.tanh(math.sqrt(2.0 / math.pi) * (y + 0.044715 * (y * y * y))))
    y = y * _sigmoid(_dot(y.astype(BF16), gw_ref[...]) + gb_ref[...])
    acc = _dot(yf_ref[...].astype(BF16), w1_ref[...]) + _dot(y.astype(BF16), w3_ref[...])

    def finish(yh):
        o_ref[...] = x_ref[...] + gate_ref[...] * (acc + _dot(yh.astype(BF16), w2_ref[...]))

    if nt == 1:
        finish(jnp.concatenate([yh_ref[0], _flip_rows(yh_ref[1], j_ref[...])], axis=0))
        return
    i = pl.program_id(1)

    @pl.when(i < nt // 2)
    def _():
        finish(yh_ref[...])

    @pl.when(i >= nt // 2)
    def _():
        finish(_flip_rows(yh_ref[...], j_ref[...]))


def _outproj(yf, yh, ys, x, gate, w1, w2, w3, gw, gb):
    b, l, d = x.shape
    tm = min(l, 512)
    nt = l // tm
    row = lambda bi, i: (bi, i, 0)
    const = lambda bi, i: (0, 0)
    full = lambda a: pl.BlockSpec(a.shape, const)
    flip = _flip_matrix(min(256, tm // 2 if nt == 1 else tm))
    return pl.pallas_call(
        functools.partial(_outproj_kernel, nt=nt),
        out_shape=jax.ShapeDtypeStruct((b, l, d), F32),
        grid=(b, nt),
        in_specs=[pl.BlockSpec((None, tm, yf.shape[2]), row),
                  _fold_spec(tm, nt, yh.shape[3]),
                  pl.BlockSpec((None, tm, ys.shape[2]), row),
                  pl.BlockSpec((None, tm, d), row),
                  pl.BlockSpec((None, 1, d), lambda bi, i: (bi, 0, 0)),
                  full(w1), full(w2), full(w3), full(gw), full(gb), full(flip)],
        out_specs=pl.BlockSpec((None, tm, d), row),
        compiler_params=_cp(("parallel", "parallel"), VMEM_LIMIT),
        name="outproj",
    )(yf, yh, ys, x, gate, w1, w2, w3, gw, gb, flip)


def _ffn_kernel(x_ref, g_ref, sh_ref, sc_ref, gate_ref, rwh_ref, rwl_ref, rb_ref, wg_ref, wu_ref, wd_ref,
                fg_ref, o_ref, xn_scr, acc_scr, comb_scr, *, moe, final_norm):
    e = pl.program_id(2)
    f = pl.program_id(3)
    first = jnp.logical_and(e == 0, f == 0)
    last = jnp.logical_and(e == pl.num_programs(2) - 1, f == pl.num_programs(3) - 1)

    @pl.when(first)
    def _():
        h = _rms_mod(x_ref[...], g_ref[...], sh_ref[...], sc_ref[...])
        xn_scr[...] = h.astype(BF16)
        acc_scr[...] = jnp.zeros_like(acc_scr)
        if moe:
            h_hi, h_lo = _split_bf16(h)
            rwh = rwh_ref[...]
            lg = _dot(h_hi, rwh) + _dot(h_hi, rwl_ref[...]) + _dot(h_lo, rwh) + rb_ref[...]
            lane = lax.broadcasted_iota(jnp.int32, lg.shape, 1)
            m1 = jnp.max(lg, axis=-1, keepdims=True)
            i1 = jnp.min(jnp.where(lg == m1, lane, LANES), axis=-1, keepdims=True)
            lg2 = jnp.where(lane == i1, NEG_BIG, lg)
            m2 = jnp.max(lg2, axis=-1, keepdims=True)
            i2 = jnp.min(jnp.where(lg2 == m2, lane, LANES), axis=-1, keepdims=True)
            ex = jnp.exp(m2 - m1)
            den = 1.0 + ex
            comb_scr[...] = (jnp.where(lane == i1, 1.0 / den, 0.0)
                             + jnp.where(lane == i2, ex / den, 0.0))

    xn = xn_scr[...]
    gt = _dot(xn, wg_ref[...])
    up = _dot(xn, wu_ref[...])
    hmid = gt * _sigmoid(gt) * up
    if moe:
        comb = comb_scr[...]
        lane = lax.broadcasted_iota(jnp.int32, comb.shape, 1)
        hmid = hmid * jnp.sum(jnp.where(lane == e, comb, 0.0), axis=-1, keepdims=True)
    acc_scr[...] += _dot(hmid.astype(BF16), wd_ref[...])

    @pl.when(last)
    def _():
        y = x_ref[...] + gate_ref[...] * acc_scr[...]
        if final_norm:
            ms = jnp.mean(y * y, axis=-1, keepdims=True)
            y = y * lax.rsqrt(ms + RMS_EPS) * fg_ref[...]
        o_ref[...] = y


def _ffn(x, g, sh, sc, gate, router, wg, wu, wd, final_g, *, tf):
    b, l, d = x.shape
    ne, _, ff = wg.shape
    moe = router is not None
    final_norm = final_g is not None
    tm = min(l, 1024)
    if moe:
        rwh, rwl, rb = router
    else:
        rwh = rwl = jnp.zeros((d, LANES), BF16)
        rb = jnp.zeros((1, LANES), F32)
    fg = final_g if final_norm else jnp.ones((1, d), F32)
    row = lambda bi, i, e, f: (bi, i, 0)
    vec = lambda bi, i, e, f: (bi, 0, 0)
    const = lambda bi, i, e, f: (0, 0)
    return pl.pallas_call(
        functools.partial(_ffn_kernel, moe=moe, final_norm=final_norm),
        out_shape=jax.ShapeDtypeStruct((b, l, d), F32),
        grid=(b, l // tm, ne, ff // tf),
        in_specs=[pl.BlockSpec((None, tm, d), row),
                  pl.BlockSpec((1, d), const),
                  pl.BlockSpec((None, 1, d), vec),
                  pl.BlockSpec((None, 1, d), vec),
                  pl.BlockSpec((None, 1, d), vec),
                  pl.BlockSpec((d, LANES), const),
                  pl.BlockSpec((d, LANES), const),
                  pl.BlockSpec((1, LANES), const),
                  pl.BlockSpec((None, d, tf), lambda bi, i, e, f: (e, 0, f)),
                  pl.BlockSpec((None, d, tf), lambda bi, i, e, f: (e, 0, f)),
                  pl.BlockSpec((None, tf, d), lambda bi, i, e, f: (e, f, 0)),
                  pl.BlockSpec((1, d), const)],
        out_specs=pl.BlockSpec((None, tm, d), row),
        scratch_shapes=[pltpu.VMEM((tm, d), BF16), pltpu.VMEM((tm, d), F32), pltpu.VMEM((tm, LANES), F32)],
        compiler_params=_cp(("parallel", "parallel", "arbitrary", "arbitrary"), VMEM_LIMIT),
        name="moe_ffn" if moe else "dense_ffn",
    )(x, g, sh, sc, gate, rwh, rwl, rb, wg, wu, wd, fg)


SLAB = 8
ROUTE_TM = 512
GROUP_TM = 1024


def _to_slabs(ref, val, n):
    for s in range(SLAB):
        ref[pl.ds(s, n, stride=SLAB), :] = val[:, s * LANES:(s + 1) * LANES]


def _from_slabs(ref, n):
    return jnp.concatenate([ref[pl.ds(s, n, stride=SLAB), :] for s in range(SLAB)], axis=1)


def _route_kernel(x_ref, g_ref, sh_ref, sc_ref, rwh_ref, rwl_ref, rb_ref, tri_ref,
                  xs_ref, meta_ref, gate_ref, cnt_ref, carry_scr, *, tm):
    step = pl.program_id(0) * pl.num_programs(1) + pl.program_id(1)

    @pl.when(step == 0)
    def _():
        carry_scr[...] = jnp.zeros_like(carry_scr)

    h = _rms_mod(x_ref[...], g_ref[...], sh_ref[...], sc_ref[...])
    _to_slabs(xs_ref, h, tm)
    h_hi, h_lo = _split_bf16(h)
    rwh = rwh_ref[...]
    lg = _dot(h_hi, rwh) + _dot(h_hi, rwl_ref[...]) + _dot(h_lo, rwh) + rb_ref[...]
    lane = lax.broadcasted_iota(jnp.int32, lg.shape, 1)
    m1 = jnp.max(lg, axis=-1, keepdims=True)
    i1 = jnp.min(jnp.where(lg == m1, lane, LANES), axis=-1, keepdims=True)
    lg2 = jnp.where(lane == i1, NEG_BIG, lg)
    m2 = jnp.max(lg2, axis=-1, keepdims=True)
    i2 = jnp.min(jnp.where(lg2 == m2, lane, LANES), axis=-1, keepdims=True)
    ex = jnp.exp(m2 - m1)
    den = 1.0 + ex
    gate_ref[...] = jnp.where(lane == 0, 1.0 / den, jnp.where(lane == 1, ex / den, 0.0))
    sel1 = lane == i1
    sel2 = lane == i2
    tri = tri_ref[...]
    p1 = _dot(tri, sel1.astype(BF16))
    p2 = _dot(tri, sel2.astype(BF16))
    tot1 = p1[tm - 1:tm, :]
    tot2 = p2[tm - 1:tm, :]
    carry = carry_scr[...]
    r1 = jnp.sum(jnp.where(sel1, carry + p1 - 1.0, 0.0), axis=-1, keepdims=True)
    r2 = jnp.sum(jnp.where(sel2, carry + tot1 + p2 - 1.0, 0.0), axis=-1, keepdims=True)
    carry = carry + tot1 + tot2
    carry_scr[...] = carry
    cnt_ref[...] = carry.astype(jnp.int32)
    meta_ref[...] = jnp.where(lane == 0, i1, jnp.where(lane == 1, i2, jnp.where(
        lane == 2, r1.astype(jnp.int32), jnp.where(lane == 3, r2.astype(jnp.int32), 0))))


def _route(x, g, sh, sc, rwh, rwl, rb):
    b, l, d = x.shape
    tm = min(l, ROUTE_TM)
    n = b * l
    nl = l // tm
    tri = jnp.tril(jnp.ones((tm, tm), F32)).astype(BF16)
    row = lambda bi, i: (bi, i, 0)
    vec = lambda bi, i: (bi, 0, 0)
    const = lambda bi, i: (0, 0)
    flat = lambda bi, i: (bi * nl + i, 0)
    return pl.pallas_call(
        functools.partial(_route_kernel, tm=tm),
        out_shape=(jax.ShapeDtypeStruct((n * SLAB, LANES), F32),
                   jax.ShapeDtypeStruct((n, LANES), jnp.int32),
                   jax.ShapeDtypeStruct((n, LANES), F32),
                   jax.ShapeDtypeStruct((1, LANES), jnp.int32)),
        grid=(b, nl),
        in_specs=[pl.BlockSpec((None, tm, d), row),
                  pl.BlockSpec((1, d), const),
                  pl.BlockSpec((None, 1, d), vec),
                  pl.BlockSpec((None, 1, d), vec),
                  pl.BlockSpec((d, LANES), const),
                  pl.BlockSpec((d, LANES), const),
                  pl.BlockSpec((1, LANES), const),
                  pl.BlockSpec((tm, tm), const)],
        out_specs=(pl.BlockSpec((tm * SLAB, LANES), flat),
                   pl.BlockSpec((tm, LANES), flat),
                   pl.BlockSpec((tm, LANES), flat),
                   pl.BlockSpec((1, LANES), const)),
        scratch_shapes=[pltpu.VMEM((1, LANES), F32)],
        compiler_params=_cp(("arbitrary", "arbitrary"), VMEM_LIMIT),
        name="moe_route",
    )(x, g, sh, sc, rwh, rwl, rb, tri)


def _dispatch_kernel(pos_hbm, xs_ref, init_hbm, out_hbm, idx_smem, isem, sem, *, tm):
    del init_hbm
    step = pl.program_id(0)
    icopy = pltpu.make_async_copy(pos_hbm.at[pl.ds(step * 2 * tm, 2 * tm)], idx_smem, isem)
    icopy.start()
    icopy.wait()

    def row_copy(i, k):
        return pltpu.make_async_copy(xs_ref.at[pl.ds(i * SLAB, SLAB), :], out_hbm.at[idx_smem[2 * i + k]], sem)

    def issue(i, carry):
        row_copy(i, 0).start()
        row_copy(i, 1).start()
        return carry

    def drain(i, carry):
        row_copy(i, 0).wait()
        row_copy(i, 1).wait()
        return carry

    lax.fori_loop(0, tm, issue, 0, unroll=8)
    lax.fori_loop(0, tm, drain, 0, unroll=8)


def _dispatch(xs, pos, n_rows):
    n = xs.shape[0] // SLAB
    tm = min(n, ROUTE_TM)
    init = jnp.zeros((n_rows, SLAB, LANES), F32)
    return pl.pallas_call(
        functools.partial(_dispatch_kernel, tm=tm),
        out_shape=jax.ShapeDtypeStruct((n_rows, SLAB, LANES), F32),
        grid=(n // tm,),
        in_specs=[pl.BlockSpec(memory_space=pl.ANY),
                  pl.BlockSpec((tm * SLAB, LANES), lambda i: (i, 0)),
                  pl.BlockSpec(memory_space=pl.ANY)],
        out_specs=pl.BlockSpec(memory_space=pl.ANY),
        scratch_shapes=[pltpu.SMEM((2 * tm,), jnp.int32), pltpu.SemaphoreType.DMA, pltpu.SemaphoreType.DMA],
        input_output_aliases={2: 0},
        compiler_params=_cp(("arbitrary",), VMEM_LIMIT),
        name="moe_dispatch",
    )(pos.reshape(-1), xs, init)


def _group_ffn_kernel(te_ref, x_ref, wg_ref, wu_ref, wd_ref, o_ref, xn_scr, acc_scr, *, tm, nt):
    t = pl.program_id(0)
    f = pl.program_id(1)
    valid = t < te_ref[nt]
    last = f == pl.num_programs(1) - 1

    @pl.when(jnp.logical_and(valid, f == 0))
    def _():
        xn_scr[...] = _from_slabs(x_ref, tm).astype(BF16)
        acc_scr[...] = jnp.zeros_like(acc_scr)

    @pl.when(valid)
    def _():
        xn = xn_scr[...]
        gt = _dot(xn, wg_ref[...])
        up = _dot(xn, wu_ref[...])
        acc_scr[...] += _dot((gt * _sigmoid(gt) * up).astype(BF16), wd_ref[...])

    @pl.when(jnp.logical_and(valid, last))
    def _():
        _to_slabs(o_ref, acc_scr[...], tm)

    @pl.when(jnp.logical_and(jnp.logical_not(valid), last))
    def _():
        o_ref[...] = jnp.zeros_like(o_ref)


def _group_ffn(xs, tile_expert, wg, wu, wd, *, tf):
    n_rows = xs.shape[0]
    tm = GROUP_TM
    nt = n_rows // tm
    _, d, ff = wg.shape
    x2 = xs.reshape(n_rows * SLAB, LANES)
    rows = pl.BlockSpec((tm * SLAB, LANES), lambda t, f, te: (t, 0))
    out = pl.pallas_call(
        functools.partial(_group_ffn_kernel, tm=tm, nt=nt),
        out_shape=jax.ShapeDtypeStruct((n_rows * SLAB, LANES), F32),
        grid_spec=pltpu.PrefetchScalarGridSpec(
            num_scalar_prefetch=1,
            grid=(nt, ff // tf),
            in_specs=[rows,
                      pl.BlockSpec((None, d, tf), lambda t, f, te: (te[t], 0, f)),
                      pl.BlockSpec((None, d, tf), lambda t, f, te: (te[t], 0, f)),
                      pl.BlockSpec((None, tf, d), lambda t, f, te: (te[t], f, 0))],
            out_specs=rows,
            scratch_shapes=[pltpu.VMEM((tm, d), BF16), pltpu.VMEM((tm, d), F32)]),
        compiler_params=_cp(("arbitrary", "arbitrary"), VMEM_LIMIT),
        name="moe_group_ffn",
    )(tile_expert, x2, wg, wu, wd)
    return out.reshape(n_rows, SLAB, LANES)


def _combine_kernel(pos_hbm, ys_hbm, x_ref, gate_ref, w_ref, fg_ref, o_ref, idx_smem, ybuf, isem, sem,
                    *, tm, final_norm):
    step = pl.program_id(0) * pl.num_programs(1) + pl.program_id(1)
    icopy = pltpu.make_async_copy(pos_hbm.at[pl.ds(step * 2 * tm, 2 * tm)], idx_smem, isem)
    icopy.start()
    icopy.wait()

    def row_copy(i, k):
        return pltpu.make_async_copy(ys_hbm.at[idx_smem[2 * i + k]], ybuf.at[k, pl.ds(i * SLAB, SLAB), :], sem)

    def issue(i, carry):
        row_copy(i, 0).start()
        row_copy(i, 1).start()
        return carry

    def drain(i, carry):
        row_copy(i, 0).wait()
        row_copy(i, 1).wait()
        return carry

    lax.fori_loop(0, tm, issue, 0, unroll=8)
    lax.fori_loop(0, tm, drain, 0, unroll=8)
    w = w_ref[...]
    moe = w[:, 0:1] * _from_slabs(ybuf.at[0], tm) + w[:, 1:2] * _from_slabs(ybuf.at[1], tm)
    y = x_ref[...] + gate_ref[...] * moe
    if final_norm:
        ms = jnp.mean(y * y, axis=-1, keepdims=True)
        y = y * lax.rsqrt(ms + RMS_EPS) * fg_ref[...]
    o_ref[...] = y


def _combine(ys, pos, x, gate, w, final_g):
    b, l, d = x.shape
    tm = min(l, ROUTE_TM)
    nl = l // tm
    final_norm = final_g is not None
    fg = final_g if final_norm else jnp.ones((1, d), F32)
    row = lambda bi, i: (bi, i, 0)
    return pl.pallas_call(
        functools.partial(_combine_kernel, tm=tm, final_norm=final_norm),
        out_shape=jax.ShapeDtypeStruct((b, l, d), F32),
        grid=(b, nl),
        in_specs=[pl.BlockSpec(memory_space=pl.ANY),
                  pl.BlockSpec(memory_space=pl.ANY),
                  pl.BlockSpec((None, tm, d), row),
                  pl.BlockSpec((None, 1, d), lambda bi, i: (bi, 0, 0)),
                  pl.BlockSpec((tm, LANES), lambda bi, i: (bi * nl + i, 0)),
                  pl.BlockSpec((1, d), lambda bi, i: (0, 0))],
        out_specs=pl.BlockSpec((None, tm, d), row),
        scratch_shapes=[pltpu.SMEM((2 * tm,), jnp.int32), pltpu.VMEM((2, tm * SLAB, LANES), F32),
                        pltpu.SemaphoreType.DMA, pltpu.SemaphoreType.DMA],
        compiler_params=_cp(("arbitrary", "arbitrary"), VMEM_LIMIT),
        name="moe_combine",
    )(pos.reshape(-1), ys, x, gate, w, fg)


def _moe(x, g, sh, sc, gate, router_w, router_b, wg, wu, wd, final_g):
    b, l, d = x.shape
    n = b * l
    ne = router_w.shape[-1]
    rw = jnp.pad(router_w.astype(F32), ((0, 0), (0, LANES - ne)))
    rwh, rwl = _split_bf16(rw)
    rb = jnp.pad(router_b.astype(F32), (0, LANES - ne), constant_values=NEG_BIG).reshape(1, LANES)
    xs, meta, w, counts = _route(x, g, sh, sc, rwh, rwl, rb)
    counts = counts[0, :ne]
    padded = (counts + GROUP_TM - 1) // GROUP_TM * GROUP_TM
    ends = jnp.cumsum(padded)
    offs = ends - padded
    experts = jnp.arange(ne, dtype=jnp.int32)
    off_of = lambda e: jnp.sum(jnp.where(e[:, None] == experts[None, :], offs[None, :], 0), axis=1)
    pos = jnp.stack([off_of(meta[:, 0]) + meta[:, 2], off_of(meta[:, 1]) + meta[:, 3]], axis=1).astype(jnp.int32)
    n_rows = TOP_K * n + ne * GROUP_TM
    nt = n_rows // GROUP_TM
    tile_start = jnp.arange(nt, dtype=jnp.int32) * GROUP_TM
    tile_e = jnp.sum(tile_start[:, None] >= ends[None, :], axis=1)
    n_live = ends[-1] // GROUP_TM
    last_e = jnp.sum((n_live - 1) * GROUP_TM >= ends)
    tile_e = jnp.where(tile_start < ends[-1], tile_e, last_e)
    tile_expert = jnp.concatenate([tile_e, n_live[None]]).astype(jnp.int32)
    xd = _dispatch(xs, pos, n_rows)
    ys = _group_ffn(xd, tile_expert, wg, wu, wd, tf=896)
    return _combine(ys, pos, x, gate, w, final_g)


def _hyena_filters(l, fw1, fb1, freq1, fw2, fb2, freq2, fw3):
    t = jnp.linspace(0.0, 1.0, l, dtype=F32)[:, None]
    w = (2.0 * math.pi / l) * jnp.arange(l, dtype=F32)[:, None]
    bands = jnp.linspace(1e-4, HYENA_BANDS - 1, HYENA_BANDS, dtype=F32)[None, :]
    z = jnp.concatenate([t, jnp.cos(bands * w), -jnp.sin(bands * w)], axis=-1)
    h = jnp.sin(freq1.astype(F32) * (jnp.dot(z, fw1.astype(F32), precision=HI) + fb1.astype(F32)))
    h = jnp.sin(freq2.astype(F32) * (jnp.dot(h, fw2.astype(F32), precision=HI) + fb2.astype(F32)))
    h = jnp.dot(h, fw3.astype(F32), precision=HI).reshape(l, 2, HYENA_ORDER, D_HYENA)
    max_decay = math.log(HYENA_DECAY_TARGET) / HYENA_FAST_DECAY_PCT
    min_decay = math.log(HYENA_DECAY_TARGET) / HYENA_SLOW_DECAY_PCT
    deltas = jnp.abs(jnp.linspace(min_decay, max_decay, D_HYENA, dtype=F32))
    h = h * jnp.exp(-t[:, :, None, None] * deltas)
    h_fwd, h_bwd = h[:, 0], h[:, 1]
    k = jnp.concatenate([h_fwd, jnp.zeros_like(h_fwd[:1]), h_bwd[:0:-1]], axis=0)
    return k / jnp.sum(jnp.abs(k), axis=0, keepdims=True)


def _hyena_spectra(l, *filter_params):
    k = _hyena_filters(l, *filter_params)
    kf = jnp.fft.rfft(k, axis=0)
    n = 2 * l
    wk = jnp.where(jnp.arange(l) == 0, 1.0, 2.0).astype(F32)[:, None] / n
    out = []
    for o in range(HYENA_ORDER):
        re = jnp.real(kf[:l, o]).astype(F32)
        im = jnp.imag(kf[:l, o]).astype(F32)
        nyq = jnp.real(kf[l, o]).astype(F32)
        kr = re * wk
        ki = (im * wk).at[0].set(0.0)
        kd = kr.at[0].set(nyq / n)
        mult = jnp.stack([kr, ki, kd], axis=0)
        out.append(jnp.stack([mult[:, 0::2], mult[:, 1::2]], axis=0))
    return out


def _fnet_maps(w_f, l):
    d = FNET_HEAD_DIM
    idx = np.arange(d)
    ang = 2.0 * np.pi * ((idx[:, None] * idx[None, :]) % d) / d
    scale = 1.0 / math.sqrt(d * l)
    cd = jnp.asarray(np.cos(ang) * scale, F32)
    sd = jnp.asarray(np.sin(ang) * scale, F32)
    wf = w_f.astype(F32)
    a = jnp.einsum('de,hef->hdf', cd, wf, precision=HI)
    bm = -jnp.einsum('de,hef->hdf', sd, wf, precision=HI)
    eye = jnp.eye(FNET_HEADS, dtype=F32)
    blk = lambda m: jnp.einsum('hdf,hg->hdgf', m, eye).reshape(D_FNET, D_FNET).astype(BF16)
    return blk(a), blk(bm)


def kernel(x, c, ctx, c_ctx, ada_w, ada_b, norm_mix_g, norm_ffn_g, w_in, w_out, fnet_w, hy_conv_w, hy_conv_b,
           hy_fw1, hy_fb1, hy_freq1, hy_fw2, hy_fb2, hy_freq2, hy_fw3, hy_bias, s5_a_re, s5_a_im, s5_log_dt,
           s5_b_re, s5_b_im, s5_c_re, s5_c_im, s5_d, s5_glu_w, s5_glu_b, ffn_w_gate, ffn_w_up, ffn_w_down,
           moe_router_w, moe_router_b, moe_w_gate, moe_w_up, moe_w_down, final_g):
    b, l, d = x.shape
    lc = ctx.shape[1]
    depth = ada_w.shape[0]
    assert l % LANES == 0 and lc % LANES == 0 and d % LANES == 0

    dft = {}
    for n in sorted({l, lc}):
        fwd = _hyena_dft(n)
        dft[n] = dict(hy_fwd=fwd, hy_inv={k: m.T for k, m in fwd.items()}, fn=_dft_cos_sin(n, n))

    cc = jnp.concatenate([c, c_ctx[None, :]], axis=0).astype(F32)
    rows = -(-(b + 1) // 8) * 8
    cc = jnp.pad(cc, ((0, rows - (b + 1)), (0, 0)))

    def mix(li, xs, sh, sc, grid_mode, s0f, s0b, s5m, states_only):
        n_tok = xs.shape[1]
        g_mix = norm_mix_g[li].reshape(1, d)
        if states_only:
            (ps,) = _inproj(xs, g_mix, sh, sc, w_in[li][:, S5_OFF:].astype(BF16),
                            jnp.zeros((3, LANES), F32), jnp.zeros((1, LANES), F32),
                            grid_mode=grid_mode, full=False)
            _, hf, hb = _s5(ps, s0f, s0b, s5m, s5_d[li])
            return None, hf, hb
        pf, qv, qx1, qx2, ps = _inproj(
            xs, g_mix, sh, sc, w_in[li].astype(BF16), hy_conv_w[li].astype(F32),
            hy_conv_b[li].astype(F32).reshape(1, -1), grid_mode=grid_mode, full=True)
        tabs = dft[n_tok]
        amat, bmat = _fnet_maps(fnet_w[li], n_tok)
        yf = _fnet(pf, tabs['fn'][0], tabs['fn'][1], amat, bmat)
        spectra = _hyena_spectra(n_tok, hy_fw1[li], hy_fb1[li], hy_freq1[li], hy_fw2[li], hy_fb2[li],
                                 hy_freq2[li], hy_fw3[li])
        z = qv
        for o, xg in enumerate((qx1, qx2)):
            pr, pi = _hy_fwd(z, tabs['hy_fwd'], spectra[o])
            z = _hy_inv(pr, pi, tabs['hy_inv'], xg, z, hy_bias[li][o].astype(F32))
        ys, hf, hb = _s5(ps, s0f, s0b, s5m, s5_d[li])
        return (yf, z, ys), hf, hb

    def out_proj(li, parts, xs, gate):
        wo = w_out[li].astype(BF16)
        return _outproj(parts[0], parts[1], parts[2], xs, gate,
                        wo[:D_FNET], wo[D_FNET:D_FNET + D_HYENA], wo[D_FNET + D_HYENA:],
                        s5_glu_w[li].astype(BF16), s5_glu_b[li].astype(F32).reshape(1, -1))

    def channel_mix(li, xs, sh, sc, gate, fin):
        i = li // 2
        g_ffn = norm_ffn_g[li].reshape(1, d)
        if li % 2 == 0:
            return _ffn(xs, g_ffn, sh, sc, gate, None, ffn_w_gate[i][None].astype(BF16),
                        ffn_w_up[i][None].astype(BF16), ffn_w_down[i][None].astype(BF16), fin, tf=1408)
        return _moe(xs, g_ffn, sh, sc, gate, moe_router_w[i], moe_router_b[i], moe_w_gate[i].astype(BF16),
                    moe_w_up[i].astype(BF16), moe_w_down[i].astype(BF16), fin)

    zero_state = jnp.zeros((b, N_S5_BLOCKS, 1, 2 * STATE_HALF), F32)
    ctx_s = ctx
    for li in range(depth):
        last = li == depth - 1
        mod = _ada(cc, ada_w[li].astype(F32), ada_b[li].astype(F32))
        lat = [m.reshape(b, 1, d) for m in jnp.split(mod[:b], N_MOD, axis=-1)]
        cm = [jnp.broadcast_to(m.reshape(1, 1, d), (b, 1, d)) for m in jnp.split(mod[b:b + 1], N_MOD, axis=-1)]
        s5m = _s5_matrices(s5_a_re[li], s5_a_im[li], s5_log_dt[li], s5_b_re[li], s5_b_im[li],
                           s5_c_re[li], s5_c_im[li])

        parts, hf_c, hb_c = mix(li, ctx_s, cm[0], cm[1], False, zero_state, zero_state, s5m, last)
        if not last:
            ctx_s = out_proj(li, parts, ctx_s, cm[2])
            ctx_s = channel_mix(li, ctx_s, cm[3], cm[4], cm[5], None)

        parts, _, _ = mix(li, x, lat[0], lat[1], True, hf_c, hb_c, s5m, False)
        x = out_proj(li, parts, x, lat[2])
        x = channel_mix(li, x, lat[3], lat[4], lat[5], final_g.reshape(1, d) if last else None)

    if depth == 0:
        raise ValueError("depth must be positive")
    return x
```

```python
import functools
import math

import numpy as np
import jax
import jax.numpy as jnp
from jax import lax
from jax.experimental import pallas as pl
from jax.experimental.pallas import tpu as pltpu

F32 = jnp.float32
BF16 = jnp.bfloat16
HI = lax.Precision.HIGHEST

LANES = 128
GRID_ROW = 64
FNET_HEADS, FNET_HEAD_DIM = 4, 64
D_FNET = FNET_HEADS * FNET_HEAD_DIM
D_HYENA = 384
HYENA_ORDER = 2
HYENA_EMB = 33
HYENA_BANDS = (HYENA_EMB - 1) // 2
HYENA_FAST_DECAY_PCT, HYENA_SLOW_DECAY_PCT, HYENA_DECAY_TARGET = 0.3, 1.5, 0.01
S5_GROUP, S5_GROUPS, S5_STATE = 16, 24, 64
D_S5 = S5_GROUP * S5_GROUPS
HY_OFF = D_FNET
S5_OFF = D_FNET + (HYENA_ORDER + 1) * D_HYENA
N_MOD = 6
TOP_K = 2
RMS_EPS = 1e-6
S5_CHUNK = 8
GROUPS_PER_BLOCK = LANES // S5_GROUP
N_S5_BLOCKS = D_S5 // LANES
STATE_HALF = GROUPS_PER_BLOCK * S5_STATE
NEG_BIG = float(np.finfo(np.float32).min)
VMEM_LIMIT = 56 * 1024 * 1024


def _cp(sem, vmem=None):
    return pltpu.CompilerParams(dimension_semantics=sem, vmem_limit_bytes=vmem)


def _dot(a, b):
    return jnp.dot(a, b, preferred_element_type=F32)


def _split_bf16(a):
    hi = a.astype(BF16)
    lo = (a - hi.astype(F32)).astype(BF16)
    return hi, lo


def _sigmoid(x):
    return 1.0 / (1.0 + jnp.exp(-x))


def _rms_mod(x, g, sh, sc):
    ms = jnp.mean(x * x, axis=-1, keepdims=True)
    return (x * lax.rsqrt(ms + RMS_EPS) * g) * (1.0 + sc) + sh


def _ada_kernel(c_ref, w_ref, b_ref, o_ref):
    c = c_ref[...]
    s = c * _sigmoid(c)
    s_hi, s_lo = _split_bf16(s)
    w_hi, w_lo = _split_bf16(w_ref[...])
    o_ref[...] = _dot(s_hi, w_hi) + _dot(s_hi, w_lo) + _dot(s_lo, w_hi) + b_ref[...]


def _ada(cc, w, b):
    r, d = cc.shape
    n = w.shape[1]
    tn = 512
    return pl.pallas_call(
        _ada_kernel,
        out_shape=jax.ShapeDtypeStruct((r, n), F32),
        grid=(n // tn,),
        in_specs=[pl.BlockSpec((r, d), lambda j: (0, 0)),
                  pl.BlockSpec((d, tn), lambda j: (0, j)),
                  pl.BlockSpec((1, tn), lambda j: (0, j))],
        out_specs=pl.BlockSpec((r, tn), lambda j: (0, j)),
        compiler_params=_cp(("parallel",)),
        name="ada_mod",
    )(cc, w, b.reshape(1, n))


def _expand_kernel(a1_ref, a2_ref, b1_ref, b2_ref, o_ref, *, nh):
    a2 = a2_ref[...]
    b2 = b2_ref[...]
    for h in range(nh):
        blk = a1_ref[:, h:h + 1] * a2 + b1_ref[:, h:h + 1] * b2
        o_ref[:, h * LANES:(h + 1) * LANES] = blk.astype(o_ref.dtype)


def _expand(a1, a2, b1, b2):
    r, nh = a1.shape
    tr = min(r, 256)
    return pl.pallas_call(
        functools.partial(_expand_kernel, nh=nh),
        out_shape=jax.ShapeDtypeStruct((r, nh * LANES), BF16),
        grid=(r // tr,),
        in_specs=[pl.BlockSpec((tr, nh), lambda i: (i, 0)),
                  pl.BlockSpec((tr, LANES), lambda i: (i, 0)),
                  pl.BlockSpec((tr, nh), lambda i: (i, 0)),
                  pl.BlockSpec((tr, LANES), lambda i: (i, 0))],
        out_specs=pl.BlockSpec((tr, nh * LANES), lambda i: (i, 0)),
        compiler_params=_cp(("parallel",)),
        name="dft_expand",
    )(a1, a2, b1, b2)


def _trig_tables(n_rows, period):
    nh = n_rows // LANES
    r = np.arange(n_rows, dtype=np.int64)[:, None]
    hi = (r * (np.arange(nh, dtype=np.int64)[None, :] * LANES)) % period
    lo = (r * np.arange(LANES, dtype=np.int64)[None, :]) % period
    ang_hi = 2.0 * np.pi * hi / period
    ang_lo = 2.0 * np.pi * lo / period
    return np.cos(ang_hi), np.sin(ang_hi), np.cos(ang_lo), np.sin(ang_lo)


def _dft_cos_sin(n, period):
    c1, s1, c2, s2 = (jnp.asarray(t, F32) for t in _trig_tables(n, period))
    return _expand(c1, c2, -s1, s2), _expand(s1, c2, c1, s2)


def _hyena_dft(l):
    h = l // 2
    nh = h // LANES
    period = 4 * l
    kp = np.arange(h, dtype=np.int64)[:, None]
    f = lambda t: jnp.asarray(t, F32)
    out = {}
    for name, k in (("e", 2 * kp), ("o", 2 * kp + 1)):
        qa = (k * (2 * LANES) * np.arange(nh, dtype=np.int64)[None, :]) % period
        qb = (k * (2 * np.arange(LANES, dtype=np.int64)[None, :] + 1)) % period
        aa = 2.0 * np.pi * qa / period
        ab = 2.0 * np.pi * qb / period
        c1, s1, c2, s2 = np.cos(aa), np.sin(aa), np.cos(ab), np.sin(ab)
        a1, a2, b1 = -s1, c2.copy(), -c1
        if name == "e":
            a1[0, :] = 1.0
            a2[0, :] = (-1.0) ** np.arange(LANES)
            b1[0, :] = 0.0
        out["c" + name] = _expand(f(c1), f(c2), f(-s1), f(s2))
        out["s" + name] = _expand(f(a1), f(a2), f(b1), f(s2))
    return out


def _flip_matrix(n):
    return jnp.asarray(np.eye(n, dtype=np.float32)[::-1], BF16)


def _flip_rows(v, j):
    sb = j.shape[0]
    nblk = v.shape[0] // sb
    v_hi, v_lo = _split_bf16(v)
    parts = []
    for s in range(nblk):
        r = slice((nblk - 1 - s) * sb, (nblk - s) * sb)
        parts.append(_dot(j, v_hi[r]) + _dot(j, v_lo[r]))
    return parts[0] if nblk == 1 else jnp.concatenate(parts, axis=0)


def _inproj_kernel(x_ref, g_ref, sh_ref, sc_ref, w_ref, cw_ref, cb_ref, j_ref, *out_refs, tm, nt, grid_mode, full):
    h = _rms_mod(x_ref[...], g_ref[...], sh_ref[...], sc_ref[...])
    acc = _dot(h.astype(BF16), w_ref[...])
    if not full:
        out_refs[0][...] = acc
        return
    pf_ref, qv_ref, qx1_ref, qx2_ref, ps_ref = out_refs
    pf_ref[...] = acc[:, :HY_OFF]
    ps_ref[...] = acc[:, S5_OFF:]
    hy = acc[:, HY_OFF:S5_OFF]
    row = lax.broadcasted_iota(jnp.int32, (tm, 1), 0)
    if grid_mode:
        first = (row % GRID_ROW) == 0
        last = (row % GRID_ROW) == GRID_ROW - 1
    else:
        first = row == 0
        last = row == tm - 1
    prev = jnp.where(first, 0.0, pltpu.roll(hy, 1, 0))
    nxt = jnp.where(last, 0.0, pltpu.roll(hy, tm - 1, 0))
    q = prev * cw_ref[0:1, :] + hy * cw_ref[1:2, :] + nxt * cw_ref[2:3, :] + cb_ref[...]
    folded = (qv_ref, qx1_ref, qx2_ref)
    cols = [slice(n * D_HYENA, (n + 1) * D_HYENA) for n in range(len(folded))]
    if nt == 1:
        half = tm // 2
        q_hi = _flip_rows(q[half:], j_ref[...])
        for r, c in zip(folded, cols):
            r[0] = q[:half, c]
            r[1] = q_hi[:, c]
        return
    i = pl.program_id(1)

    @pl.when(i < nt // 2)
    def _():
        for r, c in zip(folded, cols):
            r[...] = q[:, c]

    @pl.when(i >= nt // 2)
    def _():
        qf = _flip_rows(q, j_ref[...])
        for r, c in zip(folded, cols):
            r[...] = qf[:, c]


def _fold_spec(tm, nt, width):
    if nt == 1:
        return pl.BlockSpec((None, 2, tm // 2, width), lambda bi, i: (bi, 0, 0, 0))
    hn = nt // 2
    return pl.BlockSpec((None, None, tm, width),
                        lambda bi, i: (bi, i // hn, jnp.where(i < hn, i, nt - 1 - i), 0))


def _inproj(x, g, sh, sc, w, cw, cb, *, grid_mode, full):
    b, l, d = x.shape
    n = w.shape[1]
    tm = min(l, 512)
    nt = l // tm
    if not grid_mode:
        assert tm == l
    else:
        assert tm % GRID_ROW == 0
    assert nt == 1 or nt % 2 == 0
    row = lambda bi, i: (bi, i, 0)
    vec = lambda bi, i: (bi, 0, 0)
    const = lambda bi, i: (0, 0)
    flip = _flip_matrix(min(256, tm // 2 if nt == 1 else tm))
    plain = lambda wd: (jax.ShapeDtypeStruct((b, l, wd), F32), pl.BlockSpec((None, tm, wd), row))
    fold = lambda wd: (jax.ShapeDtypeStruct((b, 2, l // 2, wd), F32), _fold_spec(tm, nt, wd))
    outs = ([plain(D_FNET), fold(D_HYENA), fold(D_HYENA), fold(D_HYENA), plain(D_S5)] if full else [plain(n)])
    return pl.pallas_call(
        functools.partial(_inproj_kernel, tm=tm, nt=nt, grid_mode=grid_mode, full=full),
        out_shape=tuple(o[0] for o in outs),
        grid=(b, nt),
        in_specs=[pl.BlockSpec((None, tm, d), row),
                  pl.BlockSpec((1, d), const),
                  pl.BlockSpec((None, 1, d), vec),
                  pl.BlockSpec((None, 1, d), vec),
                  pl.BlockSpec((d, n), const),
                  pl.BlockSpec(cw.shape, const),
                  pl.BlockSpec(cb.shape, const),
                  pl.BlockSpec(flip.shape, const)],
        out_specs=tuple(o[1] for o in outs),
        compiler_params=_cp(("parallel", "parallel"), VMEM_LIMIT),
        name="inproj",
    )(x, g, sh, sc, w, cw, cb, flip)


def _fnet_kernel(x_ref, mc_ref, ms_ref, a_ref, b_ref, o_ref, accc, accs, *, nb):
    k = pl.program_id(2)

    @pl.when(k == 0)
    def _():
        accc[...] = jnp.zeros_like(accc)
        accs[...] = jnp.zeros_like(accs)

    mc = mc_ref[...]
    ms = ms_ref[...]
    for bi in range(nb):
        xb = x_ref[bi].astype(BF16)
        accc[bi] += _dot(mc, xb)
        accs[bi] += _dot(ms, xb)

    @pl.when(k == pl.num_programs(2) - 1)
    def _():
        for bi in range(nb):
            o_ref[bi] = (_dot(accc[bi].astype(BF16), a_ref[...])
                         + _dot(accs[bi].astype(BF16), b_ref[...]))


def _fnet(pf, mc, ms, amat, bmat):
    b, l, c = pf.shape
    nb = min(b, 2)
    t = min(l, 1024)
    return pl.pallas_call(
        functools.partial(_fnet_kernel, nb=nb),
        out_shape=jax.ShapeDtypeStruct((b, l, c), F32),
        grid=(b // nb, l // t, l // t),
        in_specs=[pl.BlockSpec((nb, t, c), lambda bb, i, k: (bb, k, 0)),
                  pl.BlockSpec((t, t), lambda bb, i, k: (i, k)),
                  pl.BlockSpec((t, t), lambda bb, i, k: (i, k)),
                  pl.BlockSpec((c, c), lambda bb, i, k: (0, 0)),
                  pl.BlockSpec((c, c), lambda bb, i, k: (0, 0))],
        out_specs=pl.BlockSpec((nb, t, c), lambda bb, i, k: (bb, i, 0)),
        scratch_shapes=[pltpu.VMEM((nb, t, c), F32), pltpu.VMEM((nb, t, c), F32)],
        compiler_params=_cp(("parallel", "parallel", "arbitrary"), VMEM_LIMIT),
        name="fnet_dft",
    )(pf, mc, ms, amat, bmat)


def _hy_fwd_kernel(z_ref, ce_ref, se_ref, co_ref, so_ref, *rest, nb, filtered):
    if filtered:
        kf_ref, pr_ref, pi_ref, acc = rest
    else:
        pr_ref, pi_ref, acc = rest
    k = pl.program_id(2)

    @pl.when(k == 0)
    def _():
        acc[...] = jnp.zeros_like(acc)

    c = z_ref.shape[-1]
    ev = jnp.concatenate([(z_ref[bi, 0] + z_ref[bi, 1]).astype(BF16) for bi in range(nb)], axis=1)
    od = jnp.concatenate([(z_ref[bi, 0] - z_ref[bi, 1]).astype(BF16) for bi in range(nb)], axis=1)
    acc[0] += _dot(ce_ref[...], ev)
    acc[1] += _dot(se_ref[...], od)
    acc[2] += _dot(co_ref[...], od)
    acc[3] += _dot(so_ref[...], ev)

    @pl.when(k == pl.num_programs(2) - 1)
    def _():
        for bi in range(nb):
            cols = slice(bi * c, (bi + 1) * c)
            for p in range(2):
                zr, zi = acc[2 * p, :, cols], acc[2 * p + 1, :, cols]
                if filtered:
                    kr, ki, kd = kf_ref[p, 0], kf_ref[p, 1], kf_ref[p, 2]
                    zr, zi = zr * kr - zi * ki, zr * ki + zi * kd
                pr_ref[bi, p] = zr.astype(pr_ref.dtype)
                pi_ref[bi, p] = zi.astype(pi_ref.dtype)


def _hy_tiles(h):
    return min(h, 512), min(h, 1024)


def _hy_fwd(z, mats, kf, out_dtype=BF16):
    b, _, h, c = z.shape
    nb = min(b, 2)
    ti, tk = _hy_tiles(h)
    mspec = pl.BlockSpec((ti, tk), lambda bb, i, k: (i, k))
    ospec = pl.BlockSpec((nb, 2, ti, c), lambda bb, i, k: (bb, 0, i, 0))
    filtered = kf is not None
    extra_specs = [pl.BlockSpec((2, 3, ti, c), lambda bb, i, k: (0, 0, i, 0))] if filtered else []
    extra_args = (kf,) if filtered else ()
    return pl.pallas_call(
        functools.partial(_hy_fwd_kernel, nb=nb, filtered=filtered),
        out_shape=(jax.ShapeDtypeStruct((b, 2, h, c), out_dtype), jax.ShapeDtypeStruct((b, 2, h, c), out_dtype)),
        grid=(b // nb, h // ti, h // tk),
        in_specs=[pl.BlockSpec((nb, 2, tk, c), lambda bb, i, k: (bb, 0, k, 0)),
                  mspec, mspec, mspec, mspec] + extra_specs,
        out_specs=(ospec, ospec),
        scratch_shapes=[pltpu.VMEM((4, ti, nb * c), F32)],
        compiler_params=_cp(("parallel", "parallel", "arbitrary"), VMEM_LIMIT),
        name="hyena_fwd_dft",
    )(z, mats["ce"], mats["se"], mats["co"], mats["so"], *extra_args)


def _hy_inv_kernel(pr_ref, pi_ref, ce_ref, se_ref, co_ref, so_ref, xg_ref, zin_ref, bias_ref, o_ref, acc, *, nb):
    k = pl.program_id(2)

    @pl.when(k == 0)
    def _():
        acc[...] = jnp.zeros_like(acc)

    c = xg_ref.shape[-1]
    side = lambda ref, p: jnp.concatenate([ref[bi, p] for bi in range(nb)], axis=1)
    acc[0] += _dot(ce_ref[...], side(pr_ref, 0)) + _dot(so_ref[...], side(pi_ref, 1))
    acc[1] += _dot(se_ref[...], side(pi_ref, 0)) + _dot(co_ref[...], side(pr_ref, 1))

    @pl.when(k == pl.num_programs(2) - 1)
    def _():
        bias = bias_ref[...]
        for bi in range(nb):
            cols = slice(bi * c, (bi + 1) * c)
            u, v = acc[0, :, cols], acc[1, :, cols]
            o_ref[bi, 0] = xg_ref[bi, 0] * (u + v + zin_ref[bi, 0] * bias)
            o_ref[bi, 1] = xg_ref[bi, 1] * (u - v + zin_ref[bi, 1] * bias)


def _hy_inv(pr, pi, mats_t, xg, zin, bias):
    b, _, h, c = xg.shape
    nb = min(b, 2)
    ti, tk = _hy_tiles(h)
    pspec = pl.BlockSpec((nb, 2, tk, c), lambda bb, i, k: (bb, 0, k, 0))
    mspec = pl.BlockSpec((ti, tk), lambda bb, i, k: (i, k))
    ospec = pl.BlockSpec((nb, 2, ti, c), lambda bb, i, k: (bb, 0, i, 0))
    return pl.pallas_call(
        functools.partial(_hy_inv_kernel, nb=nb),
        out_shape=jax.ShapeDtypeStruct((b, 2, h, c), F32),
        grid=(b // nb, h // ti, h // tk),
        in_specs=[pspec, pspec, mspec, mspec, mspec, mspec, ospec, ospec,
                  pl.BlockSpec((1, c), lambda bb, i, k: (0, 0))],
        out_specs=ospec,
        scratch_shapes=[pltpu.VMEM((2, ti, nb * c), F32)],
        compiler_params=_cp(("parallel", "parallel", "arbitrary"), VMEM_LIMIT),
        name="hyena_inv_dft",
    )(pr, pi, mats_t["ce"], mats_t["se"], mats_t["co"], mats_t["so"], xg, zin, bias.reshape(1, c))


def _cmul(a, h):
    ar, ai = a[:, :STATE_HALF], a[:, STATE_HALF:]
    hr, hi = h[:, :STATE_HALF], h[:, STATE_HALF:]
    return jnp.concatenate([ar * hr - ai * hi, ar * hi + ai * hr], axis=1)


def _s5_kernel(u_ref, s0f_ref, s0b_ref, mi_ref, mof_ref, mob_ref, mnf_ref, mnb_ref, atf_ref, atb_ref,
               d_ref, y_ref, hf_ref, hb_ref, x_scr, sf_scr, sb_scr, hinf_scr, hinb_scr, *, nc, t_chunk):
    for s in range(t_chunk):
        x_scr[:, s * LANES:(s + 1) * LANES] = u_ref[pl.ds(s, nc, stride=t_chunk), :].astype(BF16)
    x = x_scr[...]
    sf_scr[...] = _dot(x, mof_ref[...])
    sb_scr[...] = _dot(x, mob_ref[...])
    atf = atf_ref[...]
    atb = atb_ref[...]

    def body(c, carry):
        hf, hb = carry
        cb = nc - 1 - c
        hinf_scr[pl.ds(c, 1), :] = hf
        hinb_scr[pl.ds(cb, 1), :] = hb
        hf = _cmul(atf, hf) + sf_scr[pl.ds(c, 1), :]
        hb = _cmul(atb, hb) + sb_scr[pl.ds(cb, 1), :]
        return hf, hb

    hf, hb = lax.fori_loop(0, nc, body, (s0f_ref[...], s0b_ref[...]))
    hf_ref[...] = hf
    hb_ref[...] = hb
    y = (_dot(x, mi_ref[...])
         + _dot(hinf_scr[...].astype(BF16), mnf_ref[...])
         + _dot(hinb_scr[...].astype(BF16), mnb_ref[...]))
    d = d_ref[...]
    for t in range(t_chunk):
        rows = pl.ds(t, nc, stride=t_chunk)
        y_ref[rows, :] = y[:, t * LANES:(t + 1) * LANES] + d * u_ref[rows, :]


def _s5(u, s0f, s0b, mats, d_skip):
    b, l, _ = u.shape
    tc = S5_CHUNK
    nc = l // tc
    kin = tc * LANES
    ns = 2 * STATE_HALF
    mi, mof, mob, mnf, mnb, atf, atb = mats
    st_spec = pl.BlockSpec((None, None, 1, ns), lambda j, bi: (bi, j, 0, 0))
    w3 = lambda s1, s2: pl.BlockSpec((None, s1, s2), lambda j, bi: (j, 0, 0))
    u_spec = pl.BlockSpec((None, l, LANES), lambda j, bi: (bi, 0, j))
    return pl.pallas_call(
        functools.partial(_s5_kernel, nc=nc, t_chunk=tc),
        out_shape=(jax.ShapeDtypeStruct((b, l, D_S5), F32),
                   jax.ShapeDtypeStruct((b, N_S5_BLOCKS, 1, ns), F32),
                   jax.ShapeDtypeStruct((b, N_S5_BLOCKS, 1, ns), F32)),
        grid=(N_S5_BLOCKS, b),
        in_specs=[u_spec, st_spec, st_spec,
                  w3(kin, kin), w3(kin, ns), w3(kin, ns), w3(ns, kin), w3(ns, kin),
                  w3(1, ns), w3(1, ns),
                  pl.BlockSpec((1, LANES), lambda j, bi: (0, j))],
        out_specs=(u_spec, st_spec, st_spec),
        scratch_shapes=[pltpu.VMEM((nc, kin), BF16),
                        pltpu.VMEM((nc, ns), F32), pltpu.VMEM((nc, ns), F32),
                        pltpu.VMEM((nc, ns), F32), pltpu.VMEM((nc, ns), F32)],
        compiler_params=_cp(("parallel", "parallel"), VMEM_LIMIT),
        name="s5_scan",
    )(u, s0f, s0b, mi, mof, mob, mnf, mnb, atf, atb, d_skip.reshape(1, D_S5))


def _s5_matrices(a_re, a_im, log_dt, b_re, b_im, c_re, c_im):
    tc = S5_CHUNK
    g, p, hd = S5_GROUPS, S5_STATE, S5_GROUP
    lam = lax.complex(a_re.astype(F32), a_im.astype(F32))
    dt = jnp.exp(log_dt.astype(F32))[..., None]
    a_bar = jnp.exp(lam * dt)
    b_bar = ((a_bar - 1.0) / lam)[..., None] * lax.complex(b_re.astype(F32), b_im.astype(F32))
    cc = lax.complex(c_re.astype(F32), c_im.astype(F32))
    ks = jnp.arange(tc + 1, dtype=F32)
    pw = jnp.exp((lam * dt)[:, :, None, :] * ks[None, None, :, None])
    kern = jnp.real(jnp.einsum('dgip,dgkp,dgpj->dgkij', cc, pw[:, :, :tc], b_bar, precision=HI))
    s_idx = jnp.arange(tc)[:, None]
    t_idx = jnp.arange(tc)[None, :]
    lag = t_idx - s_idx
    kf = jnp.where((lag >= 0)[None, :, :, None, None], kern[0][:, jnp.clip(lag, 0, tc - 1)], 0.0)
    kb = jnp.where((lag <= 0)[None, :, :, None, None], kern[1][:, jnp.clip(-lag, 0, tc - 1)], 0.0)
    ktot = kf + kb
    eye = jnp.eye(GROUPS_PER_BLOCK, dtype=F32)
    kt = ktot.reshape(N_S5_BLOCKS, GROUPS_PER_BLOCK, tc, tc, hd, hd)
    kt = jnp.transpose(kt, (0, 2, 1, 5, 3, 4))
    mi = kt[:, :, :, :, :, None, :] * eye[None, None, :, None, None, :, None]
    mi = mi.reshape(N_S5_BLOCKS, tc * LANES, tc * LANES)

    def parts(z):
        return jnp.stack([jnp.real(z), jnp.imag(z)], axis=0)

    def out_mat(d, powers):
        z = powers[:, :, :, None] * b_bar[d][:, None, :, :]
        z = parts(z).reshape(2, N_S5_BLOCKS, GROUPS_PER_BLOCK, tc, p, hd)
        z = jnp.transpose(z, (1, 3, 2, 5, 0, 4))
        m = z[:, :, :, :, :, None, :] * eye[None, None, :, None, None, :, None]
        return m.reshape(N_S5_BLOCKS, tc * LANES, 2 * STATE_HALF)

    def in_mat(d, powers):
        z = cc[d][:, None, :, :] * powers[:, :, None, :]
        z = jnp.stack([jnp.real(z), -jnp.imag(z)], axis=0)
        z = z.reshape(2, N_S5_BLOCKS, GROUPS_PER_BLOCK, tc, hd, p)
        z = jnp.transpose(z, (1, 0, 2, 5, 3, 4))
        m = z[:, :, :, :, :, None, :] * eye[None, None, :, None, None, :, None]
        return m.reshape(N_S5_BLOCKS, 2 * STATE_HALF, tc * LANES)

    rev = jnp.arange(tc - 1, -1, -1)
    mof = out_mat(0, pw[0][:, rev])
    mob = out_mat(1, pw[1][:, :tc])
    mnf = in_mat(0, pw[0][:, 1:tc + 1])
    mnb = in_mat(1, pw[1][:, tc - jnp.arange(tc)])

    def step(d):
        z = parts(pw[d][:, tc]).reshape(2, N_S5_BLOCKS, 1, STATE_HALF)
        return jnp.transpose(z, (1, 2, 0, 3)).reshape(N_S5_BLOCKS, 1, 2 * STATE_HALF)

    bf = lambda m: m.astype(BF16)
    return bf(mi), bf(mof), bf(mob), bf(mnf), bf(mnb), step(0), step(1)


def _outproj_kernel(yf_ref, yh_ref, ys_ref, x_ref, gate_ref, w1_ref, w2_ref, w3_ref, gw_ref, gb_ref, j_ref, o_ref,
                    *, nt):
    y = ys_ref[...]
    y = 0.5 * y * (1.0 + jnp.tanh(math.sqrt(2.0 / math.pi) * (y + 0.044715 * (y * y * y))))
    y = y * _sigmoid(_dot(y.astype(BF16), gw_ref[...]) + gb_ref[...])
    acc = _dot(yf_ref[...].astype(BF16), w1_ref[...]) + _dot(y.astype(BF16), w3_ref[...])

    def finish(yh):
        o_ref[...] = x_ref[...] + gate_ref[...] * (acc + _dot(yh.astype(BF16), w2_ref[...]))

    if nt == 1:
        finish(jnp.concatenate([yh_ref[0], _flip_rows(yh_ref[1], j_ref[...])], axis=0))
        return
    i = pl.program_id(1)

    @pl.when(i < nt // 2)
    def _():
        finish(yh_ref[...])

    @pl.when(i >= nt // 2)
    def _():
        finish(_flip_rows(yh_ref[...], j_ref[...]))


def _outproj(yf, yh, ys, x, gate, w1, w2, w3, gw, gb):
    b, l, d = x.shape
    tm = min(l, 512)
    nt = l // tm
    row = lambda bi, i: (bi, i, 0)
    const = lambda bi, i: (0, 0)
    full = lambda a: pl.BlockSpec(a.shape, const)
    flip = _flip_matrix(min(256, tm // 2 if nt == 1 else tm))
    return pl.pallas_call(
        functools.partial(_outproj_kernel, nt=nt),
        out_shape=jax.ShapeDtypeStruct((b, l, d), F32),
        grid=(b, nt),
        in_specs=[pl.BlockSpec((None, tm, yf.shape[2]), row),
                  _fold_spec(tm, nt, yh.shape[3]),
                  pl.BlockSpec((None, tm, ys.shape[2]), row),
                  pl.BlockSpec((None, tm, d), row),
                  pl.BlockSpec((None, 1, d), lambda bi, i: (bi, 0, 0)),
                  full(w1), full(w2), full(w3), full(gw), full(gb), full(flip)],
        out_specs=pl.BlockSpec((None, tm, d), row),
        compiler_params=_cp(("parallel", "parallel"), VMEM_LIMIT),
        name="outproj",
    )(yf, yh, ys, x, gate, w1, w2, w3, gw, gb, flip)


def _ffn_kernel(x_ref, g_ref, sh_ref, sc_ref, gate_ref, rwh_ref, rwl_ref, rb_ref, wg_ref, wu_ref, wd_ref,
                fg_ref, o_ref, xn_scr, acc_scr, comb_scr, *, moe, final_norm):
    e = pl.program_id(2)
    f = pl.program_id(3)
    first = jnp.logical_and(e == 0, f == 0)
    last = jnp.logical_and(e == pl.num_programs(2) - 1, f == pl.num_programs(3) - 1)

    @pl.when(first)
    def _():
        h = _rms_mod(x_ref[...], g_ref[...], sh_ref[...], sc_ref[...])
        xn_scr[...] = h.astype(BF16)
        acc_scr[...] = jnp.zeros_like(acc_scr)
        if moe:
            h_hi, h_lo = _split_bf16(h)
            rwh = rwh_ref[...]
            lg = _dot(h_hi, rwh) + _dot(h_hi, rwl_ref[...]) + _dot(h_lo, rwh) + rb_ref[...]
            lane = lax.broadcasted_iota(jnp.int32, lg.shape, 1)
            m1 = jnp.max(lg, axis=-1, keepdims=True)
            i1 = jnp.min(jnp.where(lg == m1, lane, LANES), axis=-1, keepdims=True)
            lg2 = jnp.where(lane == i1, NEG_BIG, lg)
            m2 = jnp.max(lg2, axis=-1, keepdims=True)
            i2 = jnp.min(jnp.where(lg2 == m2, lane, LANES), axis=-1, keepdims=True)
            ex = jnp.exp(m2 - m1)
            den = 1.0 + ex
            comb_scr[...] = (jnp.where(lane == i1, 1.0 / den, 0.0)
                             + jnp.where(lane == i2, ex / den, 0.0))

    xn = xn_scr[...]
    gt = _dot(xn, wg_ref[...])
    up = _dot(xn, wu_ref[...])
    hmid = gt * _sigmoid(gt) * up
    if moe:
        comb = comb_scr[...]
        lane = lax.broadcasted_iota(jnp.int32, comb.shape, 1)
        hmid = hmid * jnp.sum(jnp.where(lane == e, comb, 0.0), axis=-1, keepdims=True)
    acc_scr[...] += _dot(hmid.astype(BF16), wd_ref[...])

    @pl.when(last)
    def _():
        y = x_ref[...] + gate_ref[...] * acc_scr[...]
        if final_norm:
            ms = jnp.mean(y * y, axis=-1, keepdims=True)
            y = y * lax.rsqrt(ms + RMS_EPS) * fg_ref[...]
        o_ref[...] = y


def _ffn(x, g, sh, sc, gate, router, wg, wu, wd, final_g, *, tf):
    b, l, d = x.shape
    ne, _, ff = wg.shape
    moe = router is not None
    final_norm = final_g is not None
    tm = min(l, 1024)
    if moe:
        rwh, rwl, rb = router
    else:
        rwh = rwl = jnp.zeros((d, LANES), BF16)
        rb = jnp.zeros((1, LANES), F32)
    fg = final_g if final_norm else jnp.ones((1, d), F32)
    row = lambda bi, i, e, f: (bi, i, 0)
    vec = lambda bi, i, e, f: (bi, 0, 0)
    const = lambda bi, i, e, f: (0, 0)
    return pl.pallas_call(
        functools.partial(_ffn_kernel, moe=moe, final_norm=final_norm),
        out_shape=jax.ShapeDtypeStruct((b, l, d), F32),
        grid=(b, l // tm, ne, ff // tf),
        in_specs=[pl.BlockSpec((None, tm, d), row),
                  pl.BlockSpec((1, d), const),
                  pl.BlockSpec((None, 1, d), vec),
                  pl.BlockSpec((None, 1, d), vec),
                  pl.BlockSpec((None, 1, d), vec),
                  pl.BlockSpec((d, LANES), const),
                  pl.BlockSpec((d, LANES), const),
                  pl.BlockSpec((1, LANES), const),
                  pl.BlockSpec((None, d, tf), lambda bi, i, e, f: (e, 0, f)),
                  pl.BlockSpec((None, d, tf), lambda bi, i, e, f: (e, 0, f)),
                  pl.BlockSpec((None, tf, d), lambda bi, i, e, f: (e, f, 0)),
                  pl.BlockSpec((1, d), const)],
        out_specs=pl.BlockSpec((None, tm, d), row),
        scratch_shapes=[pltpu.VMEM((tm, d), BF16), pltpu.VMEM((tm, d), F32), pltpu.VMEM((tm, LANES), F32)],
        compiler_params=_cp(("parallel", "parallel", "arbitrary", "arbitrary"), VMEM_LIMIT),
        name="moe_ffn" if moe else "dense_ffn",
    )(x, g, sh, sc, gate, rwh, rwl, rb, wg, wu, wd, fg)


SLAB = 8
ROUTE_TM = 512
GROUP_TM = 1024


def _to_slabs(ref, val, n):
    for s in range(SLAB):
        ref[pl.ds(s, n, stride=SLAB), :] = val[:, s * LANES:(s + 1) * LANES]


def _from_slabs(ref, n):
    return jnp.concatenate([ref[pl.ds(s, n, stride=SLAB), :] for s in range(SLAB)], axis=1)


def _route_kernel(x_ref, g_ref, sh_ref, sc_ref, rwh_ref, rwl_ref, rb_ref, tri_ref,
                  xs_ref, meta_ref, gate_ref, cnt_ref, carry_scr, *, tm):
    step = pl.program_id(0) * pl.num_programs(1) + pl.program_id(1)

    @pl.when(step == 0)
    def _():
        carry_scr[...] = jnp.zeros_like(carry_scr)

    h = _rms_mod(x_ref[...], g_ref[...], sh_ref[...], sc_ref[...])
    _to_slabs(xs_ref, h, tm)
    h_hi, h_lo = _split_bf16(h)
    rwh = rwh_ref[...]
    lg = _dot(h_hi, rwh) + _dot(h_hi, rwl_ref[...]) + _dot(h_lo, rwh) + rb_ref[...]
    lane = lax.broadcasted_iota(jnp.int32, lg.shape, 1)
    m1 = jnp.max(lg, axis=-1, keepdims=True)
    i1 = jnp.min(jnp.where(lg == m1, lane, LANES), axis=-1, keepdims=True)
    lg2 = jnp.where(lane == i1, NEG_BIG, lg)
    m2 = jnp.max(lg2, axis=-1, keepdims=True)
    i2 = jnp.min(jnp.where(lg2 == m2, lane, LANES), axis=-1, keepdims=True)
    ex = jnp.exp(m2 - m1)
    den = 1.0 + ex
    gate_ref[...] = jnp.where(lane == 0, 1.0 / den, jnp.where(lane == 1, ex / den, 0.0))
    sel1 = lane == i1
    sel2 = lane == i2
    tri = tri_ref[...]
    p1 = _dot(tri, sel1.astype(BF16))
    p2 = _dot(tri, sel2.astype(BF16))
    tot1 = p1[tm - 1:tm, :]
    tot2 = p2[tm - 1:tm, :]
    carry = carry_scr[...]
    r1 = jnp.sum(jnp.where(sel1, carry + p1 - 1.0, 0.0), axis=-1, keepdims=True)
    r2 = jnp.sum(jnp.where(sel2, carry + tot1 + p2 - 1.0, 0.0), axis=-1, keepdims=True)
    carry = carry + tot1 + tot2
    carry_scr[...] = carry
    cnt_ref[...] = carry.astype(jnp.int32)
    meta_ref[...] = jnp.where(lane == 0, i1, jnp.where(lane == 1, i2, jnp.where(
        lane == 2, r1.astype(jnp.int32), jnp.where(lane == 3, r2.astype(jnp.int32), 0))))


def _route(x, g, sh, sc, rwh, rwl, rb):
    b, l, d = x.shape
    tm = min(l, ROUTE_TM)
    n = b * l
    nl = l // tm
    tri = jnp.tril(jnp.ones((tm, tm), F32)).astype(BF16)
    row = lambda bi, i: (bi, i, 0)
    vec = lambda bi, i: (bi, 0, 0)
    const = lambda bi, i: (0, 0)
    flat = lambda bi, i: (bi * nl + i, 0)
    return pl.pallas_call(
        functools.partial(_route_kernel, tm=tm),
        out_shape=(jax.ShapeDtypeStruct((n * SLAB, LANES), F32),
                   jax.ShapeDtypeStruct((n, LANES), jnp.int32),
                   jax.ShapeDtypeStruct((n, LANES), F32),
                   jax.ShapeDtypeStruct((1, LANES), jnp.int32)),
        grid=(b, nl),
        in_specs=[pl.BlockSpec((None, tm, d), row),
                  pl.BlockSpec((1, d), const),
                  pl.BlockSpec((None, 1, d), vec),
                  pl.BlockSpec((None, 1, d), vec),
                  pl.BlockSpec((d, LANES), const),
                  pl.BlockSpec((d, LANES), const),
                  pl.BlockSpec((1, LANES), const),
                  pl.BlockSpec((tm, tm), const)],
        out_specs=(pl.BlockSpec((tm * SLAB, LANES), flat),
                   pl.BlockSpec((tm, LANES), flat),
                   pl.BlockSpec((tm, LANES), flat),
                   pl.BlockSpec((1, LANES), const)),
        scratch_shapes=[pltpu.VMEM((1, LANES), F32)],
        compiler_params=_cp(("arbitrary", "arbitrary"), VMEM_LIMIT),
        name="moe_route",
    )(x, g, sh, sc, rwh, rwl, rb, tri)


def _dispatch_kernel(pos_hbm, xs_ref, init_hbm, out_hbm, idx_smem, isem, sem, *, tm):
    del init_hbm
    step = pl.program_id(0)
    icopy = pltpu.make_async_copy(pos_hbm.at[pl.ds(step * 2 * tm, 2 * tm)], idx_smem, isem)
    icopy.start()
    icopy.wait()

    def row_copy(i, k):
        return pltpu.make_async_copy(xs_ref.at[pl.ds(i * SLAB, SLAB), :], out_hbm.at[idx_smem[2 * i + k]], sem)

    def issue(i, carry):
        row_copy(i, 0).start()
        row_copy(i, 1).start()
        return carry

    def drain(i, carry):
        row_copy(i, 0).wait()
        row_copy(i, 1).wait()
        return carry

    lax.fori_loop(0, tm, issue, 0, unroll=8)
    lax.fori_loop(0, tm, drain, 0, unroll=8)


def _dispatch(xs, pos, n_rows):
    n = xs.shape[0] // SLAB
    tm = min(n, ROUTE_TM)
    init = jnp.zeros((n_rows, SLAB, LANES), F32)
    return pl.pallas_call(
        functools.partial(_dispatch_kernel, tm=tm),
        out_shape=jax.ShapeDtypeStruct((n_rows, SLAB, LANES), F32),
        grid=(n // tm,),
        in_specs=[pl.BlockSpec(memory_space=pl.ANY),
                  pl.BlockSpec((tm * SLAB, LANES), lambda i: (i, 0)),
                  pl.BlockSpec(memory_space=pl.ANY)],
        out_specs=pl.BlockSpec(memory_space=pl.ANY),
        scratch_shapes=[pltpu.SMEM((2 * tm,), jnp.int32), pltpu.SemaphoreType.DMA, pltpu.SemaphoreType.DMA],
        input_output_aliases={2: 0},
        compiler_params=_cp(("arbitrary",), VMEM_LIMIT),
        name="moe_dispatch",
    )(pos.reshape(-1), xs, init)


def _group_ffn_kernel(te_ref, x_ref, wg_ref, wu_ref, wd_ref, o_ref, xn_scr, acc_scr, *, tm, nt):
    t = pl.program_id(0)
    f = pl.program_id(1)
    valid = t < te_ref[nt]
    last = f == pl.num_programs(1) - 1

    @pl.when(jnp.logical_and(valid, f == 0))
    def _():
        xn_scr[...] = _from_slabs(x_ref, tm).astype(BF16)
        acc_scr[...] = jnp.zeros_like(acc_scr)

    @pl.when(valid)
    def _():
        xn = xn_scr[...]
        gt = _dot(xn, wg_ref[...])
        up = _dot(xn, wu_ref[...])
        acc_scr[...] += _dot((gt * _sigmoid(gt) * up).astype(BF16), wd_ref[...])

    @pl.when(jnp.logical_and(valid, last))
    def _():
        _to_slabs(o_ref, acc_scr[...], tm)

    @pl.when(jnp.logical_and(jnp.logical_not(valid), last))
    def _():
        o_ref[...] = jnp.zeros_like(o_ref)


def _group_ffn(xs, tile_expert, wg, wu, wd, *, tf):
    n_rows = xs.shape[0]
    tm = GROUP_TM
    nt = n_rows // tm
    _, d, ff = wg.shape
    x2 = xs.reshape(n_rows * SLAB, LANES)
    rows = pl.BlockSpec((tm * SLAB, LANES), lambda t, f, te: (t, 0))
    out = pl.pallas_call(
        functools.partial(_group_ffn_kernel, tm=tm, nt=nt),
        out_shape=jax.ShapeDtypeStruct((n_rows * SLAB, LANES), F32),
        grid_spec=pltpu.PrefetchScalarGridSpec(
            num_scalar_prefetch=1,
            grid=(nt, ff // tf),
            in_specs=[rows,
                      pl.BlockSpec((None, d, tf), lambda t, f, te: (te[t], 0, f)),
                      pl.BlockSpec((None, d, tf), lambda t, f, te: (te[t], 0, f)),
                      pl.BlockSpec((None, tf, d), lambda t, f, te: (te[t], f, 0))],
            out_specs=rows,
            scratch_shapes=[pltpu.VMEM((tm, d), BF16), pltpu.VMEM((tm, d), F32)]),
        compiler_params=_cp(("arbitrary", "arbitrary"), VMEM_LIMIT),
        name="moe_group_ffn",
    )(tile_expert, x2, wg, wu, wd)
    return out.reshape(n_rows, SLAB, LANES)


def _combine_kernel(pos_hbm, ys_hbm, x_ref, gate_ref, w_ref, fg_ref, o_ref, idx_smem, ybuf, isem, sem,
                    *, tm, final_norm):
    step = pl.program_id(0) * pl.num_programs(1) + pl.program_id(1)
    icopy = pltpu.make_async_copy(pos_hbm.at[pl.ds(step * 2 * tm, 2 * tm)], idx_smem, isem)
    icopy.start()
    icopy.wait()

    def row_copy(i, k):
        return pltpu.make_async_copy(ys_hbm.at[idx_smem[2 * i + k]], ybuf.at[k, pl.ds(i * SLAB, SLAB), :], sem)

    def issue(i, carry):
        row_copy(i, 0).start()
        row_copy(i, 1).start()
        return carry

    def drain(i, carry):
        row_copy(i, 0).wait()
        row_copy(i, 1).wait()
        return carry

    lax.fori_loop(0, tm, issue, 0, unroll=8)
    lax.fori_loop(0, tm, drain, 0, unroll=8)
    w = w_ref[...]
    moe = w[:, 0:1] * _from_slabs(ybuf.at[0], tm) + w[:, 1:2] * _from_slabs(ybuf.at[1], tm)
    y = x_ref[...] + gate_ref[...] * moe
    if final_norm:
        ms = jnp.mean(y * y, axis=-1, keepdims=True)
        y = y * lax.rsqrt(ms + RMS_EPS) * fg_ref[...]
    o_ref[...] = y


def _combine(ys, pos, x, gate, w, final_g):
    b, l, d = x.shape
    tm = min(l, ROUTE_TM)
    nl = l // tm
    final_norm = final_g is not None
    fg = final_g if final_norm else jnp.ones((1, d), F32)
    row = lambda bi, i: (bi, i, 0)
    return pl.pallas_call(
        functools.partial(_combine_kernel, tm=tm, final_norm=final_norm),
        out_shape=jax.ShapeDtypeStruct((b, l, d), F32),
        grid=(b, nl),
        in_specs=[pl.BlockSpec(memory_space=pl.ANY),
                  pl.BlockSpec(memory_space=pl.ANY),
                  pl.BlockSpec((None, tm, d), row),
                  pl.BlockSpec((None, 1, d), lambda bi, i: (bi, 0, 0)),
                  pl.BlockSpec((tm, LANES), lambda bi, i: (bi * nl + i, 0)),
                  pl.BlockSpec((1, d), lambda bi, i: (0, 0))],
        out_specs=pl.BlockSpec((None, tm, d), row),
        scratch_shapes=[pltpu.SMEM((2 * tm,), jnp.int32), pltpu.VMEM((2, tm * SLAB, LANES), F32),
                        pltpu.SemaphoreType.DMA, pltpu.SemaphoreType.DMA],
        compiler_params=_cp(("arbitrary", "arbitrary"), VMEM_LIMIT),
        name="moe_combine",
    )(pos.reshape(-1), ys, x, gate, w, fg)


def _moe(x, g, sh, sc, gate, router_w, router_b, wg, wu, wd, final_g):
    b, l, d = x.shape
    n = b * l
    ne = router_w.shape[-1]
    rw = jnp.pad(router_w.astype(F32), ((0, 0), (0, LANES - ne)))
    rwh, rwl = _split_bf16(rw)
    rb = jnp.pad(router_b.astype(F32), (0, LANES - ne), constant_values=NEG_BIG).reshape(1, LANES)
    xs, meta, w, counts = _route(x, g, sh, sc, rwh, rwl, rb)
    counts = counts[0, :ne]
    padded = (counts + GROUP_TM - 1) // GROUP_TM * GROUP_TM
    ends = jnp.cumsum(padded)
    offs = ends - padded
    experts = jnp.arange(ne, dtype=jnp.int32)
    off_of = lambda e: jnp.sum(jnp.where(e[:, None] == experts[None, :], offs[None, :], 0), axis=1)
    pos = jnp.stack([off_of(meta[:, 0]) + meta[:, 2], off_of(meta[:, 1]) + meta[:, 3]], axis=1).astype(jnp.int32)
    n_rows = TOP_K * n + ne * GROUP_TM
    nt = n_rows // GROUP_TM
    tile_start = jnp.arange(nt, dtype=jnp.int32) * GROUP_TM
    tile_e = jnp.sum(tile_start[:, None] >= ends[None, :], axis=1)
    n_live = ends[-1] // GROUP_TM
    last_e = jnp.sum((n_live - 1) * GROUP_TM >= ends)
    tile_e = jnp.where(tile_start < ends[-1], tile_e, last_e)
    tile_expert = jnp.concatenate([tile_e, n_live[None]]).astype(jnp.int32)
    xd = _dispatch(xs, pos, n_rows)
    ys = _group_ffn(xd, tile_expert, wg, wu, wd, tf=512)
    return _combine(ys, pos, x, gate, w, final_g)


def _hyena_filters(l, fw1, fb1, freq1, fw2, fb2, freq2, fw3):
    half = l // 2
    pos = np.concatenate([np.arange(half), l - 1 - np.arange(half)]).astype(np.float32)[:, None]
    t = jnp.asarray(pos) / (l - 1)
    w = (2.0 * math.pi / l) * jnp.asarray(pos)
    bands = jnp.linspace(1e-4, HYENA_BANDS - 1, HYENA_BANDS, dtype=F32)[None, :]
    z = jnp.concatenate([t, jnp.cos(bands * w), -jnp.sin(bands * w)], axis=-1)
    h = jnp.sin(freq1.astype(F32) * (jnp.dot(z, fw1.astype(F32), precision=HI) + fb1.astype(F32)))
    h = jnp.sin(freq2.astype(F32) * (jnp.dot(h, fw2.astype(F32), precision=HI) + fb2.astype(F32)))
    h = jnp.dot(h, fw3.astype(F32), precision=HI).reshape(l, 2, HYENA_ORDER, D_HYENA)
    max_decay = math.log(HYENA_DECAY_TARGET) / HYENA_FAST_DECAY_PCT
    min_decay = math.log(HYENA_DECAY_TARGET) / HYENA_SLOW_DECAY_PCT
    deltas = jnp.abs(jnp.linspace(min_decay, max_decay, D_HYENA, dtype=F32))
    return h * jnp.exp(-t[:, :, None, None] * deltas)


def _hyena_spectra(l, mats_fwd, *filter_params):
    half = l // 2
    n2 = 2 * l
    h = _hyena_filters(l, *filter_params)
    tot = jnp.sum(jnp.abs(h), axis=0)
    hb0 = h[0, 1]
    denom = tot[0] + tot[1] - jnp.abs(hb0)
    sig = h.reshape(2, half, 2, HYENA_ORDER, D_HYENA).transpose(2, 3, 0, 1, 4)
    sig = sig.reshape(2 * HYENA_ORDER, 2, half, D_HYENA)
    fr, fi = _hy_fwd(sig, mats_fwd, None, out_dtype=F32)
    fr = fr.reshape(2, HYENA_ORDER, 2, half, D_HYENA)
    fi = fi.reshape(2, HYENA_ORDER, 2, half, D_HYENA)
    freq = 2 * np.arange(half)[None, :] + np.arange(2)[:, None]
    phi = np.pi * freq / (2.0 * l)
    cph = jnp.asarray(np.cos(phi), F32)[:, :, None]
    sph = jnp.asarray(np.sin(phi), F32)[:, :, None]
    re = fr * cph - fi * sph
    im = fr * sph + fi * cph
    dc_slot = jnp.asarray(freq == 0)[:, :, None]
    k_re = re[0] + re[1] - hb0[:, None, None, :]
    k_im = im[0] - im[1]
    k_nyq = fi[0] + fi[1] - hb0[:, None, None, :]
    wgt = jnp.where(dc_slot, 1.0, 2.0) / (n2 * denom[:, None, None, :])
    kr = k_re * wgt
    ki = jnp.where(dc_slot, 0.0, k_im * wgt)
    kd = jnp.where(dc_slot, k_nyq * wgt, kr)
    return [jnp.stack([kr[o], ki[o], kd[o]], axis=1) for o in range(HYENA_ORDER)]


def _fnet_maps(w_f, l):
    d = FNET_HEAD_DIM
    idx = np.arange(d)
    ang = 2.0 * np.pi * ((idx[:, None] * idx[None, :]) % d) / d
    scale = 1.0 / math.sqrt(d * l)
    cd = jnp.asarray(np.cos(ang) * scale, F32)
    sd = jnp.asarray(np.sin(ang) * scale, F32)
    wf = w_f.astype(F32)
    a = jnp.einsum('de,hef->hdf', cd, wf, precision=HI)
    bm = -jnp.einsum('de,hef->hdf', sd, wf, precision=HI)
    eye = jnp.eye(FNET_HEADS, dtype=F32)
    blk = lambda m: jnp.einsum('hdf,hg->hdgf', m, eye).reshape(D_FNET, D_FNET).astype(BF16)
    return blk(a), blk(bm)


def kernel(x, c, ctx, c_ctx, ada_w, ada_b, norm_mix_g, norm_ffn_g, w_in, w_out, fnet_w, hy_conv_w, hy_conv_b,
           hy_fw1, hy_fb1, hy_freq1, hy_fw2, hy_fb2, hy_freq2, hy_fw3, hy_bias, s5_a_re, s5_a_im, s5_log_dt,
           s5_b_re, s5_b_im, s5_c_re, s5_c_im, s5_d, s5_glu_w, s5_glu_b, ffn_w_gate, ffn_w_up, ffn_w_down,
           moe_router_w, moe_router_b, moe_w_gate, moe_w_up, moe_w_down, final_g):
    b, l, d = x.shape
    lc = ctx.shape[1]
    depth = ada_w.shape[0]
    assert l % LANES == 0 and lc % LANES == 0 and d % LANES == 0

    dft = {}
    for n in sorted({l, lc}):
        fwd = _hyena_dft(n)
        dft[n] = dict(hy_fwd=fwd, hy_inv={k: m.T for k, m in fwd.items()}, fn=_dft_cos_sin(n, n))

    cc = jnp.concatenate([c, c_ctx[None, :]], axis=0).astype(F32)
    rows = -(-(b + 1) // 8) * 8
    cc = jnp.pad(cc, ((0, rows - (b + 1)), (0, 0)))

    def mix(li, xs, sh, sc, grid_mode, s0f, s0b, s5m, states_only):
        n_tok = xs.shape[1]
        g_mix = norm_mix_g[li].reshape(1, d)
        if states_only:
            (ps,) = _inproj(xs, g_mix, sh, sc, w_in[li][:, S5_OFF:].astype(BF16),
                            jnp.zeros((3, LANES), F32), jnp.zeros((1, LANES), F32),
                            grid_mode=grid_mode, full=False)
            _, hf, hb = _s5(ps, s0f, s0b, s5m, s5_d[li])
            return None, hf, hb
        pf, qv, qx1, qx2, ps = _inproj(
            xs, g_mix, sh, sc, w_in[li].astype(BF16), hy_conv_w[li].astype(F32),
            hy_conv_b[li].astype(F32).reshape(1, -1), grid_mode=grid_mode, full=True)
        tabs = dft[n_tok]
        amat, bmat = _fnet_maps(fnet_w[li], n_tok)
        yf = _fnet(pf, tabs['fn'][0], tabs['fn'][1], amat, bmat)
        spectra = _hyena_spectra(n_tok, tabs['hy_fwd'], hy_fw1[li], hy_fb1[li], hy_freq1[li], hy_fw2[li],
                                 hy_fb2[li], hy_freq2[li], hy_fw3[li])
        z = qv
        for o, xg in enumerate((qx1, qx2)):
            pr, pi = _hy_fwd(z, tabs['hy_fwd'], spectra[o])
            z = _hy_inv(pr, pi, tabs['hy_inv'], xg, z, hy_bias[li][o].astype(F32))
        ys, hf, hb = _s5(ps, s0f, s0b, s5m, s5_d[li])
        return (yf, z, ys), hf, hb

    def out_proj(li, parts, xs, gate):
        wo = w_out[li].astype(BF16)
        return _outproj(parts[0], parts[1], parts[2], xs, gate,
                        wo[:D_FNET], wo[D_FNET:D_FNET + D_HYENA], wo[D_FNET + D_HYENA:],
                        s5_glu_w[li].astype(BF16), s5_glu_b[li].astype(F32).reshape(1, -1))

    def channel_mix(li, xs, sh, sc, gate, fin):
        i = li // 2
        g_ffn = norm_ffn_g[li].reshape(1, d)
        if li % 2 == 0:
            return _ffn(xs, g_ffn, sh, sc, gate, None, ffn_w_gate[i][None].astype(BF16),
                        ffn_w_up[i][None].astype(BF16), ffn_w_down[i][None].astype(BF16), fin, tf=1408)
        return _moe(xs, g_ffn, sh, sc, gate, moe_router_w[i], moe_router_b[i], moe_w_gate[i].astype(BF16),
                    moe_w_up[i].astype(BF16), moe_w_down[i].astype(BF16), fin)

    zero_state = jnp.zeros((b, N_S5_BLOCKS, 1, 2 * STATE_HALF), F32)
    ctx_s = ctx
    for li in range(depth):
        last = li == depth - 1
        mod = _ada(cc, ada_w[li].astype(F32), ada_b[li].astype(F32))
        lat = [m.reshape(b, 1, d) for m in jnp.split(mod[:b], N_MOD, axis=-1)]
        cm = [jnp.broadcast_to(m.reshape(1, 1, d), (b, 1, d)) for m in jnp.split(mod[b:b + 1], N_MOD, axis=-1)]
        s5m = _s5_matrices(s5_a_re[li], s5_a_im[li], s5_log_dt[li], s5_b_re[li], s5_b_im[li],
                           s5_c_re[li], s5_c_im[li])

        parts, hf_c, hb_c = mix(li, ctx_s, cm[0], cm[1], False, zero_state, zero_state, s5m, last)
        if not last:
            ctx_s = out_proj(li, parts, ctx_s, cm[2])
            ctx_s = channel_mix(li, ctx_s, cm[3], cm[4], cm[5], None)

        parts, _, _ = mix(li, x, lat[0], lat[1], True, hf_c, hb_c, s5m, False)
        x = out_proj(li, parts, x, lat[2])
        x = channel_mix(li, x, lat[3], lat[4], lat[5], final_g.reshape(1, d) if last else None)

    if depth == 0:
        raise ValueError("depth must be positive")
    return x
```

```python
import functools
import math

import numpy as np
import jax
import jax.numpy as jnp
from jax import lax
from jax.experimental import pallas as pl
from jax.experimental.pallas import tpu as pltpu

F32 = jnp.float32
BF16 = jnp.bfloat16
HI = lax.Precision.HIGHEST

LANES = 128
GRID_ROW = 64
FNET_HEADS, FNET_HEAD_DIM = 4, 64
D_FNET = FNET_HEADS * FNET_HEAD_DIM
D_HYENA = 384
HYENA_ORDER = 2
HYENA_EMB = 33
HYENA_BANDS = (HYENA_EMB - 1) // 2
HYENA_FAST_DECAY_PCT, HYENA_SLOW_DECAY_PCT, HYENA_DECAY_TARGET = 0.3, 1.5, 0.01
S5_GROUP, S5_GROUPS, S5_STATE = 16, 24, 64
D_S5 = S5_GROUP * S5_GROUPS
HY_OFF = D_FNET
S5_OFF = D_FNET + (HYENA_ORDER + 1) * D_HYENA
N_MOD = 6
TOP_K = 2
RMS_EPS = 1e-6
S5_CHUNK = 16
GROUPS_PER_BLOCK = LANES // S5_GROUP
N_S5_BLOCKS = D_S5 // LANES
STATE_HALF = GROUPS_PER_BLOCK * S5_STATE
NEG_BIG = float(np.finfo(np.float32).min)
VMEM_LIMIT = 56 * 1024 * 1024


def _cp(sem, vmem=None):
    return pltpu.CompilerParams(dimension_semantics=sem, vmem_limit_bytes=vmem)


def _dot(a, b):
    return jnp.dot(a, b, preferred_element_type=F32)


def _split_bf16(a):
    hi = a.astype(BF16)
    lo = (a - hi.astype(F32)).astype(BF16)
    return hi, lo


def _sigmoid(x):
    return 1.0 / (1.0 + jnp.exp(-x))


def _rms_mod(x, g, sh, sc):
    ms = jnp.mean(x * x, axis=-1, keepdims=True)
    return (x * lax.rsqrt(ms + RMS_EPS) * g) * (1.0 + sc) + sh


def _ada_kernel(c_ref, w_ref, b_ref, o_ref):
    c = c_ref[...]
    s = c * _sigmoid(c)
    s_hi, s_lo = _split_bf16(s)
    w_hi, w_lo = _split_bf16(w_ref[...])
    o_ref[...] = _dot(s_hi, w_hi) + _dot(s_hi, w_lo) + _dot(s_lo, w_hi) + b_ref[...]


def _ada(cc, w_all, li, b):
    r, d = cc.shape
    n = w_all.shape[2]
    tn = 512
    return pl.pallas_call(
        _ada_kernel,
        out_shape=jax.ShapeDtypeStruct((r, n), F32),
        grid=(n // tn,),
        in_specs=[pl.BlockSpec((r, d), lambda j: (0, 0)),
                  pl.BlockSpec((None, d, tn), lambda j: (li, 0, j)),
                  pl.BlockSpec((1, tn), lambda j: (0, j))],
        out_specs=pl.BlockSpec((r, tn), lambda j: (0, j)),
        compiler_params=_cp(("parallel",)),
        name="ada_mod",
    )(cc, w_all, b.reshape(1, n))


def _expand_kernel(a1_ref, a2_ref, b1_ref, b2_ref, o_ref, *, nh):
    a2 = a2_ref[...]
    b2 = b2_ref[...]
    for h in range(nh):
        blk = a1_ref[:, h:h + 1] * a2 + b1_ref[:, h:h + 1] * b2
        o_ref[:, h * LANES:(h + 1) * LANES] = blk.astype(o_ref.dtype)


def _expand(a1, a2, b1, b2):
    r, nh = a1.shape
    tr = min(r, 256)
    return pl.pallas_call(
        functools.partial(_expand_kernel, nh=nh),
        out_shape=jax.ShapeDtypeStruct((r, nh * LANES), BF16),
        grid=(r // tr,),
        in_specs=[pl.BlockSpec((tr, nh), lambda i: (i, 0)),
                  pl.BlockSpec((tr, LANES), lambda i: (i, 0)),
                  pl.BlockSpec((tr, nh), lambda i: (i, 0)),
                  pl.BlockSpec((tr, LANES), lambda i: (i, 0))],
        out_specs=pl.BlockSpec((tr, nh * LANES), lambda i: (i, 0)),
        compiler_params=_cp(("parallel",)),
        name="dft_expand",
    )(a1, a2, b1, b2)


def _trig_tables(n_rows, period):
    nh = n_rows // LANES
    r = np.arange(n_rows, dtype=np.int64)[:, None]
    hi = (r * (np.arange(nh, dtype=np.int64)[None, :] * LANES)) % period
    lo = (r * np.arange(LANES, dtype=np.int64)[None, :]) % period
    ang_hi = 2.0 * np.pi * hi / period
    ang_lo = 2.0 * np.pi * lo / period
    return np.cos(ang_hi), np.sin(ang_hi), np.cos(ang_lo), np.sin(ang_lo)


def _dft_cos_sin(n, period):
    c1, s1, c2, s2 = (jnp.asarray(t, F32) for t in _trig_tables(n, period))
    return _expand(c1, c2, -s1, s2), _expand(s1, c2, c1, s2)


def _hyena_dft(l):
    h = l // 2
    nh = h // LANES
    period = 4 * l
    kp = np.arange(h, dtype=np.int64)[:, None]
    f = lambda t: jnp.asarray(t, F32)
    out = {}
    for name, k in (("e", 2 * kp), ("o", 2 * kp + 1)):
        qa = (k * (2 * LANES) * np.arange(nh, dtype=np.int64)[None, :]) % period
        qb = (k * (2 * np.arange(LANES, dtype=np.int64)[None, :] + 1)) % period
        aa = 2.0 * np.pi * qa / period
        ab = 2.0 * np.pi * qb / period
        c1, s1, c2, s2 = np.cos(aa), np.sin(aa), np.cos(ab), np.sin(ab)
        a1, a2, b1 = -s1, c2.copy(), -c1
        if name == "e":
            a1[0, :] = 1.0
            a2[0, :] = (-1.0) ** np.arange(LANES)
            b1[0, :] = 0.0
        out["c" + name] = _expand(f(c1), f(c2), f(-s1), f(s2))
        out["s" + name] = _expand(f(a1), f(a2), f(b1), f(s2))
    return out


def _flip_matrix(n):
    return jnp.asarray(np.eye(n, dtype=np.float32)[::-1], BF16)


def _flip_rows(v, j):
    sb = j.shape[0]
    nblk = v.shape[0] // sb
    v_hi, v_lo = _split_bf16(v)
    parts = []
    for s in range(nblk):
        r = slice((nblk - 1 - s) * sb, (nblk - s) * sb)
        parts.append(_dot(j, v_hi[r]) + _dot(j, v_lo[r]))
    return parts[0] if nblk == 1 else jnp.concatenate(parts, axis=0)


def _inproj_kernel(x_ref, g_ref, sh_ref, sc_ref, w_ref, cw_ref, cb_ref, j_ref, *out_refs, tm, nt, grid_mode, full):
    h = _rms_mod(x_ref[...], g_ref[...], sh_ref[...], sc_ref[...])
    acc = _dot(h.astype(BF16), w_ref[...])
    if not full:
        out_refs[0][...] = acc
        return
    pf_ref, qv_ref, qx1_ref, qx2_ref, ps_ref = out_refs
    pf_ref[...] = acc[:, :HY_OFF]
    ps_ref[...] = acc[:, S5_OFF:]
    hy = acc[:, HY_OFF:S5_OFF]
    row = lax.broadcasted_iota(jnp.int32, (tm, 1), 0)
    if grid_mode:
        first = (row % GRID_ROW) == 0
        last = (row % GRID_ROW) == GRID_ROW - 1
    else:
        first = row == 0
        last = row == tm - 1
    prev = jnp.where(first, 0.0, pltpu.roll(hy, 1, 0))
    nxt = jnp.where(last, 0.0, pltpu.roll(hy, tm - 1, 0))
    q = prev * cw_ref[0:1, :] + hy * cw_ref[1:2, :] + nxt * cw_ref[2:3, :] + cb_ref[...]
    folded = (qv_ref, qx1_ref, qx2_ref)
    cols = [slice(n * D_HYENA, (n + 1) * D_HYENA) for n in range(len(folded))]
    if nt == 1:
        half = tm // 2
        q_hi = _flip_rows(q[half:], j_ref[...])
        for r, c in zip(folded, cols):
            r[0] = q[:half, c]
            r[1] = q_hi[:, c]
        return
    i = pl.program_id(1)

    @pl.when(i < nt // 2)
    def _():
        for r, c in zip(folded, cols):
            r[...] = q[:, c]

    @pl.when(i >= nt // 2)
    def _():
        qf = _flip_rows(q, j_ref[...])
        for r, c in zip(folded, cols):
            r[...] = qf[:, c]


def _fold_spec(tm, nt, width):
    if nt == 1:
        return pl.BlockSpec((None, 2, tm // 2, width), lambda bi, i: (bi, 0, 0, 0))
    hn = nt // 2
    return pl.BlockSpec((None, None, tm, width),
                        lambda bi, i: (bi, i // hn, jnp.where(i < hn, i, nt - 1 - i), 0))


def _inproj(x, g, sh, sc, w, cw, cb, *, grid_mode, full):
    b, l, d = x.shape
    n = w.shape[1]
    tm = min(l, 512)
    nt = l // tm
    if not grid_mode:
        assert tm == l
    else:
        assert tm % GRID_ROW == 0
    assert nt == 1 or nt % 2 == 0
    row = lambda bi, i: (bi, i, 0)
    vec = lambda bi, i: (bi, 0, 0)
    const = lambda bi, i: (0, 0)
    flip = _flip_matrix(min(256, tm // 2 if nt == 1 else tm))
    plain = lambda wd: (jax.ShapeDtypeStruct((b, l, wd), F32), pl.BlockSpec((None, tm, wd), row))
    fold = lambda wd: (jax.ShapeDtypeStruct((b, 2, l // 2, wd), F32), _fold_spec(tm, nt, wd))
    outs = ([plain(D_FNET), fold(D_HYENA), fold(D_HYENA), fold(D_HYENA), plain(D_S5)] if full else [plain(n)])
    return pl.pallas_call(
        functools.partial(_inproj_kernel, tm=tm, nt=nt, grid_mode=grid_mode, full=full),
        out_shape=tuple(o[0] for o in outs),
        grid=(b, nt),
        in_specs=[pl.BlockSpec((None, tm, d), row),
                  pl.BlockSpec((1, d), const),
                  pl.BlockSpec((None, 1, d), vec),
                  pl.BlockSpec((None, 1, d), vec),
                  pl.BlockSpec((d, n), const),
                  pl.BlockSpec(cw.shape, const),
                  pl.BlockSpec(cb.shape, const),
                  pl.BlockSpec(flip.shape, const)],
        out_specs=tuple(o[1] for o in outs),
        compiler_params=_cp(("parallel", "parallel"), VMEM_LIMIT),
        name="inproj",
    )(x, g, sh, sc, w, cw, cb, flip)


def _fnet_kernel(x_ref, mc_ref, ms_ref, a_ref, b_ref, o_ref, accc, accs, *, nb):
    k = pl.program_id(2)

    @pl.when(k == 0)
    def _():
        accc[...] = jnp.zeros_like(accc)
        accs[...] = jnp.zeros_like(accs)

    mc = mc_ref[...]
    ms = ms_ref[...]
    for bi in range(nb):
        xb = x_ref[bi].astype(BF16)
        accc[bi] += _dot(mc, xb)
        accs[bi] += _dot(ms, xb)

    @pl.when(k == pl.num_programs(2) - 1)
    def _():
        for bi in range(nb):
            o_ref[bi] = (_dot(accc[bi].astype(BF16), a_ref[...])
                         + _dot(accs[bi].astype(BF16), b_ref[...]))


def _fnet(pf, mc, ms, amat, bmat):
    b, l, c = pf.shape
    nb = min(b, 2)
    t = min(l, 1024)
    return pl.pallas_call(
        functools.partial(_fnet_kernel, nb=nb),
        out_shape=jax.ShapeDtypeStruct((b, l, c), F32),
        grid=(b // nb, l // t, l // t),
        in_specs=[pl.BlockSpec((nb, t, c), lambda bb, i, k: (bb, k, 0)),
                  pl.BlockSpec((t, t), lambda bb, i, k: (i, k)),
                  pl.BlockSpec((t, t), lambda bb, i, k: (i, k)),
                  pl.BlockSpec((c, c), lambda bb, i, k: (0, 0)),
                  pl.BlockSpec((c, c), lambda bb, i, k: (0, 0))],
        out_specs=pl.BlockSpec((nb, t, c), lambda bb, i, k: (bb, i, 0)),
        scratch_shapes=[pltpu.VMEM((nb, t, c), F32), pltpu.VMEM((nb, t, c), F32)],
        compiler_params=_cp(("parallel", "parallel", "arbitrary"), VMEM_LIMIT),
        name="fnet_dft",
    )(pf, mc, ms, amat, bmat)


def _hy_fwd_kernel(z_ref, ce_ref, se_ref, co_ref, so_ref, *rest, nb, filtered):
    if filtered:
        kf_ref, pr_ref, pi_ref, acc = rest
    else:
        pr_ref, pi_ref, acc = rest
    k = pl.program_id(2)

    @pl.when(k == 0)
    def _():
        acc[...] = jnp.zeros_like(acc)

    c = z_ref.shape[-1]
    ev = jnp.concatenate([(z_ref[bi, 0] + z_ref[bi, 1]).astype(BF16) for bi in range(nb)], axis=1)
    od = jnp.concatenate([(z_ref[bi, 0] - z_ref[bi, 1]).astype(BF16) for bi in range(nb)], axis=1)
    acc[0] += _dot(ce_ref[...], ev)
    acc[1] += _dot(se_ref[...], od)
    acc[2] += _dot(co_ref[...], od)
    acc[3] += _dot(so_ref[...], ev)

    @pl.when(k == pl.num_programs(2) - 1)
    def _():
        for bi in range(nb):
            cols = slice(bi * c, (bi + 1) * c)
            for p in range(2):
                zr, zi = acc[2 * p, :, cols], acc[2 * p + 1, :, cols]
                if filtered:
                    kr, ki, kd = kf_ref[p, 0], kf_ref[p, 1], kf_ref[p, 2]
                    zr, zi = zr * kr - zi * ki, zr * ki + zi * kd
                pr_ref[bi, p] = zr.astype(pr_ref.dtype)
                pi_ref[bi, p] = zi.astype(pi_ref.dtype)


def _hy_tiles(h):
    return min(h, 1024), min(h, 512)


def _once(block_shape, index_map):
    return pl.BlockSpec(block_shape, index_map, pipeline_mode=pl.Buffered(1))


def _hy_fwd(z, mats, kf, out_dtype=BF16):
    b, _, h, c = z.shape
    nb = min(b, 2)
    ti, tk = _hy_tiles(h)
    mspec = pl.BlockSpec((ti, tk), lambda i, bb, k: (i, k))
    ospec = pl.BlockSpec((nb, 2, ti, c), lambda i, bb, k: (bb, 0, i, 0))
    filtered = kf is not None
    extra_specs = [_once((2, 3, ti, c), lambda i, bb, k: (0, 0, i, 0))] if filtered else []
    extra_args = (kf,) if filtered else ()
    return pl.pallas_call(
        functools.partial(_hy_fwd_kernel, nb=nb, filtered=filtered),
        out_shape=(jax.ShapeDtypeStruct((b, 2, h, c), out_dtype), jax.ShapeDtypeStruct((b, 2, h, c), out_dtype)),
        grid=(h // ti, b // nb, h // tk),
        in_specs=[pl.BlockSpec((nb, 2, tk, c), lambda i, bb, k: (bb, 0, k, 0)),
                  mspec, mspec, mspec, mspec] + extra_specs,
        out_specs=(ospec, ospec),
        scratch_shapes=[pltpu.VMEM((4, ti, nb * c), F32)],
        compiler_params=_cp(("parallel", "parallel", "arbitrary"), VMEM_LIMIT),
        name="hyena_fwd_dft",
    )(z, mats["ce"], mats["se"], mats["co"], mats["so"], *extra_args)


def _hy_inv_kernel(pr_ref, pi_ref, ce_ref, se_ref, co_ref, so_ref, xg_ref, zin_ref, bias_ref, o_ref, acc, *, nb):
    k = pl.program_id(2)

    @pl.when(k == 0)
    def _():
        acc[...] = jnp.zeros_like(acc)

    c = xg_ref.shape[-1]
    side = lambda ref, p: jnp.concatenate([ref[bi, p] for bi in range(nb)], axis=1)
    acc[0] += _dot(ce_ref[...], side(pr_ref, 0)) + _dot(so_ref[...], side(pi_ref, 1))
    acc[1] += _dot(se_ref[...], side(pi_ref, 0)) + _dot(co_ref[...], side(pr_ref, 1))

    @pl.when(k == pl.num_programs(2) - 1)
    def _():
        bias = bias_ref[...]
        for bi in range(nb):
            cols = slice(bi * c, (bi + 1) * c)
            u, v = acc[0, :, cols], acc[1, :, cols]
            o_ref[bi, 0] = xg_ref[bi, 0] * (u + v + zin_ref[bi, 0] * bias)
            o_ref[bi, 1] = xg_ref[bi, 1] * (u - v + zin_ref[bi, 1] * bias)


def _hy_inv(pr, pi, mats_t, xg, zin, bias):
    b, _, h, c = xg.shape
    nb = min(b, 2)
    ti, tk = _hy_tiles(h)
    pspec = pl.BlockSpec((nb, 2, tk, c), lambda bb, i, k: (bb, 0, k, 0))
    mspec = pl.BlockSpec((ti, tk), lambda bb, i, k: (i, k))
    ospec = pl.BlockSpec((nb, 2, ti, c), lambda bb, i, k: (bb, 0, i, 0))
    return pl.pallas_call(
        functools.partial(_hy_inv_kernel, nb=nb),
        out_shape=jax.ShapeDtypeStruct((b, 2, h, c), F32),
        grid=(b // nb, h // ti, h // tk),
        in_specs=[pspec, pspec, mspec, mspec, mspec, mspec,
                  _once((nb, 2, ti, c), lambda bb, i, k: (bb, 0, i, 0)),
                  _once((nb, 2, ti, c), lambda bb, i, k: (bb, 0, i, 0)),
                  pl.BlockSpec((1, c), lambda bb, i, k: (0, 0))],
        out_specs=ospec,
        scratch_shapes=[pltpu.VMEM((2, ti, nb * c), F32)],
        compiler_params=_cp(("parallel", "parallel", "arbitrary"), VMEM_LIMIT),
        name="hyena_inv_dft",
    )(pr, pi, mats_t["ce"], mats_t["se"], mats_t["co"], mats_t["so"], xg, zin, bias.reshape(1, c))


def _cmul(a, h):
    ar, ai = a[:, :STATE_HALF], a[:, STATE_HALF:]
    hr, hi = h[:, :STATE_HALF], h[:, STATE_HALF:]
    return jnp.concatenate([ar * hr - ai * hi, ar * hi + ai * hr], axis=1)


def _s5_kernel(u_ref, s0f_ref, s0b_ref, mi_ref, mof_ref, mob_ref, mnf_ref, mnb_ref, atf_ref, atb_ref,
               d_ref, y_ref, hf_ref, hb_ref, x_scr, sf_scr, sb_scr, hinf_scr, hinb_scr, *, nc, t_chunk):
    for s in range(t_chunk):
        x_scr[:, s * LANES:(s + 1) * LANES] = u_ref[pl.ds(s, nc, stride=t_chunk), :].astype(BF16)
    x = x_scr[...]
    sf_scr[...] = _dot(x, mof_ref[...])
    sb_scr[...] = _dot(x, mob_ref[...])
    atf = atf_ref[...]
    atb = atb_ref[...]

    def body(c, carry):
        hf, hb = carry
        cb = nc - 1 - c
        hinf_scr[pl.ds(c, 1), :] = hf
        hinb_scr[pl.ds(cb, 1), :] = hb
        hf = _cmul(atf, hf) + sf_scr[pl.ds(c, 1), :]
        hb = _cmul(atb, hb) + sb_scr[pl.ds(cb, 1), :]
        return hf, hb

    hf, hb = lax.fori_loop(0, nc, body, (s0f_ref[...], s0b_ref[...]), unroll=4)
    hf_ref[...] = hf
    hb_ref[...] = hb
    y = (_dot(x, mi_ref[...])
         + _dot(hinf_scr[...].astype(BF16), mnf_ref[...])
         + _dot(hinb_scr[...].astype(BF16), mnb_ref[...]))
    d = d_ref[...]
    for t in range(t_chunk):
        rows = pl.ds(t, nc, stride=t_chunk)
        y_ref[rows, :] = y[:, t * LANES:(t + 1) * LANES] + d * u_ref[rows, :]


def _s5(u, s0f, s0b, mats, d_skip):
    b, l, _ = u.shape
    tc = S5_CHUNK
    nc = l // tc
    kin = tc * LANES
    ns = 2 * STATE_HALF
    mi, mof, mob, mnf, mnb, atf, atb = mats
    st_spec = pl.BlockSpec((None, None, 1, ns), lambda j, bi: (bi, j, 0, 0))
    w3 = lambda s1, s2: pl.BlockSpec((None, s1, s2), lambda j, bi: (j, 0, 0), pipeline_mode=pl.Buffered(1))
    u_spec = pl.BlockSpec((None, l, LANES), lambda j, bi: (bi, 0, j))
    return pl.pallas_call(
        functools.partial(_s5_kernel, nc=nc, t_chunk=tc),
        out_shape=(jax.ShapeDtypeStruct((b, l, D_S5), F32),
                   jax.ShapeDtypeStruct((b, N_S5_BLOCKS, 1, ns), F32),
                   jax.ShapeDtypeStruct((b, N_S5_BLOCKS, 1, ns), F32)),
        grid=(N_S5_BLOCKS, b),
        in_specs=[u_spec, st_spec, st_spec,
                  w3(kin, kin), w3(kin, ns), w3(kin, ns), w3(ns, kin), w3(ns, kin),
                  w3(1, ns), w3(1, ns),
                  pl.BlockSpec((1, LANES), lambda j, bi: (0, j))],
        out_specs=(u_spec, st_spec, st_spec),
        scratch_shapes=[pltpu.VMEM((nc, kin), BF16),
                        pltpu.VMEM((nc, ns), F32), pltpu.VMEM((nc, ns), F32),
                        pltpu.VMEM((nc, ns), F32), pltpu.VMEM((nc, ns), F32)],
        compiler_params=_cp(("parallel", "parallel"), VMEM_LIMIT),
        name="s5_scan",
    )(u, s0f, s0b, mi, mof, mob, mnf, mnb, atf, atb, d_skip.reshape(1, D_S5))


def _s5_matrices(a_re, a_im, log_dt, b_re, b_im, c_re, c_im):
    tc = S5_CHUNK
    g, p, hd = S5_GROUPS, S5_STATE, S5_GROUP
    lam = lax.complex(a_re.astype(F32), a_im.astype(F32))
    dt = jnp.exp(log_dt.astype(F32))[..., None]
    a_bar = jnp.exp(lam * dt)
    b_bar = ((a_bar - 1.0) / lam)[..., None] * lax.complex(b_re.astype(F32), b_im.astype(F32))
    cc = lax.complex(c_re.astype(F32), c_im.astype(F32))
    ks = jnp.arange(tc + 1, dtype=F32)
    pw = jnp.exp((lam * dt)[:, :, None, :] * ks[None, None, :, None])
    kern = jnp.real(jnp.einsum('dgip,dgkp,dgpj->dgkij', cc, pw[:, :, :tc], b_bar, precision=HI))
    s_idx = jnp.arange(tc)[:, None]
    t_idx = jnp.arange(tc)[None, :]
    lag = t_idx - s_idx
    kf = jnp.where((lag >= 0)[None, :, :, None, None], kern[0][:, jnp.clip(lag, 0, tc - 1)], 0.0)
    kb = jnp.where((lag <= 0)[None, :, :, None, None], kern[1][:, jnp.clip(-lag, 0, tc - 1)], 0.0)
    ktot = kf + kb
    def place(slab, inner, row_group_width):
        width = slab.shape[-1]
        src = np.arange(width)
        e = np.zeros((width, GROUPS_PER_BLOCK * width), np.float32)
        for hh in range(GROUPS_PER_BLOCK):
            e[src, (src // inner) * GROUPS_PER_BLOCK * inner + hh * inner + src % inner] = 1.0
        wide = jnp.einsum('Jrk,kc->Jrc', slab, jnp.asarray(e), precision=HI)
        rows = lax.broadcasted_iota(jnp.int32, wide.shape[1:], 0)
        cols = lax.broadcasted_iota(jnp.int32, wide.shape[1:], 1)
        keep = (rows // row_group_width) % GROUPS_PER_BLOCK == (cols // inner) % GROUPS_PER_BLOCK
        return jnp.where(keep[None], wide, 0.0)

    kt = ktot.reshape(N_S5_BLOCKS, GROUPS_PER_BLOCK, tc, tc, hd, hd)
    kt = jnp.transpose(kt, (0, 2, 1, 5, 3, 4))
    mi = place(kt.reshape(N_S5_BLOCKS, tc * LANES, tc * hd), hd, hd)

    def parts(z):
        return jnp.stack([jnp.real(z), jnp.imag(z)], axis=0)

    def out_mat(d, powers):
        z = powers[:, :, :, None] * b_bar[d][:, None, :, :]
        z = parts(z).reshape(2, N_S5_BLOCKS, GROUPS_PER_BLOCK, tc, p, hd)
        z = jnp.transpose(z, (1, 3, 2, 5, 0, 4))
        return place(z.reshape(N_S5_BLOCKS, tc * LANES, 2 * p), p, hd)

    def in_mat(d, powers):
        z = cc[d][:, None, :, :] * powers[:, :, None, :]
        z = jnp.stack([jnp.real(z), -jnp.imag(z)], axis=0)
        z = z.reshape(2, N_S5_BLOCKS, GROUPS_PER_BLOCK, tc, hd, p)
        z = jnp.transpose(z, (1, 0, 2, 5, 3, 4))
        return place(z.reshape(N_S5_BLOCKS, 2 * STATE_HALF, tc * hd), hd, p)

    rev = jnp.arange(tc - 1, -1, -1)
    mof = out_mat(0, pw[0][:, rev])
    mob = out_mat(1, pw[1][:, :tc])
    mnf = in_mat(0, pw[0][:, 1:tc + 1])
    mnb = in_mat(1, pw[1][:, tc - jnp.arange(tc)])

    def step(d):
        z = parts(pw[d][:, tc]).reshape(2, N_S5_BLOCKS, 1, STATE_HALF)
        return jnp.transpose(z, (1, 2, 0, 3)).reshape(N_S5_BLOCKS, 1, 2 * STATE_HALF)

    bf = lambda m: m.astype(BF16)
    return bf(mi), bf(mof), bf(mob), bf(mnf), bf(mnb), step(0), step(1)


def _outproj_kernel(yf_ref, yh_ref, ys_ref, x_ref, gate_ref, w1_ref, w2_ref, w3_ref, gw_ref, gb_ref, j_ref, o_ref,
                    *, nt):
    y = ys_ref[...]
    y = 0.5 * y * (1.0 + jnp.tanh(math.sqrt(2.0 / math.pi) * (y + 0.044715 * (y * y * y))))
    y = y * _sigmoid(_dot(y.astype(BF16), gw_ref[...]) + gb_ref[...])
    acc = _dot(yf_ref[...].astype(BF16), w1_ref[...]) + _dot(y.astype(BF16), w3_ref[...])

    def finish(yh):
        o_ref[...] = x_ref[...] + gate_ref[...] * (acc + _dot(yh.astype(BF16), w2_ref[...]))

    if nt == 1:
        finish(jnp.concatenate([yh_ref[0], _flip_rows(yh_ref[1], j_ref[...])], axis=0))
        return
    i = pl.program_id(1)

    @pl.when(i < nt // 2)
    def _():
        finish(yh_ref[...])

    @pl.when(i >= nt // 2)
    def _():
        finish(_flip_rows(yh_ref[...], j_ref[...]))


def _outproj(yf, yh, ys, x, gate, w1, w2, w3, gw, gb):
    b, l, d = x.shape
    tm = min(l, 512)
    nt = l // tm
    row = lambda bi, i: (bi, i, 0)
    const = lambda bi, i: (0, 0)
    full = lambda a: pl.BlockSpec(a.shape, const)
    flip = _flip_matrix(min(256, tm // 2 if nt == 1 else tm))
    return pl.pallas_call(
        functools.partial(_outproj_kernel, nt=nt),
        out_shape=jax.ShapeDtypeStruct((b, l, d), F32),
        grid=(b, nt),
        in_specs=[pl.BlockSpec((None, tm, yf.shape[2]), row),
                  _fold_spec(tm, nt, yh.shape[3]),
                  pl.BlockSpec((None, tm, ys.shape[2]), row),
                  pl.BlockSpec((None, tm, d), row),
                  pl.BlockSpec((None, 1, d), lambda bi, i: (bi, 0, 0)),
                  full(w1), full(w2), full(w3), full(gw), full(gb), full(flip)],
        out_specs=pl.BlockSpec((None, tm, d), row),
        compiler_params=_cp(("parallel", "parallel"), VMEM_LIMIT),
        name="outproj",
    )(yf, yh, ys, x, gate, w1, w2, w3, gw, gb, flip)


def _ffn_kernel(x_ref, g_ref, sh_ref, sc_ref, gate_ref, rwh_ref, rwl_ref, rb_ref, wg_ref, wu_ref, wd_ref,
                fg_ref, o_ref, xn_scr, acc_scr, comb_scr, *, moe, final_norm):
    e = pl.program_id(2)
    f = pl.program_id(3)
    first = jnp.logical_and(e == 0, f == 0)
    last = jnp.logical_and(e == pl.num_programs(2) - 1, f == pl.num_programs(3) - 1)

    @pl.when(first)
    def _():
        h = _rms_mod(x_ref[...], g_ref[...], sh_ref[...], sc_ref[...])
        xn_scr[...] = h.astype(BF16)
        acc_scr[...] = jnp.zeros_like(acc_scr)
        if moe:
            h_hi, h_lo = _split_bf16(h)
            rwh = rwh_ref[...]
            lg = _dot(h_hi, rwh) + _dot(h_hi, rwl_ref[...]) + _dot(h_lo, rwh) + rb_ref[...]
            lane = lax.broadcasted_iota(jnp.int32, lg.shape, 1)
            m1 = jnp.max(lg, axis=-1, keepdims=True)
            i1 = jnp.min(jnp.where(lg == m1, lane, LANES), axis=-1, keepdims=True)
            lg2 = jnp.where(lane == i1, NEG_BIG, lg)
            m2 = jnp.max(lg2, axis=-1, keepdims=True)
            i2 = jnp.min(jnp.where(lg2 == m2, lane, LANES), axis=-1, keepdims=True)
            ex = jnp.exp(m2 - m1)
            den = 1.0 + ex
            comb_scr[...] = (jnp.where(lane == i1, 1.0 / den, 0.0)
                             + jnp.where(lane == i2, ex / den, 0.0))

    xn = xn_scr[...]
    gt = _dot(xn, wg_ref[...])
    up = _dot(xn, wu_ref[...])
    hmid = gt * _sigmoid(gt) * up
    if moe:
        comb = comb_scr[...]
        lane = lax.broadcasted_iota(jnp.int32, comb.shape, 1)
        hmid = hmid * jnp.sum(jnp.where(lane == e, comb, 0.0), axis=-1, keepdims=True)
    acc_scr[...] += _dot(hmid.astype(BF16), wd_ref[...])

    @pl.when(last)
    def _():
        y = x_ref[...] + gate_ref[...] * acc_scr[...]
        if final_norm:
            ms = jnp.mean(y * y, axis=-1, keepdims=True)
            y = y * lax.rsqrt(ms + RMS_EPS) * fg_ref[...]
        o_ref[...] = y


def _ffn(x, g, sh, sc, gate, router, wg, wu, wd, final_g, *, tf):
    b, l, d = x.shape
    ne, _, ff = wg.shape
    moe = router is not None
    final_norm = final_g is not None
    tm = min(l, 1024)
    if moe:
        rwh, rwl, rb = router
    else:
        rwh = rwl = jnp.zeros((d, LANES), BF16)
        rb = jnp.zeros((1, LANES), F32)
    fg = final_g if final_norm else jnp.ones((1, d), F32)
    row = lambda bi, i, e, f: (bi, i, 0)
    vec = lambda bi, i, e, f: (bi, 0, 0)
    const = lambda bi, i, e, f: (0, 0)
    return pl.pallas_call(
        functools.partial(_ffn_kernel, moe=moe, final_norm=final_norm),
        out_shape=jax.ShapeDtypeStruct((b, l, d), F32),
        grid=(b, l // tm, ne, ff // tf),
        in_specs=[pl.BlockSpec((None, tm, d), row),
                  pl.BlockSpec((1, d), const),
                  pl.BlockSpec((None, 1, d), vec),
                  pl.BlockSpec((None, 1, d), vec),
                  pl.BlockSpec((None, 1, d), vec),
                  pl.BlockSpec((d, LANES), const),
                  pl.BlockSpec((d, LANES), const),
                  pl.BlockSpec((1, LANES), const),
                  pl.BlockSpec((None, d, tf), lambda bi, i, e, f: (e, 0, f)),
                  pl.BlockSpec((None, d, tf), lambda bi, i, e, f: (e, 0, f)),
                  pl.BlockSpec((None, tf, d), lambda bi, i, e, f: (e, f, 0)),
                  pl.BlockSpec((1, d), const)],
        out_specs=pl.BlockSpec((None, tm, d), row),
        scratch_shapes=[pltpu.VMEM((tm, d), BF16), pltpu.VMEM((tm, d), F32), pltpu.VMEM((tm, LANES), F32)],
        compiler_params=_cp(("parallel", "parallel", "arbitrary", "arbitrary"), VMEM_LIMIT),
        name="moe_ffn" if moe else "dense_ffn",
    )(x, g, sh, sc, gate, rwh, rwl, rb, wg, wu, wd, fg)


SLAB = 8
ROUTE_TM = 512
GROUP_TM = 1024


def _to_slabs(ref, val, n):
    for s in range(SLAB):
        ref[pl.ds(s, n, stride=SLAB), :] = val[:, s * LANES:(s + 1) * LANES]


def _from_slabs(ref, n):
    return jnp.concatenate([ref[pl.ds(s, n, stride=SLAB), :] for s in range(SLAB)], axis=1)


def _route_kernel(x_ref, g_ref, sh_ref, sc_ref, rwh_ref, rwl_ref, rb_ref, tri_ref,
                  xs_ref, meta_ref, gate_ref, cnt_ref, carry_scr, *, tm):
    step = pl.program_id(0) * pl.num_programs(1) + pl.program_id(1)

    @pl.when(step == 0)
    def _():
        carry_scr[...] = jnp.zeros_like(carry_scr)

    h = _rms_mod(x_ref[...], g_ref[...], sh_ref[...], sc_ref[...])
    _to_slabs(xs_ref, h, tm)
    h_hi, h_lo = _split_bf16(h)
    rwh = rwh_ref[...]
    lg = _dot(h_hi, rwh) + _dot(h_hi, rwl_ref[...]) + _dot(h_lo, rwh) + rb_ref[...]
    lane = lax.broadcasted_iota(jnp.int32, lg.shape, 1)
    m1 = jnp.max(lg, axis=-1, keepdims=True)
    i1 = jnp.min(jnp.where(lg == m1, lane, LANES), axis=-1, keepdims=True)
    lg2 = jnp.where(lane == i1, NEG_BIG, lg)
    m2 = jnp.max(lg2, axis=-1, keepdims=True)
    i2 = jnp.min(jnp.where(lg2 == m2, lane, LANES), axis=-1, keepdims=True)
    ex = jnp.exp(m2 - m1)
    den = 1.0 + ex
    gate_ref[...] = jnp.where(lane == 0, 1.0 / den, jnp.where(lane == 1, ex / den, 0.0))
    sel1 = lane == i1
    sel2 = lane == i2
    tri = tri_ref[...]
    p1 = _dot(tri, sel1.astype(BF16))
    p2 = _dot(tri, sel2.astype(BF16))
    tot1 = p1[tm - 1:tm, :]
    tot2 = p2[tm - 1:tm, :]
    carry = carry_scr[...]
    r1 = jnp.sum(jnp.where(sel1, carry + p1 - 1.0, 0.0), axis=-1, keepdims=True)
    r2 = jnp.sum(jnp.where(sel2, carry + tot1 + p2 - 1.0, 0.0), axis=-1, keepdims=True)
    carry = carry + tot1 + tot2
    carry_scr[...] = carry
    cnt_ref[...] = carry.astype(jnp.int32)
    meta_ref[...] = jnp.where(lane == 0, i1, jnp.where(lane == 1, i2, jnp.where(
        lane == 2, r1.astype(jnp.int32), jnp.where(lane == 3, r2.astype(jnp.int32), 0))))


def _route(x, g, sh, sc, rwh, rwl, rb):
    b, l, d = x.shape
    tm = min(l, ROUTE_TM)
    n = b * l
    nl = l // tm
    tri = jnp.tril(jnp.ones((tm, tm), F32)).astype(BF16)
    row = lambda bi, i: (bi, i, 0)
    vec = lambda bi, i: (bi, 0, 0)
    const = lambda bi, i: (0, 0)
    flat = lambda bi, i: (bi * nl + i, 0)
    return pl.pallas_call(
        functools.partial(_route_kernel, tm=tm),
        out_shape=(jax.ShapeDtypeStruct((n * SLAB, LANES), F32),
                   jax.ShapeDtypeStruct((n, LANES), jnp.int32),
                   jax.ShapeDtypeStruct((n, LANES), F32),
                   jax.ShapeDtypeStruct((1, LANES), jnp.int32)),
        grid=(b, nl),
        in_specs=[pl.BlockSpec((None, tm, d), row),
                  pl.BlockSpec((1, d), const),
                  pl.BlockSpec((None, 1, d), vec),
                  pl.BlockSpec((None, 1, d), vec),
                  pl.BlockSpec((d, LANES), const),
                  pl.BlockSpec((d, LANES), const),
                  pl.BlockSpec((1, LANES), const),
                  pl.BlockSpec((tm, tm), const)],
        out_specs=(pl.BlockSpec((tm * SLAB, LANES), flat),
                   pl.BlockSpec((tm, LANES), flat),
                   pl.BlockSpec((tm, LANES), flat),
                   pl.BlockSpec((1, LANES), const)),
        scratch_shapes=[pltpu.VMEM((1, LANES), F32)],
        compiler_params=_cp(("arbitrary", "arbitrary"), VMEM_LIMIT),
        name="moe_route",
    )(x, g, sh, sc, rwh, rwl, rb, tri)


def _dispatch_kernel(pos_hbm, xs_ref, init_hbm, out_hbm, idx_smem, isem, sem, *, tm):
    del init_hbm
    step = pl.program_id(0)
    icopy = pltpu.make_async_copy(pos_hbm.at[pl.ds(step * 2 * tm, 2 * tm)], idx_smem, isem)
    icopy.start()
    icopy.wait()

    def row_copy(i, k):
        return pltpu.make_async_copy(xs_ref.at[pl.ds(i * SLAB, SLAB), :], out_hbm.at[idx_smem[2 * i + k]], sem)

    def issue(i, carry):
        row_copy(i, 0).start()
        row_copy(i, 1).start()
        return carry

    def drain(i, carry):
        row_copy(i, 0).wait()
        row_copy(i, 1).wait()
        return carry

    lax.fori_loop(0, tm, issue, 0, unroll=8)
    lax.fori_loop(0, tm, drain, 0, unroll=8)


def _dispatch(xs, pos, n_rows):
    n = xs.shape[0] // SLAB
    tm = min(n, ROUTE_TM)
    init = jnp.zeros((n_rows, SLAB, LANES), F32)
    return pl.pallas_call(
        functools.partial(_dispatch_kernel, tm=tm),
        out_shape=jax.ShapeDtypeStruct((n_rows, SLAB, LANES), F32),
        grid=(n // tm,),
        in_specs=[pl.BlockSpec(memory_space=pl.ANY),
                  pl.BlockSpec((tm * SLAB, LANES), lambda i: (i, 0)),
                  pl.BlockSpec(memory_space=pl.ANY)],
        out_specs=pl.BlockSpec(memory_space=pl.ANY),
        scratch_shapes=[pltpu.SMEM((2 * tm,), jnp.int32), pltpu.SemaphoreType.DMA, pltpu.SemaphoreType.DMA],
        input_output_aliases={2: 0},
        compiler_params=_cp(("arbitrary",), VMEM_LIMIT),
        name="moe_dispatch",
    )(pos.reshape(-1), xs, init)


def _group_ffn_kernel(te_ref, x_ref, wg_ref, wu_ref, wd_ref, o_ref, xn_scr, acc_scr, *, tm, nt):
    t = pl.program_id(0)
    f = pl.program_id(1)
    valid = t < te_ref[nt]
    last = f == pl.num_programs(1) - 1

    @pl.when(jnp.logical_and(valid, f == 0))
    def _():
        xn_scr[...] = _from_slabs(x_ref, tm).astype(BF16)
        acc_scr[...] = jnp.zeros_like(acc_scr)

    @pl.when(valid)
    def _():
        xn = xn_scr[...]
        gt = _dot(xn, wg_ref[...])
        up = _dot(xn, wu_ref[...])
        acc_scr[...] += _dot((gt * _sigmoid(gt) * up).astype(BF16), wd_ref[...])

    @pl.when(jnp.logical_and(valid, last))
    def _():
        _to_slabs(o_ref, acc_scr[...], tm)

    @pl.when(jnp.logical_and(jnp.logical_not(valid), last))
    def _():
        o_ref[...] = jnp.zeros_like(o_ref)


def _group_ffn(xs, tile_expert, wg, wu, wd, *, tf):
    n_rows = xs.shape[0]
    tm = GROUP_TM
    nt = n_rows // tm
    _, d, ff = wg.shape
    x2 = xs.reshape(n_rows * SLAB, LANES)
    rows = pl.BlockSpec((tm * SLAB, LANES), lambda t, f, te: (t, 0))
    out = pl.pallas_call(
        functools.partial(_group_ffn_kernel, tm=tm, nt=nt),
        out_shape=jax.ShapeDtypeStruct((n_rows * SLAB, LANES), F32),
        grid_spec=pltpu.PrefetchScalarGridSpec(
            num_scalar_prefetch=1,
            grid=(nt, ff // tf),
            in_specs=[rows,
                      pl.BlockSpec((None, d, tf), lambda t, f, te: (te[t], 0, f)),
                      pl.BlockSpec((None, d, tf), lambda t, f, te: (te[t], 0, f)),
                      pl.BlockSpec((None, tf, d), lambda t, f, te: (te[t], f, 0))],
            out_specs=rows,
            scratch_shapes=[pltpu.VMEM((tm, d), BF16), pltpu.VMEM((tm, d), F32)]),
        compiler_params=_cp(("arbitrary", "arbitrary"), VMEM_LIMIT),
        name="moe_group_ffn",
    )(tile_expert, x2, wg, wu, wd)
    return out.reshape(n_rows, SLAB, LANES)


def _combine_kernel(pos_hbm, ys_hbm, x_ref, gate_ref, w_ref, fg_ref, o_ref, idx_smem, ybuf, isem, sem,
                    *, tm, final_norm):
    step = pl.program_id(0) * pl.num_programs(1) + pl.program_id(1)
    icopy = pltpu.make_async_copy(pos_hbm.at[pl.ds(step * 2 * tm, 2 * tm)], idx_smem, isem)
    icopy.start()
    icopy.wait()

    def row_copy(i, k):
        return pltpu.make_async_copy(ys_hbm.at[idx_smem[2 * i + k]], ybuf.at[k, pl.ds(i * SLAB, SLAB), :], sem)

    def issue(i, carry):
        row_copy(i, 0).start()
        row_copy(i, 1).start()
        return carry

    def drain(i, carry):
        row_copy(i, 0).wait()
        row_copy(i, 1).wait()
        return carry

    lax.fori_loop(0, tm, issue, 0, unroll=8)
    lax.fori_loop(0, tm, drain, 0, unroll=8)
    w = w_ref[...]
    moe = w[:, 0:1] * _from_slabs(ybuf.at[0], tm) + w[:, 1:2] * _from_slabs(ybuf.at[1], tm)
    y = x_ref[...] + gate_ref[...] * moe
    if final_norm:
        ms = jnp.mean(y * y, axis=-1, keepdims=True)
        y = y * lax.rsqrt(ms + RMS_EPS) * fg_ref[...]
    o_ref[...] = y


def _combine(ys, pos, x, gate, w, final_g):
    b, l, d = x.shape
    tm = min(l, ROUTE_TM)
    nl = l // tm
    final_norm = final_g is not None
    fg = final_g if final_norm else jnp.ones((1, d), F32)
    row = lambda bi, i: (bi, i, 0)
    return pl.pallas_call(
        functools.partial(_combine_kernel, tm=tm, final_norm=final_norm),
        out_shape=jax.ShapeDtypeStruct((b, l, d), F32),
        grid=(b, nl),
        in_specs=[pl.BlockSpec(memory_space=pl.ANY),
                  pl.BlockSpec(memory_space=pl.ANY),
                  pl.BlockSpec((None, tm, d), row),
                  pl.BlockSpec((None, 1, d), lambda bi, i: (bi, 0, 0)),
                  pl.BlockSpec((tm, LANES), lambda bi, i: (bi * nl + i, 0)),
                  pl.BlockSpec((1, d), lambda bi, i: (0, 0))],
        out_specs=pl.BlockSpec((None, tm, d), row),
        scratch_shapes=[pltpu.SMEM((2 * tm,), jnp.int32), pltpu.VMEM((2, tm * SLAB, LANES), F32),
                        pltpu.SemaphoreType.DMA, pltpu.SemaphoreType.DMA],
        compiler_params=_cp(("arbitrary", "arbitrary"), VMEM_LIMIT),
        name="moe_combine",
    )(pos.reshape(-1), ys, x, gate, w, fg)


def _moe(x, g, sh, sc, gate, router_w, router_b, wg, wu, wd, final_g):
    b, l, d = x.shape
    n = b * l
    ne = router_w.shape[-1]
    rw = jnp.pad(router_w.astype(F32), ((0, 0), (0, LANES - ne)))
    rwh, rwl = _split_bf16(rw)
    rb = jnp.pad(router_b.astype(F32), (0, LANES - ne), constant_values=NEG_BIG).reshape(1, LANES)
    xs, meta, w, counts = _route(x, g, sh, sc, rwh, rwl, rb)
    counts = counts[0, :ne]
    padded = (counts + GROUP_TM - 1) // GROUP_TM * GROUP_TM
    ends = jnp.cumsum(padded)
    offs = ends - padded
    experts = jnp.arange(ne, dtype=jnp.int32)
    off_of = lambda e: jnp.sum(jnp.where(e[:, None] == experts[None, :], offs[None, :], 0), axis=1)
    pos = jnp.stack([off_of(meta[:, 0]) + meta[:, 2], off_of(meta[:, 1]) + meta[:, 3]], axis=1).astype(jnp.int32)
    n_rows = TOP_K * n + ne * GROUP_TM
    nt = n_rows // GROUP_TM
    tile_start = jnp.arange(nt, dtype=jnp.int32) * GROUP_TM
    tile_e = jnp.sum(tile_start[:, None] >= ends[None, :], axis=1)
    n_live = ends[-1] // GROUP_TM
    last_e = jnp.sum((n_live - 1) * GROUP_TM >= ends)
    tile_e = jnp.where(tile_start < ends[-1], tile_e, last_e)
    tile_expert = jnp.concatenate([tile_e, n_live[None]]).astype(jnp.int32)
    xd = _dispatch(xs, pos, n_rows)
    ys = _group_ffn(xd, tile_expert, wg, wu, wd, tf=512)
    return _combine(ys, pos, x, gate, w, final_g)


def _hyena_filters(l, fw1, fb1, freq1, fw2, fb2, freq2, fw3):
    half = l // 2
    pos = np.concatenate([np.arange(half), l - 1 - np.arange(half)]).astype(np.float32)[:, None]
    t = jnp.asarray(pos) / (l - 1)
    w = (2.0 * math.pi / l) * jnp.asarray(pos)
    bands = jnp.linspace(1e-4, HYENA_BANDS - 1, HYENA_BANDS, dtype=F32)[None, :]
    z = jnp.concatenate([t, jnp.cos(bands * w), -jnp.sin(bands * w)], axis=-1)
    h = jnp.sin(freq1.astype(F32) * (jnp.dot(z, fw1.astype(F32), precision=HI) + fb1.astype(F32)))
    h = jnp.sin(freq2.astype(F32) * (jnp.dot(h, fw2.astype(F32), precision=HI) + fb2.astype(F32)))
    h = jnp.dot(h, fw3.astype(F32), precision=HI).reshape(l, 2, HYENA_ORDER, D_HYENA)
    max_decay = math.log(HYENA_DECAY_TARGET) / HYENA_FAST_DECAY_PCT
    min_decay = math.log(HYENA_DECAY_TARGET) / HYENA_SLOW_DECAY_PCT
    deltas = jnp.abs(jnp.linspace(min_decay, max_decay, D_HYENA, dtype=F32))
    return h * jnp.exp(-t[:, :, None, None] * deltas)


def _hyena_spectra(l, mats_fwd, *filter_params):
    half = l // 2
    n2 = 2 * l
    h = _hyena_filters(l, *filter_params)
    tot = jnp.sum(jnp.abs(h), axis=0)
    hb0 = h[0, 1]
    denom = tot[0] + tot[1] - jnp.abs(hb0)
    sig = h.reshape(2, half, 2, HYENA_ORDER, D_HYENA).transpose(2, 3, 0, 1, 4)
    sig = sig.reshape(2 * HYENA_ORDER, 2, half, D_HYENA)
    fr, fi = _hy_fwd(sig, mats_fwd, None, out_dtype=F32)
    fr = fr.reshape(2, HYENA_ORDER, 2, half, D_HYENA)
    fi = fi.reshape(2, HYENA_ORDER, 2, half, D_HYENA)
    freq = 2 * np.arange(half)[None, :] + np.arange(2)[:, None]
    phi = np.pi * freq / (2.0 * l)
    cph = jnp.asarray(np.cos(phi), F32)[:, :, None]
    sph = jnp.asarray(np.sin(phi), F32)[:, :, None]
    re = fr * cph - fi * sph
    im = fr * sph + fi * cph
    dc_slot = jnp.asarray(freq == 0)[:, :, None]
    k_re = re[0] + re[1] - hb0[:, None, None, :]
    k_im = im[0] - im[1]
    k_nyq = fi[0] + fi[1] - hb0[:, None, None, :]
    wgt = jnp.where(dc_slot, 1.0, 2.0) / (n2 * denom[:, None, None, :])
    kr = k_re * wgt
    ki = jnp.where(dc_slot, 0.0, k_im * wgt)
    kd = jnp.where(dc_slot, k_nyq * wgt, kr)
    return [jnp.stack([kr[o], ki[o], kd[o]], axis=1) for o in range(HYENA_ORDER)]


def _fnet_maps(w_f, l):
    d = FNET_HEAD_DIM
    idx = np.arange(d)
    ang = 2.0 * np.pi * ((idx[:, None] * idx[None, :]) % d) / d
    scale = 1.0 / math.sqrt(d * l)
    cd = jnp.asarray(np.cos(ang) * scale, F32)
    sd = jnp.asarray(np.sin(ang) * scale, F32)
    wf = w_f.astype(F32)
    a = jnp.einsum('de,hef->hdf', cd, wf, precision=HI)
    bm = -jnp.einsum('de,hef->hdf', sd, wf, precision=HI)
    eye = jnp.eye(FNET_HEADS, dtype=F32)
    blk = lambda m: jnp.einsum('hdf,hg->hdgf', m, eye).reshape(D_FNET, D_FNET).astype(BF16)
    return blk(a), blk(bm)


def kernel(x, c, ctx, c_ctx, ada_w, ada_b, norm_mix_g, norm_ffn_g, w_in, w_out, fnet_w, hy_conv_w, hy_conv_b,
           hy_fw1, hy_fb1, hy_freq1, hy_fw2, hy_fb2, hy_freq2, hy_fw3, hy_bias, s5_a_re, s5_a_im, s5_log_dt,
           s5_b_re, s5_b_im, s5_c_re, s5_c_im, s5_d, s5_glu_w, s5_glu_b, ffn_w_gate, ffn_w_up, ffn_w_down,
           moe_router_w, moe_router_b, moe_w_gate, moe_w_up, moe_w_down, final_g):
    b, l, d = x.shape
    lc = ctx.shape[1]
    depth = ada_w.shape[0]
    assert l % LANES == 0 and lc % LANES == 0 and d % LANES == 0

    dft = {}
    for n in sorted({l, lc}):
        fwd = _hyena_dft(n)
        dft[n] = dict(hy_fwd=fwd, hy_inv={k: m.T for k, m in fwd.items()}, fn=_dft_cos_sin(n, n))

    cc = jnp.concatenate([c, c_ctx[None, :]], axis=0).astype(F32)
    rows = -(-(b + 1) // 8) * 8
    cc = jnp.pad(cc, ((0, rows - (b + 1)), (0, 0)))

    def mix(li, xs, sh, sc, grid_mode, s0f, s0b, s5m, states_only):
        n_tok = xs.shape[1]
        g_mix = norm_mix_g[li].reshape(1, d)
        if states_only:
            (ps,) = _inproj(xs, g_mix, sh, sc, w_in[li][:, S5_OFF:].astype(BF16),
                            jnp.zeros((3, LANES), F32), jnp.zeros((1, LANES), F32),
                            grid_mode=grid_mode, full=False)
            _, hf, hb = _s5(ps, s0f, s0b, s5m, s5_d[li])
            return None, hf, hb
        pf, qv, qx1, qx2, ps = _inproj(
            xs, g_mix, sh, sc, w_in[li].astype(BF16), hy_conv_w[li].astype(F32),
            hy_conv_b[li].astype(F32).reshape(1, -1), grid_mode=grid_mode, full=True)
        tabs = dft[n_tok]
        amat, bmat = _fnet_maps(fnet_w[li], n_tok)
        yf = _fnet(pf, tabs['fn'][0], tabs['fn'][1], amat, bmat)
        spectra = _hyena_spectra(n_tok, tabs['hy_fwd'], hy_fw1[li], hy_fb1[li], hy_freq1[li], hy_fw2[li],
                                 hy_fb2[li], hy_freq2[li], hy_fw3[li])
        z = qv
        for o, xg in enumerate((qx1, qx2)):
            pr, pi = _hy_fwd(z, tabs['hy_fwd'], spectra[o])
            z = _hy_inv(pr, pi, tabs['hy_inv'], xg, z, hy_bias[li][o].astype(F32))
        ys, hf, hb = _s5(ps, s0f, s0b, s5m, s5_d[li])
        return (yf, z, ys), hf, hb

    def out_proj(li, parts, xs, gate):
        wo = w_out[li].astype(BF16)
        return _outproj(parts[0], parts[1], parts[2], xs, gate,
                        wo[:D_FNET], wo[D_FNET:D_FNET + D_HYENA], wo[D_FNET + D_HYENA:],
                        s5_glu_w[li].astype(BF16), s5_glu_b[li].astype(F32).reshape(1, -1))

    def channel_mix(li, xs, sh, sc, gate, fin):
        i = li // 2
        g_ffn = norm_ffn_g[li].reshape(1, d)
        if li % 2 == 0:
            return _ffn(xs, g_ffn, sh, sc, gate, None, ffn_w_gate[i][None].astype(BF16),
                        ffn_w_up[i][None].astype(BF16), ffn_w_down[i][None].astype(BF16), fin, tf=1408)
        return _moe(xs, g_ffn, sh, sc, gate, moe_router_w[i], moe_router_b[i], moe_w_gate[i].astype(BF16),
                    moe_w_up[i].astype(BF16), moe_w_down[i].astype(BF16), fin)

    zero_state = jnp.zeros((b, N_S5_BLOCKS, 1, 2 * STATE_HALF), F32)
    ctx_s = ctx
    for li in range(depth):
        last = li == depth - 1
        mod = _ada(cc, ada_w.astype(F32), li, ada_b[li].astype(F32))
        lat = [m.reshape(b, 1, d) for m in jnp.split(mod[:b], N_MOD, axis=-1)]
        cm = [jnp.broadcast_to(m.reshape(1, 1, d), (b, 1, d)) for m in jnp.split(mod[b:b + 1], N_MOD, axis=-1)]
        s5m = _s5_matrices(s5_a_re[li], s5_a_im[li], s5_log_dt[li], s5_b_re[li], s5_b_im[li],
                           s5_c_re[li], s5_c_im[li])

        parts, hf_c, hb_c = mix(li, ctx_s, cm[0], cm[1], False, zero_state, zero_state, s5m, last)
        if not last:
            ctx_s = out_proj(li, parts, ctx_s, cm[2])
            ctx_s = channel_mix(li, ctx_s, cm[3], cm[4], cm[5], None)

        parts, _, _ = mix(li, x, lat[0], lat[1], True, hf_c, hb_c, s5m, False)
        x = out_proj(li, parts, x, lat[2])
        x = channel_mix(li, x, lat[3], lat[4], lat[5], final_g.reshape(1, d) if last else None)

    if depth == 0:
        raise ValueError("depth must be positive")
    return x
```

```python
import functools
import math

import numpy as np
import jax
import jax.numpy as jnp
from jax import lax
from jax.experimental import pallas as pl
from jax.experimental.pallas import tpu as pltpu

F32 = jnp.float32
BF16 = jnp.bfloat16
HI = lax.Precision.HIGHEST

LANES = 128
GRID_ROW = 64
FNET_HEADS, FNET_HEAD_DIM = 4, 64
D_FNET = FNET_HEADS * FNET_HEAD_DIM
D_HYENA = 384
HYENA_ORDER = 2
HYENA_EMB = 33
HYENA_BANDS = (HYENA_EMB - 1) // 2
HYENA_FAST_DECAY_PCT, HYENA_SLOW_DECAY_PCT, HYENA_DECAY_TARGET = 0.3, 1.5, 0.01
S5_GROUP, S5_GROUPS, S5_STATE = 16, 24, 64
D_S5 = S5_GROUP * S5_GROUPS
HY_OFF = D_FNET
S5_OFF = D_FNET + (HYENA_ORDER + 1) * D_HYENA
N_MOD = 6
TOP_K = 2
RMS_EPS = 1e-6
S5_CHUNK = 16
GROUPS_PER_BLOCK = LANES // S5_GROUP
N_S5_BLOCKS = D_S5 // LANES
STATE_HALF = GROUPS_PER_BLOCK * S5_STATE
NEG_BIG = float(np.finfo(np.float32).min)
VMEM_LIMIT = 56 * 1024 * 1024


def _cp(sem, vmem=None):
    return pltpu.CompilerParams(dimension_semantics=sem, vmem_limit_bytes=vmem)


def _dot(a, b):
    return jnp.dot(a, b, preferred_element_type=F32)


def _split_bf16(a):
    hi = a.astype(BF16)
    lo = (a - hi.astype(F32)).astype(BF16)
    return hi, lo


def _sigmoid(x):
    return 1.0 / (1.0 + jnp.exp(-x))


def _rms_mod(x, g, sh, sc):
    ms = jnp.mean(x * x, axis=-1, keepdims=True)
    return (x * lax.rsqrt(ms + RMS_EPS) * g) * (1.0 + sc) + sh


def _ada_kernel(c_ref, w_ref, b_ref, o_ref):
    c = c_ref[...]
    s = c * _sigmoid(c)
    s_hi, s_lo = _split_bf16(s)
    w_hi, w_lo = _split_bf16(w_ref[...])
    o_ref[...] = _dot(s_hi, w_hi) + _dot(s_hi, w_lo) + _dot(s_lo, w_hi) + b_ref[...]


def _ada(cc, w_all, li, b):
    r, d = cc.shape
    n = w_all.shape[2]
    tn = 512
    return pl.pallas_call(
        _ada_kernel,
        out_shape=jax.ShapeDtypeStruct((r, n), F32),
        grid=(n // tn,),
        in_specs=[pl.BlockSpec((r, d), lambda j: (0, 0)),
                  pl.BlockSpec((None, d, tn), lambda j: (li, 0, j)),
                  pl.BlockSpec((1, tn), lambda j: (0, j))],
        out_specs=pl.BlockSpec((r, tn), lambda j: (0, j)),
        compiler_params=_cp(("parallel",)),
        name="ada_mod",
    )(cc, w_all, b.reshape(1, n))


def _expand_kernel(a1_ref, a2_ref, b1_ref, b2_ref, o_ref, *, nh):
    a2 = a2_ref[...]
    b2 = b2_ref[...]
    for h in range(nh):
        blk = a1_ref[:, h:h + 1] * a2 + b1_ref[:, h:h + 1] * b2
        o_ref[:, h * LANES:(h + 1) * LANES] = blk.astype(o_ref.dtype)


def _expand(a1, a2, b1, b2):
    r, nh = a1.shape
    tr = min(r, 256)
    return pl.pallas_call(
        functools.partial(_expand_kernel, nh=nh),
        out_shape=jax.ShapeDtypeStruct((r, nh * LANES), BF16),
        grid=(r // tr,),
        in_specs=[pl.BlockSpec((tr, nh), lambda i: (i, 0)),
                  pl.BlockSpec((tr, LANES), lambda i: (i, 0)),
                  pl.BlockSpec((tr, nh), lambda i: (i, 0)),
                  pl.BlockSpec((tr, LANES), lambda i: (i, 0))],
        out_specs=pl.BlockSpec((tr, nh * LANES), lambda i: (i, 0)),
        compiler_params=_cp(("parallel",)),
        name="dft_expand",
    )(a1, a2, b1, b2)


def _trig_tables(n_rows, period):
    nh = n_rows // LANES
    r = np.arange(n_rows, dtype=np.int64)[:, None]
    hi = (r * (np.arange(nh, dtype=np.int64)[None, :] * LANES)) % period
    lo = (r * np.arange(LANES, dtype=np.int64)[None, :]) % period
    ang_hi = 2.0 * np.pi * hi / period
    ang_lo = 2.0 * np.pi * lo / period
    return np.cos(ang_hi), np.sin(ang_hi), np.cos(ang_lo), np.sin(ang_lo)


def _dft_cos_sin(n, period):
    c1, s1, c2, s2 = (jnp.asarray(t, F32) for t in _trig_tables(n, period))
    return _expand(c1, c2, -s1, s2), _expand(s1, c2, c1, s2)


def _hyena_dft(l):
    h = l // 2
    nh = h // LANES
    period = 4 * l
    kp = np.arange(h, dtype=np.int64)[:, None]
    f = lambda t: jnp.asarray(t, F32)
    out = {}
    for name, k in (("e", 2 * kp), ("o", 2 * kp + 1)):
        qa = (k * (2 * LANES) * np.arange(nh, dtype=np.int64)[None, :]) % period
        qb = (k * (2 * np.arange(LANES, dtype=np.int64)[None, :] + 1)) % period
        aa = 2.0 * np.pi * qa / period
        ab = 2.0 * np.pi * qb / period
        c1, s1, c2, s2 = np.cos(aa), np.sin(aa), np.cos(ab), np.sin(ab)
        a1, a2, b1 = -s1, c2.copy(), -c1
        if name == "e":
            a1[0, :] = 1.0
            a2[0, :] = (-1.0) ** np.arange(LANES)
            b1[0, :] = 0.0
        out["c" + name] = _expand(f(c1), f(c2), f(-s1), f(s2))
        out["s" + name] = _expand(f(a1), f(a2), f(b1), f(s2))
    return out


def _flip_matrix(n):
    return jnp.asarray(np.eye(n, dtype=np.float32)[::-1], BF16)


def _flip_rows(v, j):
    sb = j.shape[0]
    nblk = v.shape[0] // sb
    v_hi, v_lo = _split_bf16(v)
    parts = []
    for s in range(nblk):
        r = slice((nblk - 1 - s) * sb, (nblk - s) * sb)
        parts.append(_dot(j, v_hi[r]) + _dot(j, v_lo[r]))
    return parts[0] if nblk == 1 else jnp.concatenate(parts, axis=0)


def _inproj_kernel(x_ref, g_ref, sh_ref, sc_ref, w_ref, cw_ref, cb_ref, j_ref, *out_refs, tm, nt, grid_mode, full):
    h = _rms_mod(x_ref[...], g_ref[...], sh_ref[...], sc_ref[...])
    acc = _dot(h.astype(BF16), w_ref[...])
    if not full:
        out_refs[0][...] = acc
        return
    pf_ref, qv_ref, qx1_ref, qx2_ref, ps_ref = out_refs
    pf_ref[...] = acc[:, :HY_OFF]
    ps_ref[...] = acc[:, S5_OFF:]
    hy = acc[:, HY_OFF:S5_OFF]
    row = lax.broadcasted_iota(jnp.int32, (tm, 1), 0)
    if grid_mode:
        first = (row % GRID_ROW) == 0
        last = (row % GRID_ROW) == GRID_ROW - 1
    else:
        first = row == 0
        last = row == tm - 1
    prev = jnp.where(first, 0.0, pltpu.roll(hy, 1, 0))
    nxt = jnp.where(last, 0.0, pltpu.roll(hy, tm - 1, 0))
    q = prev * cw_ref[0:1, :] + hy * cw_ref[1:2, :] + nxt * cw_ref[2:3, :] + cb_ref[...]
    folded = (qv_ref, qx1_ref, qx2_ref)
    cols = [slice(n * D_HYENA, (n + 1) * D_HYENA) for n in range(len(folded))]
    if nt == 1:
        half = tm // 2
        q_hi = _flip_rows(q[half:], j_ref[...])
        for r, c in zip(folded, cols):
            r[0] = q[:half, c]
            r[1] = q_hi[:, c]
        return
    i = pl.program_id(1)

    @pl.when(i < nt // 2)
    def _():
        for r, c in zip(folded, cols):
            r[...] = q[:, c]

    @pl.when(i >= nt // 2)
    def _():
        qf = _flip_rows(q, j_ref[...])
        for r, c in zip(folded, cols):
            r[...] = qf[:, c]


def _fold_spec(tm, nt, width):
    if nt == 1:
        return pl.BlockSpec((None, 2, tm // 2, width), lambda bi, i: (bi, 0, 0, 0))
    hn = nt // 2
    return pl.BlockSpec((None, None, tm, width),
                        lambda bi, i: (bi, i // hn, jnp.where(i < hn, i, nt - 1 - i), 0))


def _inproj(x, g, sh, sc, w, cw, cb, *, grid_mode, full):
    b, l, d = x.shape
    n = w.shape[1]
    tm = min(l, 512)
    nt = l // tm
    if not grid_mode:
        assert tm == l
    else:
        assert tm % GRID_ROW == 0
    assert nt == 1 or nt % 2 == 0
    row = lambda bi, i: (bi, i, 0)
    vec = lambda bi, i: (bi, 0, 0)
    const = lambda bi, i: (0, 0)
    flip = _flip_matrix(min(256, tm // 2 if nt == 1 else tm))
    plain = lambda wd: (jax.ShapeDtypeStruct((b, l, wd), F32), pl.BlockSpec((None, tm, wd), row))
    fold = lambda wd: (jax.ShapeDtypeStruct((b, 2, l // 2, wd), F32), _fold_spec(tm, nt, wd))
    outs = ([plain(D_FNET), fold(D_HYENA), fold(D_HYENA), fold(D_HYENA), plain(D_S5)] if full else [plain(n)])
    return pl.pallas_call(
        functools.partial(_inproj_kernel, tm=tm, nt=nt, grid_mode=grid_mode, full=full),
        out_shape=tuple(o[0] for o in outs),
        grid=(b, nt),
        in_specs=[pl.BlockSpec((None, tm, d), row),
                  pl.BlockSpec((1, d), const),
                  pl.BlockSpec((None, 1, d), vec),
                  pl.BlockSpec((None, 1, d), vec),
                  pl.BlockSpec((d, n), const),
                  pl.BlockSpec(cw.shape, const),
                  pl.BlockSpec(cb.shape, const),
                  pl.BlockSpec(flip.shape, const)],
        out_specs=tuple(o[1] for o in outs),
        compiler_params=_cp(("parallel", "parallel"), VMEM_LIMIT),
        name="inproj",
    )(x, g, sh, sc, w, cw, cb, flip)


def _fnet_kernel(x_ref, mc_ref, ms_ref, a_ref, b_ref, o_ref, accc, accs, *, nb):
    k = pl.program_id(2)

    @pl.when(k == 0)
    def _():
        accc[...] = jnp.zeros_like(accc)
        accs[...] = jnp.zeros_like(accs)

    mc = mc_ref[...]
    ms = ms_ref[...]
    for bi in range(nb):
        xb = x_ref[bi].astype(BF16)
        accc[bi] += _dot(mc, xb)
        accs[bi] += _dot(ms, xb)

    @pl.when(k == pl.num_programs(2) - 1)
    def _():
        for bi in range(nb):
            o_ref[bi] = (_dot(accc[bi].astype(BF16), a_ref[...])
                         + _dot(accs[bi].astype(BF16), b_ref[...]))


def _fnet(pf, mc, ms, amat, bmat):
    b, l, c = pf.shape
    nb = min(b, 2)
    t = min(l, 1024)
    return pl.pallas_call(
        functools.partial(_fnet_kernel, nb=nb),
        out_shape=jax.ShapeDtypeStruct((b, l, c), F32),
        grid=(b // nb, l // t, l // t),
        in_specs=[pl.BlockSpec((nb, t, c), lambda bb, i, k: (bb, k, 0)),
                  pl.BlockSpec((t, t), lambda bb, i, k: (i, k)),
                  pl.BlockSpec((t, t), lambda bb, i, k: (i, k)),
                  pl.BlockSpec((c, c), lambda bb, i, k: (0, 0)),
                  pl.BlockSpec((c, c), lambda bb, i, k: (0, 0))],
        out_specs=pl.BlockSpec((nb, t, c), lambda bb, i, k: (bb, i, 0)),
        scratch_shapes=[pltpu.VMEM((nb, t, c), F32), pltpu.VMEM((nb, t, c), F32)],
        compiler_params=_cp(("parallel", "parallel", "arbitrary"), VMEM_LIMIT),
        name="fnet_dft",
    )(pf, mc, ms, amat, bmat)


def _hy_fwd_kernel(z_ref, ce_ref, se_ref, co_ref, so_ref, *rest, nb, filtered):
    if filtered:
        kf_ref, pr_ref, pi_ref, acc = rest
    else:
        pr_ref, pi_ref, acc = rest
    k = pl.program_id(2)

    @pl.when(k == 0)
    def _():
        acc[...] = jnp.zeros_like(acc)

    c = z_ref.shape[-1]
    ev = jnp.concatenate([(z_ref[bi, 0] + z_ref[bi, 1]).astype(BF16) for bi in range(nb)], axis=1)
    od = jnp.concatenate([(z_ref[bi, 0] - z_ref[bi, 1]).astype(BF16) for bi in range(nb)], axis=1)
    acc[0] += _dot(ce_ref[...], ev)
    acc[1] += _dot(se_ref[...], od)
    acc[2] += _dot(co_ref[...], od)
    acc[3] += _dot(so_ref[...], ev)

    @pl.when(k == pl.num_programs(2) - 1)
    def _():
        for bi in range(nb):
            cols = slice(bi * c, (bi + 1) * c)
            for p in range(2):
                zr, zi = acc[2 * p, :, cols], acc[2 * p + 1, :, cols]
                if filtered:
                    kr, ki, kd = kf_ref[p, 0], kf_ref[p, 1], kf_ref[p, 2]
                    zr, zi = zr * kr - zi * ki, zr * ki + zi * kd
                pr_ref[bi, p] = zr.astype(pr_ref.dtype)
                pi_ref[bi, p] = zi.astype(pi_ref.dtype)


def _hy_tiles(h):
    return min(h, 1024), min(h, 512)


def _once(block_shape, index_map):
    return pl.BlockSpec(block_shape, index_map, pipeline_mode=pl.Buffered(1))


def _hy_fwd(z, mats, kf, out_dtype=BF16):
    b, _, h, c = z.shape
    nb = min(b, 2)
    ti, tk = _hy_tiles(h)
    mspec = pl.BlockSpec((ti, tk), lambda i, bb, k: (i, k))
    ospec = pl.BlockSpec((nb, 2, ti, c), lambda i, bb, k: (bb, 0, i, 0))
    filtered = kf is not None
    extra_specs = [_once((2, 3, ti, c), lambda i, bb, k: (0, 0, i, 0))] if filtered else []
    extra_args = (kf,) if filtered else ()
    return pl.pallas_call(
        functools.partial(_hy_fwd_kernel, nb=nb, filtered=filtered),
        out_shape=(jax.ShapeDtypeStruct((b, 2, h, c), out_dtype), jax.ShapeDtypeStruct((b, 2, h, c), out_dtype)),
        grid=(h // ti, b // nb, h // tk),
        in_specs=[pl.BlockSpec((nb, 2, tk, c), lambda i, bb, k: (bb, 0, k, 0)),
                  mspec, mspec, mspec, mspec] + extra_specs,
        out_specs=(ospec, ospec),
        scratch_shapes=[pltpu.VMEM((4, ti, nb * c), F32)],
        compiler_params=_cp(("parallel", "parallel", "arbitrary"), VMEM_LIMIT),
        name="hyena_fwd_dft",
    )(z, mats["ce"], mats["se"], mats["co"], mats["so"], *extra_args)


def _hy_inv_kernel(pr_ref, pi_ref, ce_ref, se_ref, co_ref, so_ref, xg_ref, zin_ref, bias_ref, o_ref, acc, *, nb):
    k = pl.program_id(2)

    @pl.when(k == 0)
    def _():
        acc[...] = jnp.zeros_like(acc)

    c = xg_ref.shape[-1]
    side = lambda ref, p: jnp.concatenate([ref[bi, p] for bi in range(nb)], axis=1)
    acc[0] += _dot(ce_ref[...], side(pr_ref, 0)) + _dot(so_ref[...], side(pi_ref, 1))
    acc[1] += _dot(se_ref[...], side(pi_ref, 0)) + _dot(co_ref[...], side(pr_ref, 1))

    @pl.when(k == pl.num_programs(2) - 1)
    def _():
        bias = bias_ref[...]
        for bi in range(nb):
            cols = slice(bi * c, (bi + 1) * c)
            u, v = acc[0, :, cols], acc[1, :, cols]
            o_ref[bi, 0] = xg_ref[bi, 0] * (u + v + zin_ref[bi, 0] * bias)
            o_ref[bi, 1] = xg_ref[bi, 1] * (u - v + zin_ref[bi, 1] * bias)


def _hy_inv(pr, pi, mats_t, xg, zin, bias):
    b, _, h, c = xg.shape
    nb = min(b, 2)
    ti, tk = min(h, 512), min(h, 1024)
    pspec = pl.BlockSpec((nb, 2, tk, c), lambda bb, i, k: (bb, 0, k, 0))
    mspec = pl.BlockSpec((ti, tk), lambda bb, i, k: (i, k))
    ospec = pl.BlockSpec((nb, 2, ti, c), lambda bb, i, k: (bb, 0, i, 0))
    return pl.pallas_call(
        functools.partial(_hy_inv_kernel, nb=nb),
        out_shape=jax.ShapeDtypeStruct((b, 2, h, c), F32),
        grid=(b // nb, h // ti, h // tk),
        in_specs=[pspec, pspec, mspec, mspec, mspec, mspec, ospec, ospec,
                  pl.BlockSpec((1, c), lambda bb, i, k: (0, 0))],
        out_specs=ospec,
        scratch_shapes=[pltpu.VMEM((2, ti, nb * c), F32)],
        compiler_params=_cp(("parallel", "parallel", "arbitrary"), VMEM_LIMIT),
        name="hyena_inv_dft",
    )(pr, pi, mats_t["ce"], mats_t["se"], mats_t["co"], mats_t["so"], xg, zin, bias.reshape(1, c))


def _s5_kernel(u_ref, s0f_ref, s0b_ref, mi_ref, mof_ref, mob_ref, mnf_ref, mnb_ref, atf_ref, atb_ref,
               d_ref, y_ref, hf_ref, hb_ref, x_scr, sf_scr, sb_scr, hinf_scr, hinb_scr, *, nc, t_chunk):
    for s in range(t_chunk):
        x_scr[:, s * LANES:(s + 1) * LANES] = u_ref[pl.ds(s, nc, stride=t_chunk), :].astype(BF16)
    x = x_scr[...]
    _to_slabs(sf_scr, _dot(x, mof_ref[...]), nc)
    _to_slabs(sb_scr, _dot(x, mob_ref[...]), nc)
    cf1, cf2 = atf_ref[0], atf_ref[1]
    cb1, cb2 = atb_ref[0], atb_ref[1]

    def body(c, carry):
        hf, hb = carry
        rf = pl.ds(pl.multiple_of(c * SLAB, SLAB), SLAB)
        rb = pl.ds(pl.multiple_of((nc - 1 - c) * SLAB, SLAB), SLAB)
        hinf_scr[rf, :] = hf
        hinb_scr[rb, :] = hb
        hf = cf1 * hf + cf2 * pltpu.roll(hf, SLAB // 2, 0) + sf_scr[rf, :]
        hb = cb1 * hb + cb2 * pltpu.roll(hb, SLAB // 2, 0) + sb_scr[rb, :]
        return hf, hb

    hf, hb = lax.fori_loop(0, nc, body, (s0f_ref[...], s0b_ref[...]), unroll=4)
    hf_ref[...] = hf
    hb_ref[...] = hb
    y = (_dot(x, mi_ref[...])
         + _dot(_from_slabs(hinf_scr, nc).astype(BF16), mnf_ref[...])
         + _dot(_from_slabs(hinb_scr, nc).astype(BF16), mnb_ref[...]))
    d = d_ref[...]
    for t in range(t_chunk):
        rows = pl.ds(t, nc, stride=t_chunk)
        y_ref[rows, :] = y[:, t * LANES:(t + 1) * LANES] + d * u_ref[rows, :]


def _s5(u, s0f, s0b, mats, d_skip):
    b, l, _ = u.shape
    tc = S5_CHUNK
    nc = l // tc
    kin = tc * LANES
    ns = 2 * STATE_HALF
    mi, mof, mob, mnf, mnb, atf, atb = mats
    st_spec = pl.BlockSpec((None, None, SLAB, LANES), lambda j, bi: (bi, j, 0, 0))
    mode = dict(pipeline_mode=pl.Buffered(1)) if l > 1024 else {}
    w3 = lambda s1, s2: pl.BlockSpec((None, s1, s2), lambda j, bi: (j, 0, 0), **mode)
    coef = pl.BlockSpec((None, 2, SLAB, LANES), lambda j, bi: (j, 0, 0, 0))
    u_spec = pl.BlockSpec((None, l, LANES), lambda j, bi: (bi, 0, j))
    state = jax.ShapeDtypeStruct((b, N_S5_BLOCKS, SLAB, LANES), F32)
    return pl.pallas_call(
        functools.partial(_s5_kernel, nc=nc, t_chunk=tc),
        out_shape=(jax.ShapeDtypeStruct((b, l, D_S5), F32), state, state),
        grid=(N_S5_BLOCKS, b),
        in_specs=[u_spec, st_spec, st_spec,
                  w3(kin, kin), w3(kin, ns), w3(kin, ns), w3(ns, kin), w3(ns, kin),
                  coef, coef,
                  pl.BlockSpec((1, LANES), lambda j, bi: (0, j))],
        out_specs=(u_spec, st_spec, st_spec),
        scratch_shapes=[pltpu.VMEM((nc, kin), BF16)] + [pltpu.VMEM((nc * SLAB, LANES), F32)] * 4,
        compiler_params=_cp(("parallel", "parallel"), VMEM_LIMIT),
        name="s5_scan",
    )(u, s0f, s0b, mi, mof, mob, mnf, mnb, atf, atb, d_skip.reshape(1, D_S5))


def _s5_matrices(a_re, a_im, log_dt, b_re, b_im, c_re, c_im):
    tc = S5_CHUNK
    g, p, hd = S5_GROUPS, S5_STATE, S5_GROUP
    lam = lax.complex(a_re.astype(F32), a_im.astype(F32))
    dt = jnp.exp(log_dt.astype(F32))[..., None]
    a_bar = jnp.exp(lam * dt)
    b_bar = ((a_bar - 1.0) / lam)[..., None] * lax.complex(b_re.astype(F32), b_im.astype(F32))
    cc = lax.complex(c_re.astype(F32), c_im.astype(F32))
    ks = jnp.arange(tc + 1, dtype=F32)
    pw = jnp.exp((lam * dt)[:, :, None, :] * ks[None, None, :, None])
    kern = jnp.real(jnp.einsum('dgip,dgkp,dgpj->dgkij', cc, pw[:, :, :tc], b_bar, precision=HI))
    s_idx = jnp.arange(tc)[:, None]
    t_idx = jnp.arange(tc)[None, :]
    lag = t_idx - s_idx
    kf = jnp.where((lag >= 0)[None, :, :, None, None], kern[0][:, jnp.clip(lag, 0, tc - 1)], 0.0)
    kb = jnp.where((lag <= 0)[None, :, :, None, None], kern[1][:, jnp.clip(-lag, 0, tc - 1)], 0.0)
    ktot = kf + kb
    def place(slab, inner, row_group_width):
        width = slab.shape[-1]
        src = np.arange(width)
        e = np.zeros((width, GROUPS_PER_BLOCK * width), np.float32)
        for hh in range(GROUPS_PER_BLOCK):
            e[src, (src // inner) * GROUPS_PER_BLOCK * inner + hh * inner + src % inner] = 1.0
        wide = jnp.einsum('Jrk,kc->Jrc', slab, jnp.asarray(e), precision=HI)
        rows = lax.broadcasted_iota(jnp.int32, wide.shape[1:], 0)
        cols = lax.broadcasted_iota(jnp.int32, wide.shape[1:], 1)
        keep = (rows // row_group_width) % GROUPS_PER_BLOCK == (cols // inner) % GROUPS_PER_BLOCK
        return jnp.where(keep[None], wide, 0.0)

    kt = ktot.reshape(N_S5_BLOCKS, GROUPS_PER_BLOCK, tc, tc, hd, hd)
    kt = jnp.transpose(kt, (0, 2, 1, 5, 3, 4))
    mi = place(kt.reshape(N_S5_BLOCKS, tc * LANES, tc * hd), hd, hd)

    def parts(z):
        return jnp.stack([jnp.real(z), jnp.imag(z)], axis=0)

    def out_mat(d, powers):
        z = powers[:, :, :, None] * b_bar[d][:, None, :, :]
        z = parts(z).reshape(2, N_S5_BLOCKS, GROUPS_PER_BLOCK, tc, p, hd)
        z = jnp.transpose(z, (1, 3, 2, 5, 0, 4))
        return place(z.reshape(N_S5_BLOCKS, tc * LANES, 2 * p), p, hd)

    def in_mat(d, powers):
        z = cc[d][:, None, :, :] * powers[:, :, None, :]
        z = jnp.stack([jnp.real(z), -jnp.imag(z)], axis=0)
        z = z.reshape(2, N_S5_BLOCKS, GROUPS_PER_BLOCK, tc, hd, p)
        z = jnp.transpose(z, (1, 0, 2, 5, 3, 4))
        return place(z.reshape(N_S5_BLOCKS, 2 * STATE_HALF, tc * hd), hd, p)

    rev = jnp.arange(tc - 1, -1, -1)
    mof = out_mat(0, pw[0][:, rev])
    mob = out_mat(1, pw[1][:, :tc])
    mnf = in_mat(0, pw[0][:, 1:tc + 1])
    mnb = in_mat(1, pw[1][:, tc - jnp.arange(tc)])

    def step(d):
        ar = jnp.real(pw[d][:, tc]).reshape(N_S5_BLOCKS, STATE_HALF)
        ai = jnp.imag(pw[d][:, tc]).reshape(N_S5_BLOCKS, STATE_HALF)
        c1 = jnp.concatenate([ar, ar], axis=1)
        c2 = jnp.concatenate([-ai, ai], axis=1)
        return jnp.stack([c1, c2], axis=1).reshape(N_S5_BLOCKS, 2, SLAB, LANES)

    bf = lambda m: m.astype(BF16)
    return bf(mi), bf(mof), bf(mob), bf(mnf), bf(mnb), step(0), step(1)


def _outproj_kernel(yf_ref, yh_ref, ys_ref, x_ref, gate_ref, w1_ref, w2_ref, w3_ref, gw_ref, gb_ref, j_ref, o_ref,
                    *, nt):
    y = ys_ref[...]
    y = 0.5 * y * (1.0 + jnp.tanh(math.sqrt(2.0 / math.pi) * (y + 0.044715 * (y * y * y))))
    y = y * _sigmoid(_dot(y.astype(BF16), gw_ref[...]) + gb_ref[...])
    acc = _dot(yf_ref[...].astype(BF16), w1_ref[...]) + _dot(y.astype(BF16), w3_ref[...])

    def finish(yh):
        o_ref[...] = x_ref[...] + gate_ref[...] * (acc + _dot(yh.astype(BF16), w2_ref[...]))

    if nt == 1:
        finish(jnp.concatenate([yh_ref[0], _flip_rows(yh_ref[1], j_ref[...])], axis=0))
        return
    i = pl.program_id(1)

    @pl.when(i < nt // 2)
    def _():
        finish(yh_ref[...])

    @pl.when(i >= nt // 2)
    def _():
        finish(_flip_rows(yh_ref[...], j_ref[...]))


def _outproj(yf, yh, ys, x, gate, w1, w2, w3, gw, gb):
    b, l, d = x.shape
    tm = min(l, 512)
    nt = l // tm
    row = lambda bi, i: (bi, i, 0)
    const = lambda bi, i: (0, 0)
    full = lambda a: pl.BlockSpec(a.shape, const)
    flip = _flip_matrix(min(256, tm // 2 if nt == 1 else tm))
    return pl.pallas_call(
        functools.partial(_outproj_kernel, nt=nt),
        out_shape=jax.ShapeDtypeStruct((b, l, d), F32),
        grid=(b, nt),
        in_specs=[pl.BlockSpec((None, tm, yf.shape[2]), row),
                  _fold_spec(tm, nt, yh.shape[3]),
                  pl.BlockSpec((None, tm, ys.shape[2]), row),
                  pl.BlockSpec((None, tm, d), row),
                  pl.BlockSpec((None, 1, d), lambda bi, i: (bi, 0, 0)),
                  full(w1), full(w2), full(w3), full(gw), full(gb), full(flip)],
        out_specs=pl.BlockSpec((None, tm, d), row),
        compiler_params=_cp(("parallel", "parallel"), VMEM_LIMIT),
        name="outproj",
    )(yf, yh, ys, x, gate, w1, w2, w3, gw, gb, flip)


def _ffn_kernel(x_ref, g_ref, sh_ref, sc_ref, gate_ref, rwh_ref, rwl_ref, rb_ref, wg_ref, wu_ref, wd_ref,
                fg_ref, o_ref, xn_scr, acc_scr, comb_scr, *, moe, final_norm):
    e = pl.program_id(2)
    f = pl.program_id(3)
    first = jnp.logical_and(e == 0, f == 0)
    last = jnp.logical_and(e == pl.num_programs(2) - 1, f == pl.num_programs(3) - 1)

    @pl.when(first)
    def _():
        h = _rms_mod(x_ref[...], g_ref[...], sh_ref[...], sc_ref[...])
        xn_scr[...] = h.astype(BF16)
        acc_scr[...] = jnp.zeros_like(acc_scr)
        if moe:
            h_hi, h_lo = _split_bf16(h)
            rwh = rwh_ref[...]
            lg = _dot(h_hi, rwh) + _dot(h_hi, rwl_ref[...]) + _dot(h_lo, rwh) + rb_ref[...]
            lane = lax.broadcasted_iota(jnp.int32, lg.shape, 1)
            m1 = jnp.max(lg, axis=-1, keepdims=True)
            i1 = jnp.min(jnp.where(lg == m1, lane, LANES), axis=-1, keepdims=True)
            lg2 = jnp.where(lane == i1, NEG_BIG, lg)
            m2 = jnp.max(lg2, axis=-1, keepdims=True)
            i2 = jnp.min(jnp.where(lg2 == m2, lane, LANES), axis=-1, keepdims=True)
            ex = jnp.exp(m2 - m1)
            den = 1.0 + ex
            comb_scr[...] = (jnp.where(lane == i1, 1.0 / den, 0.0)
                             + jnp.where(lane == i2, ex / den, 0.0))

    xn = xn_scr[...]
    gt = _dot(xn, wg_ref[...])
    up = _dot(xn, wu_ref[...])
    hmid = gt * _sigmoid(gt) * up
    if moe:
        comb = comb_scr[...]
        lane = lax.broadcasted_iota(jnp.int32, comb.shape, 1)
        hmid = hmid * jnp.sum(jnp.where(lane == e, comb, 0.0), axis=-1, keepdims=True)
    acc_scr[...] += _dot(hmid.astype(BF16), wd_ref[...])

    @pl.when(last)
    def _():
        y = x_ref[...] + gate_ref[...] * acc_scr[...]
        if final_norm:
            ms = jnp.mean(y * y, axis=-1, keepdims=True)
            y = y * lax.rsqrt(ms + RMS_EPS) * fg_ref[...]
        o_ref[...] = y


def _ffn(x, g, sh, sc, gate, router, wg, wu, wd, final_g, *, tf):
    b, l, d = x.shape
    ne, _, ff = wg.shape
    moe = router is not None
    final_norm = final_g is not None
    tm = min(l, 1024)
    if moe:
        rwh, rwl, rb = router
    else:
        rwh = rwl = jnp.zeros((d, LANES), BF16)
        rb = jnp.zeros((1, LANES), F32)
    fg = final_g if final_norm else jnp.ones((1, d), F32)
    row = lambda bi, i, e, f: (bi, i, 0)
    vec = lambda bi, i, e, f: (bi, 0, 0)
    const = lambda bi, i, e, f: (0, 0)
    return pl.pallas_call(
        functools.partial(_ffn_kernel, moe=moe, final_norm=final_norm),
        out_shape=jax.ShapeDtypeStruct((b, l, d), F32),
        grid=(b, l // tm, ne, ff // tf),
        in_specs=[pl.BlockSpec((None, tm, d), row),
                  pl.BlockSpec((1, d), const),
                  pl.BlockSpec((None, 1, d), vec),
                  pl.BlockSpec((None, 1, d), vec),
                  pl.BlockSpec((None, 1, d), vec),
                  pl.BlockSpec((d, LANES), const),
                  pl.BlockSpec((d, LANES), const),
                  pl.BlockSpec((1, LANES), const),
                  pl.BlockSpec((None, d, tf), lambda bi, i, e, f: (e, 0, f)),
                  pl.BlockSpec((None, d, tf), lambda bi, i, e, f: (e, 0, f)),
                  pl.BlockSpec((None, tf, d), lambda bi, i, e, f: (e, f, 0)),
                  pl.BlockSpec((1, d), const)],
        out_specs=pl.BlockSpec((None, tm, d), row),
        scratch_shapes=[pltpu.VMEM((tm, d), BF16), pltpu.VMEM((tm, d), F32), pltpu.VMEM((tm, LANES), F32)],
        compiler_params=_cp(("parallel", "parallel", "arbitrary", "arbitrary"), VMEM_LIMIT),
        name="moe_ffn" if moe else "dense_ffn",
    )(x, g, sh, sc, gate, rwh, rwl, rb, wg, wu, wd, fg)


SLAB = 8
ROUTE_TM = 512
GROUP_TM = 1024


def _to_slabs(ref, val, n):
    for s in range(SLAB):
        ref[pl.ds(s, n, stride=SLAB), :] = val[:, s * LANES:(s + 1) * LANES]


def _from_slabs(ref, n):
    return jnp.concatenate([ref[pl.ds(s, n, stride=SLAB), :] for s in range(SLAB)], axis=1)


def _route_kernel(x_ref, g_ref, sh_ref, sc_ref, rwh_ref, rwl_ref, rb_ref, tri_ref,
                  xs_ref, meta_ref, gate_ref, cnt_ref, carry_scr, *, tm):
    step = pl.program_id(0) * pl.num_programs(1) + pl.program_id(1)

    @pl.when(step == 0)
    def _():
        carry_scr[...] = jnp.zeros_like(carry_scr)

    h = _rms_mod(x_ref[...], g_ref[...], sh_ref[...], sc_ref[...])
    _to_slabs(xs_ref, h, tm)
    h_hi, h_lo = _split_bf16(h)
    rwh = rwh_ref[...]
    lg = _dot(h_hi, rwh) + _dot(h_hi, rwl_ref[...]) + _dot(h_lo, rwh) + rb_ref[...]
    lane = lax.broadcasted_iota(jnp.int32, lg.shape, 1)
    m1 = jnp.max(lg, axis=-1, keepdims=True)
    i1 = jnp.min(jnp.where(lg == m1, lane, LANES), axis=-1, keepdims=True)
    lg2 = jnp.where(lane == i1, NEG_BIG, lg)
    m2 = jnp.max(lg2, axis=-1, keepdims=True)
    i2 = jnp.min(jnp.where(lg2 == m2, lane, LANES), axis=-1, keepdims=True)
    ex = jnp.exp(m2 - m1)
    den = 1.0 + ex
    gate_ref[...] = jnp.where(lane == 0, 1.0 / den, jnp.where(lane == 1, ex / den, 0.0))
    sel1 = lane == i1
    sel2 = lane == i2
    tri = tri_ref[...]
    p1 = _dot(tri, sel1.astype(BF16))
    p2 = _dot(tri, sel2.astype(BF16))
    tot1 = p1[tm - 1:tm, :]
    tot2 = p2[tm - 1:tm, :]
    carry = carry_scr[...]
    r1 = jnp.sum(jnp.where(sel1, carry + p1 - 1.0, 0.0), axis=-1, keepdims=True)
    r2 = jnp.sum(jnp.where(sel2, carry + tot1 + p2 - 1.0, 0.0), axis=-1, keepdims=True)
    carry = carry + tot1 + tot2
    carry_scr[...] = carry
    cnt_ref[...] = carry.astype(jnp.int32)
    meta_ref[...] = jnp.where(lane == 0, i1, jnp.where(lane == 1, i2, jnp.where(
        lane == 2, r1.astype(jnp.int32), jnp.where(lane == 3, r2.astype(jnp.int32), 0))))


def _route(x, g, sh, sc, rwh, rwl, rb):
    b, l, d = x.shape
    tm = min(l, ROUTE_TM)
    n = b * l
    nl = l // tm
    tri = jnp.tril(jnp.ones((tm, tm), F32)).astype(BF16)
    row = lambda bi, i: (bi, i, 0)
    vec = lambda bi, i: (bi, 0, 0)
    const = lambda bi, i: (0, 0)
    flat = lambda bi, i: (bi * nl + i, 0)
    return pl.pallas_call(
        functools.partial(_route_kernel, tm=tm),
        out_shape=(jax.ShapeDtypeStruct((n * SLAB, LANES), F32),
                   jax.ShapeDtypeStruct((n, LANES), jnp.int32),
                   jax.ShapeDtypeStruct((n, LANES), F32),
                   jax.ShapeDtypeStruct((1, LANES), jnp.int32)),
        grid=(b, nl),
        in_specs=[pl.BlockSpec((None, tm, d), row),
                  pl.BlockSpec((1, d), const),
                  pl.BlockSpec((None, 1, d), vec),
                  pl.BlockSpec((None, 1, d), vec),
                  pl.BlockSpec((d, LANES), const),
                  pl.BlockSpec((d, LANES), const),
                  pl.BlockSpec((1, LANES), const),
                  pl.BlockSpec((tm, tm), const)],
        out_specs=(pl.BlockSpec((tm * SLAB, LANES), flat),
                   pl.BlockSpec((tm, LANES), flat),
                   pl.BlockSpec((tm, LANES), flat),
                   pl.BlockSpec((1, LANES), const)),
        scratch_shapes=[pltpu.VMEM((1, LANES), F32)],
        compiler_params=_cp(("arbitrary", "arbitrary"), VMEM_LIMIT),
        name="moe_route",
    )(x, g, sh, sc, rwh, rwl, rb, tri)


def _dispatch_kernel(pos_hbm, xs_ref, init_hbm, out_hbm, idx_smem, isem, sem, *, tm):
    del init_hbm
    step = pl.program_id(0)
    icopy = pltpu.make_async_copy(pos_hbm.at[pl.ds(step * 2 * tm, 2 * tm)], idx_smem, isem)
    icopy.start()
    icopy.wait()

    def row_copy(i, k):
        return pltpu.make_async_copy(xs_ref.at[pl.ds(i * SLAB, SLAB), :], out_hbm.at[idx_smem[2 * i + k]], sem)

    def issue(i, carry):
        row_copy(i, 0).start()
        row_copy(i, 1).start()
        return carry

    def drain(i, carry):
        row_copy(i, 0).wait()
        row_copy(i, 1).wait()
        return carry

    lax.fori_loop(0, tm, issue, 0, unroll=8)
    lax.fori_loop(0, tm, drain, 0, unroll=8)


def _dispatch(xs, pos, n_rows):
    n = xs.shape[0] // SLAB
    tm = min(n, ROUTE_TM)
    init = jnp.zeros((n_rows, SLAB, LANES), F32)
    return pl.pallas_call(
        functools.partial(_dispatch_kernel, tm=tm),
        out_shape=jax.ShapeDtypeStruct((n_rows, SLAB, LANES), F32),
        grid=(n // tm,),
        in_specs=[pl.BlockSpec(memory_space=pl.ANY),
                  pl.BlockSpec((tm * SLAB, LANES), lambda i: (i, 0)),
                  pl.BlockSpec(memory_space=pl.ANY)],
        out_specs=pl.BlockSpec(memory_space=pl.ANY),
        scratch_shapes=[pltpu.SMEM((2 * tm,), jnp.int32), pltpu.SemaphoreType.DMA, pltpu.SemaphoreType.DMA],
        input_output_aliases={2: 0},
        compiler_params=_cp(("arbitrary",), VMEM_LIMIT),
        name="moe_dispatch",
    )(pos.reshape(-1), xs, init)


def _group_ffn_kernel(te_ref, x_ref, wg_ref, wu_ref, wd_ref, o_ref, xn_scr, acc_scr, *, tm, nt):
    t = pl.program_id(0)
    f = pl.program_id(1)
    valid = t < te_ref[nt]
    last = f == pl.num_programs(1) - 1

    @pl.when(jnp.logical_and(valid, f == 0))
    def _():
        xn_scr[...] = _from_slabs(x_ref, tm).astype(BF16)
        acc_scr[...] = jnp.zeros_like(acc_scr)

    @pl.when(valid)
    def _():
        xn = xn_scr[...]
        gt = _dot(xn, wg_ref[...])
        up = _dot(xn, wu_ref[...])
        acc_scr[...] += _dot((gt * _sigmoid(gt) * up).astype(BF16), wd_ref[...])

    @pl.when(jnp.logical_and(valid, last))
    def _():
        _to_slabs(o_ref, acc_scr[...], tm)

    @pl.when(jnp.logical_and(jnp.logical_not(valid), last))
    def _():
        o_ref[...] = jnp.zeros_like(o_ref)


def _group_ffn(xs, tile_expert, wg, wu, wd, *, tf):
    n_rows = xs.shape[0]
    tm = GROUP_TM
    nt = n_rows // tm
    _, d, ff = wg.shape
    x2 = xs.reshape(n_rows * SLAB, LANES)
    rows = pl.BlockSpec((tm * SLAB, LANES), lambda t, f, te: (t, 0))
    out = pl.pallas_call(
        functools.partial(_group_ffn_kernel, tm=tm, nt=nt),
        out_shape=jax.ShapeDtypeStruct((n_rows * SLAB, LANES), F32),
        grid_spec=pltpu.PrefetchScalarGridSpec(
            num_scalar_prefetch=1,
            grid=(nt, ff // tf),
            in_specs=[rows,
                      pl.BlockSpec((None, d, tf), lambda t, f, te: (te[t], 0, f)),
                      pl.BlockSpec((None, d, tf), lambda t, f, te: (te[t], 0, f)),
                      pl.BlockSpec((None, tf, d), lambda t, f, te: (te[t], f, 0))],
            out_specs=rows,
            scratch_shapes=[pltpu.VMEM((tm, d), BF16), pltpu.VMEM((tm, d), F32)]),
        compiler_params=_cp(("arbitrary", "arbitrary"), VMEM_LIMIT),
        name="moe_group_ffn",
    )(tile_expert, x2, wg, wu, wd)
    return out.reshape(n_rows, SLAB, LANES)


def _combine_kernel(pos_hbm, ys_hbm, x_ref, gate_ref, w_ref, fg_ref, o_ref, idx_smem, ybuf, isem, sem,
                    *, tm, final_norm):
    step = pl.program_id(0) * pl.num_programs(1) + pl.program_id(1)
    icopy = pltpu.make_async_copy(pos_hbm.at[pl.ds(step * 2 * tm, 2 * tm)], idx_smem, isem)
    icopy.start()
    icopy.wait()

    def row_copy(i, k):
        return pltpu.make_async_copy(ys_hbm.at[idx_smem[2 * i + k]], ybuf.at[k, pl.ds(i * SLAB, SLAB), :], sem)

    def issue(i, carry):
        row_copy(i, 0).start()
        row_copy(i, 1).start()
        return carry

    def drain(i, carry):
        row_copy(i, 0).wait()
        row_copy(i, 1).wait()
        return carry

    lax.fori_loop(0, tm, issue, 0, unroll=8)
    lax.fori_loop(0, tm, drain, 0, unroll=8)
    w = w_ref[...]
    moe = w[:, 0:1] * _from_slabs(ybuf.at[0], tm) + w[:, 1:2] * _from_slabs(ybuf.at[1], tm)
    y = x_ref[...] + gate_ref[...] * moe
    if final_norm:
        ms = jnp.mean(y * y, axis=-1, keepdims=True)
        y = y * lax.rsqrt(ms + RMS_EPS) * fg_ref[...]
    o_ref[...] = y


def _combine(ys, pos, x, gate, w, final_g):
    b, l, d = x.shape
    tm = min(l, ROUTE_TM)
    nl = l // tm
    final_norm = final_g is not None
    fg = final_g if final_norm else jnp.ones((1, d), F32)
    row = lambda bi, i: (bi, i, 0)
    return pl.pallas_call(
        functools.partial(_combine_kernel, tm=tm, final_norm=final_norm),
        out_shape=jax.ShapeDtypeStruct((b, l, d), F32),
        grid=(b, nl),
        in_specs=[pl.BlockSpec(memory_space=pl.ANY),
                  pl.BlockSpec(memory_space=pl.ANY),
                  pl.BlockSpec((None, tm, d), row),
                  pl.BlockSpec((None, 1, d), lambda bi, i: (bi, 0, 0)),
                  pl.BlockSpec((tm, LANES), lambda bi, i: (bi * nl + i, 0)),
                  pl.BlockSpec((1, d), lambda bi, i: (0, 0))],
        out_specs=pl.BlockSpec((None, tm, d), row),
        scratch_shapes=[pltpu.SMEM((2 * tm,), jnp.int32), pltpu.VMEM((2, tm * SLAB, LANES), F32),
                        pltpu.SemaphoreType.DMA, pltpu.SemaphoreType.DMA],
        compiler_params=_cp(("arbitrary", "arbitrary"), VMEM_LIMIT),
        name="moe_combine",
    )(pos.reshape(-1), ys, x, gate, w, fg)


def _moe(x, g, sh, sc, gate, router_w, router_b, wg, wu, wd, final_g):
    b, l, d = x.shape
    n = b * l
    ne = router_w.shape[-1]
    rw = jnp.pad(router_w.astype(F32), ((0, 0), (0, LANES - ne)))
    rwh, rwl = _split_bf16(rw)
    rb = jnp.pad(router_b.astype(F32), (0, LANES - ne), constant_values=NEG_BIG).reshape(1, LANES)
    xs, meta, w, counts = _route(x, g, sh, sc, rwh, rwl, rb)
    counts = counts[0, :ne]
    padded = (counts + GROUP_TM - 1) // GROUP_TM * GROUP_TM
    ends = jnp.cumsum(padded)
    offs = ends - padded
    experts = jnp.arange(ne, dtype=jnp.int32)
    off_of = lambda e: jnp.sum(jnp.where(e[:, None] == experts[None, :], offs[None, :], 0), axis=1)
    pos = jnp.stack([off_of(meta[:, 0]) + meta[:, 2], off_of(meta[:, 1]) + meta[:, 3]], axis=1).astype(jnp.int32)
    n_rows = TOP_K * n + ne * GROUP_TM
    nt = n_rows // GROUP_TM
    tile_start = jnp.arange(nt, dtype=jnp.int32) * GROUP_TM
    tile_e = jnp.sum(tile_start[:, None] >= ends[None, :], axis=1)
    n_live = ends[-1] // GROUP_TM
    last_e = jnp.sum((n_live - 1) * GROUP_TM >= ends)
    tile_e = jnp.where(tile_start < ends[-1], tile_e, last_e)
    tile_expert = jnp.concatenate([tile_e, n_live[None]]).astype(jnp.int32)
    xd = _dispatch(xs, pos, n_rows)
    ys = _group_ffn(xd, tile_expert, wg, wu, wd, tf=512)
    return _combine(ys, pos, x, gate, w, final_g)


def _hyena_filters(l, fw1, fb1, freq1, fw2, fb2, freq2, fw3):
    half = l // 2
    pos = np.concatenate([np.arange(half), l - 1 - np.arange(half)]).astype(np.float32)[:, None]
    t = jnp.asarray(pos) / (l - 1)
    w = (2.0 * math.pi / l) * jnp.asarray(pos)
    bands = jnp.linspace(1e-4, HYENA_BANDS - 1, HYENA_BANDS, dtype=F32)[None, :]
    z = jnp.concatenate([t, jnp.cos(bands * w), -jnp.sin(bands * w)], axis=-1)
    h = jnp.sin(freq1.astype(F32) * (jnp.dot(z, fw1.astype(F32), precision=HI) + fb1.astype(F32)))
    h = jnp.sin(freq2.astype(F32) * (jnp.dot(h, fw2.astype(F32), precision=HI) + fb2.astype(F32)))
    h = jnp.dot(h, fw3.astype(F32), precision=HI).reshape(l, 2, HYENA_ORDER, D_HYENA)
    max_decay = math.log(HYENA_DECAY_TARGET) / HYENA_FAST_DECAY_PCT
    min_decay = math.log(HYENA_DECAY_TARGET) / HYENA_SLOW_DECAY_PCT
    deltas = jnp.abs(jnp.linspace(min_decay, max_decay, D_HYENA, dtype=F32))
    return h * jnp.exp(-t[:, :, None, None] * deltas)


def _hyena_spectra(l, mats_fwd, *filter_params):
    half = l // 2
    n2 = 2 * l
    h = _hyena_filters(l, *filter_params)
    tot = jnp.sum(jnp.abs(h), axis=0)
    hb0 = h[0, 1]
    denom = tot[0] + tot[1] - jnp.abs(hb0)
    sig = h.reshape(2, half, 2, HYENA_ORDER, D_HYENA).transpose(2, 3, 0, 1, 4)
    sig = sig.reshape(2 * HYENA_ORDER, 2, half, D_HYENA)
    fr, fi = _hy_fwd(sig, mats_fwd, None, out_dtype=F32)
    fr = fr.reshape(2, HYENA_ORDER, 2, half, D_HYENA)
    fi = fi.reshape(2, HYENA_ORDER, 2, half, D_HYENA)
    freq = 2 * np.arange(half)[None, :] + np.arange(2)[:, None]
    phi = np.pi * freq / (2.0 * l)
    cph = jnp.asarray(np.cos(phi), F32)[:, :, None]
    sph = jnp.asarray(np.sin(phi), F32)[:, :, None]
    re = fr * cph - fi * sph
    im = fr * sph + fi * cph
    dc_slot = jnp.asarray(freq == 0)[:, :, None]
    k_re = re[0] + re[1] - hb0[:, None, None, :]
    k_im = im[0] - im[1]
    k_nyq = fi[0] + fi[1] - hb0[:, None, None, :]
    wgt = jnp.where(dc_slot, 1.0, 2.0) / (n2 * denom[:, None, None, :])
    kr = k_re * wgt
    ki = jnp.where(dc_slot, 0.0, k_im * wgt)
    kd = jnp.where(dc_slot, k_nyq * wgt, kr)
    return [jnp.stack([kr[o], ki[o], kd[o]], axis=1) for o in range(HYENA_ORDER)]


def _fnet_maps(w_f, l):
    d = FNET_HEAD_DIM
    idx = np.arange(d)
    ang = 2.0 * np.pi * ((idx[:, None] * idx[None, :]) % d) / d
    scale = 1.0 / math.sqrt(d * l)
    cd = jnp.asarray(np.cos(ang) * scale, F32)
    sd = jnp.asarray(np.sin(ang) * scale, F32)
    wf = w_f.astype(F32)
    a = jnp.einsum('de,hef->hdf', cd, wf, precision=HI)
    bm = -jnp.einsum('de,hef->hdf', sd, wf, precision=HI)
    eye = jnp.eye(FNET_HEADS, dtype=F32)
    blk = lambda m: jnp.einsum('hdf,hg->hdgf', m, eye).reshape(D_FNET, D_FNET).astype(BF16)
    return blk(a), blk(bm)


def kernel(x, c, ctx, c_ctx, ada_w, ada_b, norm_mix_g, norm_ffn_g, w_in, w_out, fnet_w, hy_conv_w, hy_conv_b,
           hy_fw1, hy_fb1, hy_freq1, hy_fw2, hy_fb2, hy_freq2, hy_fw3, hy_bias, s5_a_re, s5_a_im, s5_log_dt,
           s5_b_re, s5_b_im, s5_c_re, s5_c_im, s5_d, s5_glu_w, s5_glu_b, ffn_w_gate, ffn_w_up, ffn_w_down,
           moe_router_w, moe_router_b, moe_w_gate, moe_w_up, moe_w_down, final_g):
    b, l, d = x.shape
    lc = ctx.shape[1]
    depth = ada_w.shape[0]
    assert l % LANES == 0 and lc % LANES == 0 and d % LANES == 0

    dft = {}
    for n in sorted({l, lc}):
        fwd = _hyena_dft(n)
        dft[n] = dict(hy_fwd=fwd, hy_inv={k: m.T for k, m in fwd.items()}, fn=_dft_cos_sin(n, n))

    cc = jnp.concatenate([c, c_ctx[None, :]], axis=0).astype(F32)
    rows = -(-(b + 1) // 8) * 8
    cc = jnp.pad(cc, ((0, rows - (b + 1)), (0, 0)))

    def mix(li, xs, sh, sc, grid_mode, s0f, s0b, s5m, states_only):
        n_tok = xs.shape[1]
        g_mix = norm_mix_g[li].reshape(1, d)
        if states_only:
            (ps,) = _inproj(xs, g_mix, sh, sc, w_in[li][:, S5_OFF:].astype(BF16),
                            jnp.zeros((3, LANES), F32), jnp.zeros((1, LANES), F32),
                            grid_mode=grid_mode, full=False)
            _, hf, hb = _s5(ps, s0f, s0b, s5m, s5_d[li])
            return None, hf, hb
        pf, qv, qx1, qx2, ps = _inproj(
            xs, g_mix, sh, sc, w_in[li].astype(BF16), hy_conv_w[li].astype(F32),
            hy_conv_b[li].astype(F32).reshape(1, -1), grid_mode=grid_mode, full=True)
        tabs = dft[n_tok]
        amat, bmat = _fnet_maps(fnet_w[li], n_tok)
        yf = _fnet(pf, tabs['fn'][0], tabs['fn'][1], amat, bmat)
        spectra = _hyena_spectra(n_tok, tabs['hy_fwd'], hy_fw1[li], hy_fb1[li], hy_freq1[li], hy_fw2[li],
                                 hy_fb2[li], hy_freq2[li], hy_fw3[li])
        z = qv
        for o, xg in enumerate((qx1, qx2)):
            pr, pi = _hy_fwd(z, tabs['hy_fwd'], spectra[o])
            z = _hy_inv(pr, pi, tabs['hy_inv'], xg, z, hy_bias[li][o].astype(F32))
        ys, hf, hb = _s5(ps, s0f, s0b, s5m, s5_d[li])
        return (yf, z, ys), hf, hb

    def out_proj(li, parts, xs, gate):
        wo = w_out[li].astype(BF16)
        return _outproj(parts[0], parts[1], parts[2], xs, gate,
                        wo[:D_FNET], wo[D_FNET:D_FNET + D_HYENA], wo[D_FNET + D_HYENA:],
                        s5_glu_w[li].astype(BF16), s5_glu_b[li].astype(F32).reshape(1, -1))

    def channel_mix(li, xs, sh, sc, gate, fin):
        i = li // 2
        g_ffn = norm_ffn_g[li].reshape(1, d)
        if li % 2 == 0:
            return _ffn(xs, g_ffn, sh, sc, gate, None, ffn_w_gate[i][None].astype(BF16),
                        ffn_w_up[i][None].astype(BF16), ffn_w_down[i][None].astype(BF16), fin, tf=1408)
        return _moe(xs, g_ffn, sh, sc, gate, moe_router_w[i], moe_router_b[i], moe_w_gate[i].astype(BF16),
                    moe_w_up[i].astype(BF16), moe_w_down[i].astype(BF16), fin)

    zero_state = jnp.zeros((b, N_S5_BLOCKS, SLAB, LANES), F32)
    ctx_s = ctx
    for li in range(depth):
        last = li == depth - 1
        mod = _ada(cc, ada_w.astype(F32), li, ada_b[li].astype(F32))
        lat = [m.reshape(b, 1, d) for m in jnp.split(mod[:b], N_MOD, axis=-1)]
        cm = [jnp.broadcast_to(m.reshape(1, 1, d), (b, 1, d)) for m in jnp.split(mod[b:b + 1], N_MOD, axis=-1)]
        s5m = _s5_matrices(s5_a_re[li], s5_a_im[li], s5_log_dt[li], s5_b_re[li], s5_b_im[li],
                           s5_c_re[li], s5_c_im[li])

        parts, hf_c, hb_c = mix(li, ctx_s, cm[0], cm[1], False, zero_state, zero_state, s5m, last)
        if not last:
            ctx_s = out_proj(li, parts, ctx_s, cm[2])
            ctx_s = channel_mix(li, ctx_s, cm[3], cm[4], cm[5], None)

        parts, _, _ = mix(li, x, lat[0], lat[1], True, hf_c, hb_c, s5m, False)
        x = out_proj(li, parts, x, lat[2])
        x = channel_mix(li, x, lat[3], lat[4], lat[5], final_g.reshape(1, d) if last else None)

    if depth == 0:
        raise ValueError("depth must be positive")
    return x
```

```python
import functools
import math

import numpy as np
import jax
import jax.numpy as jnp
from jax import lax
from jax.experimental import pallas as pl
from jax.experimental.pallas import tpu as pltpu

F32 = jnp.float32
BF16 = jnp.bfloat16
HI = lax.Precision.HIGHEST

LANES = 128
GRID_ROW = 64
FNET_HEADS, FNET_HEAD_DIM = 4, 64
D_FNET = FNET_HEADS * FNET_HEAD_DIM
D_HYENA = 384
HYENA_ORDER = 2
HYENA_EMB = 33
HYENA_BANDS = (HYENA_EMB - 1) // 2
HYENA_FAST_DECAY_PCT, HYENA_SLOW_DECAY_PCT, HYENA_DECAY_TARGET = 0.3, 1.5, 0.01
S5_GROUP, S5_GROUPS, S5_STATE = 16, 24, 64
D_S5 = S5_GROUP * S5_GROUPS
HY_OFF = D_FNET
S5_OFF = D_FNET + (HYENA_ORDER + 1) * D_HYENA
N_MOD = 6
TOP_K = 2
RMS_EPS = 1e-6
S5_CHUNK = 16
GROUPS_PER_BLOCK = LANES // S5_GROUP
N_S5_BLOCKS = D_S5 // LANES
STATE_HALF = GROUPS_PER_BLOCK * S5_STATE
NEG_BIG = float(np.finfo(np.float32).min)
VMEM_LIMIT = 56 * 1024 * 1024


def _cp(sem, vmem=None):
    return pltpu.CompilerParams(dimension_semantics=sem, vmem_limit_bytes=vmem)


def _dot(a, b):
    return jnp.dot(a, b, preferred_element_type=F32)


def _split_bf16(a):
    hi = a.astype(BF16)
    lo = (a - hi.astype(F32)).astype(BF16)
    return hi, lo


def _sigmoid(x):
    return 1.0 / (1.0 + jnp.exp(-x))


def _rms_mod(x, g, sh, sc):
    ms = jnp.mean(x * x, axis=-1, keepdims=True)
    return (x * lax.rsqrt(ms + RMS_EPS) * g) * (1.0 + sc) + sh


def _ada_kernel(c_ref, w_ref, b_ref, o_ref):
    c = c_ref[...]
    s = c * _sigmoid(c)
    s_hi, s_lo = _split_bf16(s)
    w_hi, w_lo = _split_bf16(w_ref[...])
    o_ref[...] = _dot(s_hi, w_hi) + _dot(s_hi, w_lo) + _dot(s_lo, w_hi) + b_ref[...]


def _ada(cc, w_all, li, b):
    r, d = cc.shape
    n = w_all.shape[2]
    tn = 512
    return pl.pallas_call(
        _ada_kernel,
        out_shape=jax.ShapeDtypeStruct((r, n), F32),
        grid=(n // tn,),
        in_specs=[pl.BlockSpec((r, d), lambda j: (0, 0)),
                  pl.BlockSpec((None, d, tn), lambda j: (li, 0, j)),
                  pl.BlockSpec((1, tn), lambda j: (0, j))],
        out_specs=pl.BlockSpec((r, tn), lambda j: (0, j)),
        compiler_params=_cp(("parallel",)),
        name="ada_mod",
    )(cc, w_all, b.reshape(1, n))


def _expand_kernel(a1_ref, a2_ref, b1_ref, b2_ref, o_ref, *, nh):
    a2 = a2_ref[...]
    b2 = b2_ref[...]
    for h in range(nh):
        blk = a1_ref[:, h:h + 1] * a2 + b1_ref[:, h:h + 1] * b2
        o_ref[:, h * LANES:(h + 1) * LANES] = blk.astype(o_ref.dtype)


def _expand(a1, a2, b1, b2):
    r, nh = a1.shape
    tr = min(r, 256)
    return pl.pallas_call(
        functools.partial(_expand_kernel, nh=nh),
        out_shape=jax.ShapeDtypeStruct((r, nh * LANES), BF16),
        grid=(r // tr,),
        in_specs=[pl.BlockSpec((tr, nh), lambda i: (i, 0)),
                  pl.BlockSpec((tr, LANES), lambda i: (i, 0)),
                  pl.BlockSpec((tr, nh), lambda i: (i, 0)),
                  pl.BlockSpec((tr, LANES), lambda i: (i, 0))],
        out_specs=pl.BlockSpec((tr, nh * LANES), lambda i: (i, 0)),
        compiler_params=_cp(("parallel",)),
        name="dft_expand",
    )(a1, a2, b1, b2)


def _fnet_dft(l):
    h = l // 2
    nh = h // LANES
    period = 2 * l
    k = np.arange(l, dtype=np.int64)[:, None]
    qa = (k * (2 * LANES) * np.arange(nh, dtype=np.int64)[None, :]) % period
    qb = (k * (2 * np.arange(LANES, dtype=np.int64)[None, :] + 1)) % period
    aa = 2.0 * np.pi * qa / period
    ab = 2.0 * np.pi * qb / period
    f = lambda t: jnp.asarray(t, F32)
    c1, s1, c2, s2 = np.cos(aa), np.sin(aa), np.cos(ab), np.sin(ab)
    phase = np.zeros((l, LANES), np.float32)
    phase[:, 0] = np.cos(np.pi * k[:, 0] / l)
    phase[:, 1] = np.sin(np.pi * k[:, 0] / l)
    return _expand(f(c1), f(c2), f(-s1), f(s2)), _expand(f(-s1), f(c2), f(-c1), f(s2)), jnp.asarray(phase)


def _hyena_dft(l):
    h = l // 2
    nh = h // LANES
    period = 4 * l
    kp = np.arange(h, dtype=np.int64)[:, None]
    f = lambda t: jnp.asarray(t, F32)
    out = {}
    for name, k in (("e", 2 * kp), ("o", 2 * kp + 1)):
        qa = (k * (2 * LANES) * np.arange(nh, dtype=np.int64)[None, :]) % period
        qb = (k * (2 * np.arange(LANES, dtype=np.int64)[None, :] + 1)) % period
        aa = 2.0 * np.pi * qa / period
        ab = 2.0 * np.pi * qb / period
        c1, s1, c2, s2 = np.cos(aa), np.sin(aa), np.cos(ab), np.sin(ab)
        a1, a2, b1 = -s1, c2.copy(), -c1
        if name == "e":
            a1[0, :] = 1.0
            a2[0, :] = (-1.0) ** np.arange(LANES)
            b1[0, :] = 0.0
        out["c" + name] = _expand(f(c1), f(c2), f(-s1), f(s2))
        out["s" + name] = _expand(f(a1), f(a2), f(b1), f(s2))
    return out


def _flip_matrix(n):
    return jnp.asarray(np.eye(n, dtype=np.float32)[::-1], BF16)


def _flip_rows(v, j):
    sb = j.shape[0]
    nblk = v.shape[0] // sb
    v_hi, v_lo = _split_bf16(v)
    parts = []
    for s in range(nblk):
        r = slice((nblk - 1 - s) * sb, (nblk - s) * sb)
        parts.append(_dot(j, v_hi[r]) + _dot(j, v_lo[r]))
    return parts[0] if nblk == 1 else jnp.concatenate(parts, axis=0)


def _inproj_kernel(x_ref, g_ref, sh_ref, sc_ref, w_ref, cw_ref, cb_ref, j_ref, *out_refs, tm, nt, grid_mode, full):
    h = _rms_mod(x_ref[...], g_ref[...], sh_ref[...], sc_ref[...])
    acc = _dot(h.astype(BF16), w_ref[...])
    if not full:
        out_refs[0][...] = acc
        return
    pf_ref, qv_ref, qx1_ref, qx2_ref, ps_ref = out_refs
    ps_ref[...] = acc[:, S5_OFF:]
    hy = acc[:, HY_OFF:S5_OFF]
    row = lax.broadcasted_iota(jnp.int32, (tm, 1), 0)
    if grid_mode:
        first = (row % GRID_ROW) == 0
        last = (row % GRID_ROW) == GRID_ROW - 1
    else:
        first = row == 0
        last = row == tm - 1
    prev = jnp.where(first, 0.0, pltpu.roll(hy, 1, 0))
    nxt = jnp.where(last, 0.0, pltpu.roll(hy, tm - 1, 0))
    q = prev * cw_ref[0:1, :] + hy * cw_ref[1:2, :] + nxt * cw_ref[2:3, :] + cb_ref[...]
    q = jnp.concatenate([acc[:, :HY_OFF], q], axis=1)
    folded = (pf_ref, qv_ref, qx1_ref, qx2_ref)
    edges = (0, D_FNET, D_FNET + D_HYENA, D_FNET + 2 * D_HYENA, D_FNET + 3 * D_HYENA)
    cols = [slice(edges[n], edges[n + 1]) for n in range(len(folded))]
    if nt == 1:
        half = tm // 2
        q_hi = _flip_rows(q[half:], j_ref[...])
        for r, c in zip(folded, cols):
            r[0] = q[:half, c]
            r[1] = q_hi[:, c]
        return
    i = pl.program_id(1)

    @pl.when(i < nt // 2)
    def _():
        for r, c in zip(folded, cols):
            r[...] = q[:, c]

    @pl.when(i >= nt // 2)
    def _():
        qf = _flip_rows(q, j_ref[...])
        for r, c in zip(folded, cols):
            r[...] = qf[:, c]


def _fold_spec(tm, nt, width):
    if nt == 1:
        return pl.BlockSpec((None, 2, tm // 2, width), lambda bi, i: (bi, 0, 0, 0))
    hn = nt // 2
    return pl.BlockSpec((None, None, tm, width),
                        lambda bi, i: (bi, i // hn, jnp.where(i < hn, i, nt - 1 - i), 0))


def _inproj(x, g, sh, sc, w, cw, cb, *, grid_mode, full):
    b, l, d = x.shape
    n = w.shape[1]
    tm = min(l, 512)
    nt = l // tm
    if not grid_mode:
        assert tm == l
    else:
        assert tm % GRID_ROW == 0
    assert nt == 1 or nt % 2 == 0
    row = lambda bi, i: (bi, i, 0)
    vec = lambda bi, i: (bi, 0, 0)
    const = lambda bi, i: (0, 0)
    flip = _flip_matrix(min(256, tm // 2 if nt == 1 else tm))
    plain = lambda wd: (jax.ShapeDtypeStruct((b, l, wd), F32), pl.BlockSpec((None, tm, wd), row))
    fold = lambda wd: (jax.ShapeDtypeStruct((b, 2, l // 2, wd), F32), _fold_spec(tm, nt, wd))
    outs = ([fold(D_FNET), fold(D_HYENA), fold(D_HYENA), fold(D_HYENA), plain(D_S5)] if full else [plain(n)])
    return pl.pallas_call(
        functools.partial(_inproj_kernel, tm=tm, nt=nt, grid_mode=grid_mode, full=full),
        out_shape=tuple(o[0] for o in outs),
        grid=(b, nt),
        in_specs=[pl.BlockSpec((None, tm, d), row),
                  pl.BlockSpec((1, d), const),
                  pl.BlockSpec((None, 1, d), vec),
                  pl.BlockSpec((None, 1, d), vec),
                  pl.BlockSpec((d, n), const),
                  pl.BlockSpec(cw.shape, const),
                  pl.BlockSpec(cb.shape, const),
                  pl.BlockSpec(flip.shape, const)],
        out_specs=tuple(o[1] for o in outs),
        compiler_params=_cp(("parallel", "parallel"), VMEM_LIMIT),
        name="inproj",
    )(x, g, sh, sc, w, cw, cb, flip)


def _fnet_kernel(x_ref, mc_ref, ms_ref, ph_ref, a_ref, b_ref, o_ref, accr, acci, *, nb):
    k = pl.program_id(2)

    @pl.when(k == 0)
    def _():
        accr[...] = jnp.zeros_like(accr)
        acci[...] = jnp.zeros_like(acci)

    mc = mc_ref[...]
    ms = ms_ref[...]
    for bi in range(nb):
        lo = x_ref[bi, 0]
        hi = x_ref[bi, 1]
        accr[bi] += _dot(mc, (lo + hi).astype(BF16))
        acci[bi] += _dot(ms, (lo - hi).astype(BF16))

    @pl.when(k == pl.num_programs(2) - 1)
    def _():
        ck = ph_ref[:, 0:1]
        sk = ph_ref[:, 1:2]
        for bi in range(nb):
            zr, zi = accr[bi], acci[bi]
            cos_part = ck * zr - sk * zi
            sin_part = -(sk * zr + ck * zi)
            o_ref[bi] = (_dot(cos_part.astype(BF16), a_ref[...])
                         + _dot(sin_part.astype(BF16), b_ref[...]))


def _fnet(pf, mc, ms, phase, amat, bmat):
    b, _, h, c = pf.shape
    l = 2 * h
    nb = min(b, 2)
    t = min(l, 1024)
    tk = min(h, 1024)
    return pl.pallas_call(
        functools.partial(_fnet_kernel, nb=nb),
        out_shape=jax.ShapeDtypeStruct((b, l, c), F32),
        grid=(b // nb, l // t, h // tk),
        in_specs=[pl.BlockSpec((nb, 2, tk, c), lambda bb, i, k: (bb, 0, k, 0)),
                  pl.BlockSpec((t, tk), lambda bb, i, k: (i, k)),
                  pl.BlockSpec((t, tk), lambda bb, i, k: (i, k)),
                  pl.BlockSpec((t, LANES), lambda bb, i, k: (i, 0)),
                  pl.BlockSpec((c, c), lambda bb, i, k: (0, 0)),
                  pl.BlockSpec((c, c), lambda bb, i, k: (0, 0))],
        out_specs=pl.BlockSpec((nb, t, c), lambda bb, i, k: (bb, i, 0)),
        scratch_shapes=[pltpu.VMEM((nb, t, c), F32), pltpu.VMEM((nb, t, c), F32)],
        compiler_params=_cp(("parallel", "parallel", "arbitrary"), VMEM_LIMIT),
        name="fnet_dft",
    )(pf, mc, ms, phase, amat, bmat)


def _hy_fwd_kernel(z_ref, ce_ref, se_ref, co_ref, so_ref, *rest, nb, filtered):
    if filtered:
        kf_ref, pr_ref, pi_ref, acc = rest
    else:
        pr_ref, pi_ref, acc = rest
    k = pl.program_id(2)

    @pl.when(k == 0)
    def _():
        acc[...] = jnp.zeros_like(acc)

    c = z_ref.shape[-1]
    ev = jnp.concatenate([(z_ref[bi, 0] + z_ref[bi, 1]).astype(BF16) for bi in range(nb)], axis=1)
    od = jnp.concatenate([(z_ref[bi, 0] - z_ref[bi, 1]).astype(BF16) for bi in range(nb)], axis=1)
    acc[0] += _dot(ce_ref[...], ev)
    acc[1] += _dot(se_ref[...], od)
    acc[2] += _dot(co_ref[...], od)
    acc[3] += _dot(so_ref[...], ev)

    @pl.when(k == pl.num_programs(2) - 1)
    def _():
        for bi in range(nb):
            cols = slice(bi * c, (bi + 1) * c)
            for p in range(2):
                zr, zi = acc[2 * p, :, cols], acc[2 * p + 1, :, cols]
                if filtered:
                    kr, ki, kd = kf_ref[p, 0], kf_ref[p, 1], kf_ref[p, 2]
                    zr, zi = zr * kr - zi * ki, zr * ki + zi * kd
                pr_ref[bi, p] = zr.astype(pr_ref.dtype)
                pi_ref[bi, p] = zi.astype(pi_ref.dtype)


def _hy_tiles(h):
    return min(h, 1024), min(h, 512)


def _once(block_shape, index_map):
    return pl.BlockSpec(block_shape, index_map, pipeline_mode=pl.Buffered(1))


def _hy_fwd(z, mats, kf, out_dtype=BF16):
    b, _, h, c = z.shape
    nb = min(b, 2)
    ti, tk = _hy_tiles(h)
    mspec = pl.BlockSpec((ti, tk), lambda i, bb, k: (i, k))
    ospec = pl.BlockSpec((nb, 2, ti, c), lambda i, bb, k: (bb, 0, i, 0))
    filtered = kf is not None
    extra_specs = [_once((2, 3, ti, c), lambda i, bb, k: (0, 0, i, 0))] if filtered else []
    extra_args = (kf,) if filtered else ()
    return pl.pallas_call(
        functools.partial(_hy_fwd_kernel, nb=nb, filtered=filtered),
        out_shape=(jax.ShapeDtypeStruct((b, 2, h, c), out_dtype), jax.ShapeDtypeStruct((b, 2, h, c), out_dtype)),
        grid=(h // ti, b // nb, h // tk),
        in_specs=[pl.BlockSpec((nb, 2, tk, c), lambda i, bb, k: (bb, 0, k, 0)),
                  mspec, mspec, mspec, mspec] + extra_specs,
        out_specs=(ospec, ospec),
        scratch_shapes=[pltpu.VMEM((4, ti, nb * c), F32)],
        compiler_params=_cp(("parallel", "parallel", "arbitrary"), VMEM_LIMIT),
        name="hyena_fwd_dft",
    )(z, mats["ce"], mats["se"], mats["co"], mats["so"], *extra_args)


def _hy_inv_kernel(pr_ref, pi_ref, ce_ref, se_ref, co_ref, so_ref, xg_ref, zin_ref, bias_ref, o_ref, acc, *, nb):
    k = pl.program_id(2)

    @pl.when(k == 0)
    def _():
        acc[...] = jnp.zeros_like(acc)

    c = xg_ref.shape[-1]
    side = lambda ref, p: jnp.concatenate([ref[bi, p] for bi in range(nb)], axis=1)
    acc[0] += _dot(ce_ref[...], side(pr_ref, 0)) + _dot(so_ref[...], side(pi_ref, 1))
    acc[1] += _dot(se_ref[...], side(pi_ref, 0)) + _dot(co_ref[...], side(pr_ref, 1))

    @pl.when(k == pl.num_programs(2) - 1)
    def _():
        bias = bias_ref[...]
        for bi in range(nb):
            cols = slice(bi * c, (bi + 1) * c)
            u, v = acc[0, :, cols], acc[1, :, cols]
            o_ref[bi, 0] = xg_ref[bi, 0] * (u + v + zin_ref[bi, 0] * bias)
            o_ref[bi, 1] = xg_ref[bi, 1] * (u - v + zin_ref[bi, 1] * bias)


def _hy_inv(pr, pi, mats_t, xg, zin, bias):
    b, _, h, c = xg.shape
    nb = min(b, 2)
    ti, tk = min(h, 512), min(h, 1024)
    pspec = pl.BlockSpec((nb, 2, tk, c), lambda bb, i, k: (bb, 0, k, 0))
    mspec = pl.BlockSpec((ti, tk), lambda bb, i, k: (i, k))
    ospec = pl.BlockSpec((nb, 2, ti, c), lambda bb, i, k: (bb, 0, i, 0))
    return pl.pallas_call(
        functools.partial(_hy_inv_kernel, nb=nb),
        out_shape=jax.ShapeDtypeStruct((b, 2, h, c), F32),
        grid=(b // nb, h // ti, h // tk),
        in_specs=[pspec, pspec, mspec, mspec, mspec, mspec, ospec, ospec,
                  pl.BlockSpec((1, c), lambda bb, i, k: (0, 0))],
        out_specs=ospec,
        scratch_shapes=[pltpu.VMEM((2, ti, nb * c), F32)],
        compiler_params=_cp(("parallel", "parallel", "arbitrary"), VMEM_LIMIT),
        name="hyena_inv_dft",
    )(pr, pi, mats_t["ce"], mats_t["se"], mats_t["co"], mats_t["so"], xg, zin, bias.reshape(1, c))


def _s5_kernel(u_ref, s0f_ref, s0b_ref, mi_ref, mof_ref, mob_ref, mnf_ref, mnb_ref, atf_ref, atb_ref,
               d_ref, y_ref, hf_ref, hb_ref, x_scr, sf_scr, sb_scr, hinf_scr, hinb_scr, *, nc, t_chunk):
    for s in range(t_chunk):
        x_scr[:, s * LANES:(s + 1) * LANES] = u_ref[pl.ds(s, nc, stride=t_chunk), :].astype(BF16)
    x = x_scr[...]
    _to_slabs(sf_scr, _dot(x, mof_ref[...]), nc)
    _to_slabs(sb_scr, _dot(x, mob_ref[...]), nc)
    cf1, cf2 = atf_ref[0], atf_ref[1]
    cb1, cb2 = atb_ref[0], atb_ref[1]

    def body(c, carry):
        hf, hb = carry
        rf = pl.ds(pl.multiple_of(c * SLAB, SLAB), SLAB)
        rb = pl.ds(pl.multiple_of((nc - 1 - c) * SLAB, SLAB), SLAB)
        hinf_scr[rf, :] = hf
        hinb_scr[rb, :] = hb
        hf = cf1 * hf + cf2 * pltpu.roll(hf, SLAB // 2, 0) + sf_scr[rf, :]
        hb = cb1 * hb + cb2 * pltpu.roll(hb, SLAB // 2, 0) + sb_scr[rb, :]
        return hf, hb

    hf, hb = lax.fori_loop(0, nc, body, (s0f_ref[...], s0b_ref[...]), unroll=4)
    hf_ref[...] = hf
    hb_ref[...] = hb
    y = (_dot(x, mi_ref[...])
         + _dot(_from_slabs(hinf_scr, nc).astype(BF16), mnf_ref[...])
         + _dot(_from_slabs(hinb_scr, nc).astype(BF16), mnb_ref[...]))
    d = d_ref[...]
    for t in range(t_chunk):
        rows = pl.ds(t, nc, stride=t_chunk)
        y_ref[rows, :] = y[:, t * LANES:(t + 1) * LANES] + d * u_ref[rows, :]


def _s5(u, s0f, s0b, mats, d_skip):
    b, l, _ = u.shape
    tc = S5_CHUNK
    nc = l // tc
    kin = tc * LANES
    ns = 2 * STATE_HALF
    mi, mof, mob, mnf, mnb, atf, atb = mats
    st_spec = pl.BlockSpec((None, None, SLAB, LANES), lambda j, bi: (bi, j, 0, 0))
    mode = dict(pipeline_mode=pl.Buffered(1)) if l > 1024 else {}
    w3 = lambda s1, s2: pl.BlockSpec((None, s1, s2), lambda j, bi: (j, 0, 0), **mode)
    coef = pl.BlockSpec((None, 2, SLAB, LANES), lambda j, bi: (j, 0, 0, 0))
    u_spec = pl.BlockSpec((None, l, LANES), lambda j, bi: (bi, 0, j))
    state = jax.ShapeDtypeStruct((b, N_S5_BLOCKS, SLAB, LANES), F32)
    return pl.pallas_call(
        functools.partial(_s5_kernel, nc=nc, t_chunk=tc),
        out_shape=(jax.ShapeDtypeStruct((b, l, D_S5), F32), state, state),
        grid=(N_S5_BLOCKS, b),
        in_specs=[u_spec, st_spec, st_spec,
                  w3(kin, kin), w3(kin, ns), w3(kin, ns), w3(ns, kin), w3(ns, kin),
                  coef, coef,
                  pl.BlockSpec((1, LANES), lambda j, bi: (0, j))],
        out_specs=(u_spec, st_spec, st_spec),
        scratch_shapes=[pltpu.VMEM((nc, kin), BF16)] + [pltpu.VMEM((nc * SLAB, LANES), F32)] * 4,
        compiler_params=_cp(("parallel", "parallel"), VMEM_LIMIT),
        name="s5_scan",
    )(u, s0f, s0b, mi, mof, mob, mnf, mnb, atf, atb, d_skip.reshape(1, D_S5))


def _s5_matrices(a_re, a_im, log_dt, b_re, b_im, c_re, c_im):
    tc = S5_CHUNK
    g, p, hd = S5_GROUPS, S5_STATE, S5_GROUP
    lam = lax.complex(a_re.astype(F32), a_im.astype(F32))
    dt = jnp.exp(log_dt.astype(F32))[..., None]
    a_bar = jnp.exp(lam * dt)
    b_bar = ((a_bar - 1.0) / lam)[..., None] * lax.complex(b_re.astype(F32), b_im.astype(F32))
    cc = lax.complex(c_re.astype(F32), c_im.astype(F32))
    ks = jnp.arange(tc + 1, dtype=F32)
    pw = jnp.exp((lam * dt)[:, :, None, :] * ks[None, None, :, None])
    kern = jnp.real(jnp.einsum('dgip,dgkp,dgpj->dgkij', cc, pw[:, :, :tc], b_bar, precision=HI))
    s_idx = jnp.arange(tc)[:, None]
    t_idx = jnp.arange(tc)[None, :]
    lag = t_idx - s_idx
    kf = jnp.where((lag >= 0)[None, :, :, None, None], kern[0][:, jnp.clip(lag, 0, tc - 1)], 0.0)
    kb = jnp.where((lag <= 0)[None, :, :, None, None], kern[1][:, jnp.clip(-lag, 0, tc - 1)], 0.0)
    ktot = kf + kb
    def place(slab, inner, row_group_width):
        width = slab.shape[-1]
        src = np.arange(width)
        e = np.zeros((width, GROUPS_PER_BLOCK * width), np.float32)
        for hh in range(GROUPS_PER_BLOCK):
            e[src, (src // inner) * GROUPS_PER_BLOCK * inner + hh * inner + src % inner] = 1.0
        wide = jnp.einsum('Jrk,kc->Jrc', slab, jnp.asarray(e), precision=HI)
        rows = lax.broadcasted_iota(jnp.int32, wide.shape[1:], 0)
        cols = lax.broadcasted_iota(jnp.int32, wide.shape[1:], 1)
        keep = (rows // row_group_width) % GROUPS_PER_BLOCK == (cols // inner) % GROUPS_PER_BLOCK
        return jnp.where(keep[None], wide, 0.0)

    kt = ktot.reshape(N_S5_BLOCKS, GROUPS_PER_BLOCK, tc, tc, hd, hd)
    kt = jnp.transpose(kt, (0, 2, 1, 5, 3, 4))
    mi = place(kt.reshape(N_S5_BLOCKS, tc * LANES, tc * hd), hd, hd)

    def parts(z):
        return jnp.stack([jnp.real(z), jnp.imag(z)], axis=0)

    def out_mat(d, powers):
        z = powers[:, :, :, None] * b_bar[d][:, None, :, :]
        z = parts(z).reshape(2, N_S5_BLOCKS, GROUPS_PER_BLOCK, tc, p, hd)
        z = jnp.transpose(z, (1, 3, 2, 5, 0, 4))
        return place(z.reshape(N_S5_BLOCKS, tc * LANES, 2 * p), p, hd)

    def in_mat(d, powers):
        z = cc[d][:, None, :, :] * powers[:, :, None, :]
        z = jnp.stack([jnp.real(z), -jnp.imag(z)], axis=0)
        z = z.reshape(2, N_S5_BLOCKS, GROUPS_PER_BLOCK, tc, hd, p)
        z = jnp.transpose(z, (1, 0, 2, 5, 3, 4))
        return place(z.reshape(N_S5_BLOCKS, 2 * STATE_HALF, tc * hd), hd, p)

    rev = jnp.arange(tc - 1, -1, -1)
    mof = out_mat(0, pw[0][:, rev])
    mob = out_mat(1, pw[1][:, :tc])
    mnf = in_mat(0, pw[0][:, 1:tc + 1])
    mnb = in_mat(1, pw[1][:, tc - jnp.arange(tc)])

    def step(d):
        ar = jnp.real(pw[d][:, tc]).reshape(N_S5_BLOCKS, STATE_HALF)
        ai = jnp.imag(pw[d][:, tc]).reshape(N_S5_BLOCKS, STATE_HALF)
        c1 = jnp.concatenate([ar, ar], axis=1)
        c2 = jnp.concatenate([-ai, ai], axis=1)
        return jnp.stack([c1, c2], axis=1).reshape(N_S5_BLOCKS, 2, SLAB, LANES)

    bf = lambda m: m.astype(BF16)
    return bf(mi), bf(mof), bf(mob), bf(mnf), bf(mnb), step(0), step(1)


def _outproj_kernel(yf_ref, yh_ref, ys_ref, x_ref, gate_ref, w1_ref, w2_ref, w3_ref, gw_ref, gb_ref, j_ref, o_ref,
                    *, nt):
    y = ys_ref[...]
    y = 0.5 * y * (1.0 + jnp.tanh(math.sqrt(2.0 / math.pi) * (y + 0.044715 * (y * y * y))))
    y = y * _sigmoid(_dot(y.astype(BF16), gw_ref[...]) + gb_ref[...])
    acc = _dot(yf_ref[...].astype(BF16), w1_ref[...]) + _dot(y.astype(BF16), w3_ref[...])

    def finish(yh):
        o_ref[...] = x_ref[...] + gate_ref[...] * (acc + _dot(yh.astype(BF16), w2_ref[...]))

    if nt == 1:
        finish(jnp.concatenate([yh_ref[0], _flip_rows(yh_ref[1], j_ref[...])], axis=0))
        return
    i = pl.program_id(1)

    @pl.when(i < nt // 2)
    def _():
        finish(yh_ref[...])

    @pl.when(i >= nt // 2)
    def _():
        finish(_flip_rows(yh_ref[...], j_ref[...]))


def _outproj(yf, yh, ys, x, gate, w1, w2, w3, gw, gb):
    b, l, d = x.shape
    tm = min(l, 512)
    nt = l // tm
    row = lambda bi, i: (bi, i, 0)
    const = lambda bi, i: (0, 0)
    full = lambda a: pl.BlockSpec(a.shape, const)
    flip = _flip_matrix(min(256, tm // 2 if nt == 1 else tm))
    return pl.pallas_call(
        functools.partial(_outproj_kernel, nt=nt),
        out_shape=jax.ShapeDtypeStruct((b, l, d), F32),
        grid=(b, nt),
        in_specs=[pl.BlockSpec((None, tm, yf.shape[2]), row),
                  _fold_spec(tm, nt, yh.shape[3]),
                  pl.BlockSpec((None, tm, ys.shape[2]), row),
                  pl.BlockSpec((None, tm, d), row),
                  pl.BlockSpec((None, 1, d), lambda bi, i: (bi, 0, 0)),
                  full(w1), full(w2), full(w3), full(gw), full(gb), full(flip)],
        out_specs=pl.BlockSpec((None, tm, d), row),
        compiler_params=_cp(("parallel", "parallel"), VMEM_LIMIT),
        name="outproj",
    )(yf, yh, ys, x, gate, w1, w2, w3, gw, gb, flip)


def _ffn_kernel(x_ref, g_ref, sh_ref, sc_ref, gate_ref, wg_ref, wu_ref, wd_ref, fg_ref, o_ref, xn_scr, acc_scr,
                *, final_norm):
    f = pl.program_id(2)

    @pl.when(f == 0)
    def _():
        xn_scr[...] = _rms_mod(x_ref[...], g_ref[...], sh_ref[...], sc_ref[...]).astype(BF16)
        acc_scr[...] = jnp.zeros_like(acc_scr)

    xn = xn_scr[...]
    gt = _dot(xn, wg_ref[...])
    up = _dot(xn, wu_ref[...])
    acc_scr[...] += _dot((gt * _sigmoid(gt) * up).astype(BF16), wd_ref[...])

    @pl.when(f == pl.num_programs(2) - 1)
    def _():
        y = x_ref[...] + gate_ref[...] * acc_scr[...]
        if final_norm:
            ms = jnp.mean(y * y, axis=-1, keepdims=True)
            y = y * lax.rsqrt(ms + RMS_EPS) * fg_ref[...]
        o_ref[...] = y


def _ffn(x, g, sh, sc, gate, wg, wu, wd, final_g, *, tf):
    b, l, d = x.shape
    ff = wg.shape[1]
    final_norm = final_g is not None
    tm = min(l, 1024)
    fg = final_g if final_norm else jnp.ones((1, d), F32)
    row = lambda bi, i, f: (bi, i, 0)
    vec = lambda bi, i, f: (bi, 0, 0)
    const = lambda bi, i, f: (0, 0)
    return pl.pallas_call(
        functools.partial(_ffn_kernel, final_norm=final_norm),
        out_shape=jax.ShapeDtypeStruct((b, l, d), F32),
        grid=(b, l // tm, ff // tf),
        in_specs=[pl.BlockSpec((None, tm, d), row),
                  pl.BlockSpec((1, d), const),
                  pl.BlockSpec((None, 1, d), vec),
                  pl.BlockSpec((None, 1, d), vec),
                  pl.BlockSpec((None, 1, d), vec),
                  pl.BlockSpec((d, tf), lambda bi, i, f: (0, f)),
                  pl.BlockSpec((d, tf), lambda bi, i, f: (0, f)),
                  pl.BlockSpec((tf, d), lambda bi, i, f: (f, 0)),
                  pl.BlockSpec((1, d), const)],
        out_specs=pl.BlockSpec((None, tm, d), row),
        scratch_shapes=[pltpu.VMEM((tm, d), BF16), pltpu.VMEM((tm, d), F32)],
        compiler_params=_cp(("parallel", "parallel", "arbitrary"), VMEM_LIMIT),
        name="dense_ffn",
    )(x, g, sh, sc, gate, wg, wu, wd, fg)


SLAB = 8
ROUTE_TM = 512
GROUP_TM = 1024


def _to_slabs(ref, val, n):
    for s in range(SLAB):
        ref[pl.ds(s, n, stride=SLAB), :] = val[:, s * LANES:(s + 1) * LANES]


def _from_slabs(ref, n):
    return jnp.concatenate([ref[pl.ds(s, n, stride=SLAB), :] for s in range(SLAB)], axis=1)


def _route_kernel(x_ref, g_ref, sh_ref, sc_ref, rwh_ref, rwl_ref, rb_ref, tri_ref,
                  xs_ref, meta_ref, gate_ref, cnt_ref, carry_scr, *, tm):
    step = pl.program_id(0) * pl.num_programs(1) + pl.program_id(1)

    @pl.when(step == 0)
    def _():
        carry_scr[...] = jnp.zeros_like(carry_scr)

    h = _rms_mod(x_ref[...], g_ref[...], sh_ref[...], sc_ref[...])
    _to_slabs(xs_ref, h, tm)
    h_hi, h_lo = _split_bf16(h)
    rwh = rwh_ref[...]
    lg = _dot(h_hi, rwh) + _dot(h_hi, rwl_ref[...]) + _dot(h_lo, rwh) + rb_ref[...]
    lane = lax.broadcasted_iota(jnp.int32, lg.shape, 1)
    m1 = jnp.max(lg, axis=-1, keepdims=True)
    i1 = jnp.min(jnp.where(lg == m1, lane, LANES), axis=-1, keepdims=True)
    lg2 = jnp.where(lane == i1, NEG_BIG, lg)
    m2 = jnp.max(lg2, axis=-1, keepdims=True)
    i2 = jnp.min(jnp.where(lg2 == m2, lane, LANES), axis=-1, keepdims=True)
    ex = jnp.exp(m2 - m1)
    den = 1.0 + ex
    gate_ref[...] = jnp.where(lane == 0, 1.0 / den, jnp.where(lane == 1, ex / den, 0.0))
    sel1 = lane == i1
    sel2 = lane == i2
    tri = tri_ref[...]
    p1 = _dot(tri, sel1.astype(BF16))
    p2 = _dot(tri, sel2.astype(BF16))
    tot1 = p1[tm - 1:tm, :]
    tot2 = p2[tm - 1:tm, :]
    carry = carry_scr[...]
    r1 = jnp.sum(jnp.where(sel1, carry + p1 - 1.0, 0.0), axis=-1, keepdims=True)
    r2 = jnp.sum(jnp.where(sel2, carry + tot1 + p2 - 1.0, 0.0), axis=-1, keepdims=True)
    carry = carry + tot1 + tot2
    carry_scr[...] = carry
    cnt_ref[...] = carry.astype(jnp.int32)
    meta_ref[...] = jnp.where(lane == 0, i1, jnp.where(lane == 1, i2, jnp.where(
        lane == 2, r1.astype(jnp.int32), jnp.where(lane == 3, r2.astype(jnp.int32), 0))))


def _route(x, g, sh, sc, rwh, rwl, rb):
    b, l, d = x.shape
    tm = min(l, ROUTE_TM)
    n = b * l
    nl = l // tm
    tri = jnp.tril(jnp.ones((tm, tm), F32)).astype(BF16)
    row = lambda bi, i: (bi, i, 0)
    vec = lambda bi, i: (bi, 0, 0)
    const = lambda bi, i: (0, 0)
    flat = lambda bi, i: (bi * nl + i, 0)
    return pl.pallas_call(
        functools.partial(_route_kernel, tm=tm),
        out_shape=(jax.ShapeDtypeStruct((n * SLAB, LANES), F32),
                   jax.ShapeDtypeStruct((n, LANES), jnp.int32),
                   jax.ShapeDtypeStruct((n, LANES), F32),
                   jax.ShapeDtypeStruct((1, LANES), jnp.int32)),
        grid=(b, nl),
        in_specs=[pl.BlockSpec((None, tm, d), row),
                  pl.BlockSpec((1, d), const),
                  pl.BlockSpec((None, 1, d), vec),
                  pl.BlockSpec((None, 1, d), vec),
                  pl.BlockSpec((d, LANES), const),
                  pl.BlockSpec((d, LANES), const),
                  pl.BlockSpec((1, LANES), const),
                  pl.BlockSpec((tm, tm), const)],
        out_specs=(pl.BlockSpec((tm * SLAB, LANES), flat),
                   pl.BlockSpec((tm, LANES), flat),
                   pl.BlockSpec((tm, LANES), flat),
                   pl.BlockSpec((1, LANES), const)),
        scratch_shapes=[pltpu.VMEM((1, LANES), F32)],
        compiler_params=_cp(("arbitrary", "arbitrary"), VMEM_LIMIT),
        name="moe_route",
    )(x, g, sh, sc, rwh, rwl, rb, tri)


def _dispatch_kernel(pos_hbm, xs_ref, init_hbm, out_hbm, idx_smem, isem, sem, *, tm):
    del init_hbm
    step = pl.program_id(0)
    icopy = pltpu.make_async_copy(pos_hbm.at[pl.ds(step * 2 * tm, 2 * tm)], idx_smem, isem)
    icopy.start()
    icopy.wait()

    def row_copy(i, k):
        return pltpu.make_async_copy(xs_ref.at[pl.ds(i * SLAB, SLAB), :], out_hbm.at[idx_smem[2 * i + k]], sem)

    def issue(i, carry):
        row_copy(i, 0).start()
        row_copy(i, 1).start()
        return carry

    def drain(i, carry):
        row_copy(i, 0).wait()
        row_copy(i, 1).wait()
        return carry

    lax.fori_loop(0, tm, issue, 0, unroll=8)
    lax.fori_loop(0, tm, drain, 0, unroll=8)


def _dispatch(xs, pos, n_rows):
    n = xs.shape[0] // SLAB
    tm = min(n, ROUTE_TM)
    init = jnp.zeros((n_rows, SLAB, LANES), F32)
    return pl.pallas_call(
        functools.partial(_dispatch_kernel, tm=tm),
        out_shape=jax.ShapeDtypeStruct((n_rows, SLAB, LANES), F32),
        grid=(n // tm,),
        in_specs=[pl.BlockSpec(memory_space=pl.ANY),
                  pl.BlockSpec((tm * SLAB, LANES), lambda i: (i, 0)),
                  pl.BlockSpec(memory_space=pl.ANY)],
        out_specs=pl.BlockSpec(memory_space=pl.ANY),
        scratch_shapes=[pltpu.SMEM((2 * tm,), jnp.int32), pltpu.SemaphoreType.DMA, pltpu.SemaphoreType.DMA],
        input_output_aliases={2: 0},
        compiler_params=_cp(("arbitrary",), VMEM_LIMIT),
        name="moe_dispatch",
    )(pos.reshape(-1), xs, init)


def _group_ffn_kernel(te_ref, x_ref, wg_ref, wu_ref, wd_ref, o_ref, xn_scr, acc_scr, *, tm, nt):
    t = pl.program_id(0)
    f = pl.program_id(1)
    valid = t < te_ref[nt]
    last = f == pl.num_programs(1) - 1

    @pl.when(jnp.logical_and(valid, f == 0))
    def _():
        xn_scr[...] = _from_slabs(x_ref, tm).astype(BF16)
        acc_scr[...] = jnp.zeros_like(acc_scr)

    @pl.when(valid)
    def _():
        xn = xn_scr[...]
        gt = _dot(xn, wg_ref[...])
        up = _dot(xn, wu_ref[...])
        acc_scr[...] += _dot((gt * _sigmoid(gt) * up).astype(BF16), wd_ref[...])

    @pl.when(jnp.logical_and(valid, last))
    def _():
        _to_slabs(o_ref, acc_scr[...], tm)

    @pl.when(jnp.logical_and(jnp.logical_not(valid), last))
    def _():
        o_ref[...] = jnp.zeros_like(o_ref)


def _group_ffn(xs, tile_expert, wg, wu, wd, *, tf):
    n_rows = xs.shape[0]
    tm = GROUP_TM
    nt = n_rows // tm
    _, d, ff = wg.shape
    x2 = xs.reshape(n_rows * SLAB, LANES)
    rows = pl.BlockSpec((tm * SLAB, LANES), lambda t, f, te: (t, 0))
    out = pl.pallas_call(
        functools.partial(_group_ffn_kernel, tm=tm, nt=nt),
        out_shape=jax.ShapeDtypeStruct((n_rows * SLAB, LANES), F32),
        grid_spec=pltpu.PrefetchScalarGridSpec(
            num_scalar_prefetch=1,
            grid=(nt, ff // tf),
            in_specs=[rows,
                      pl.BlockSpec((None, d, tf), lambda t, f, te: (te[t], 0, f)),
                      pl.BlockSpec((None, d, tf), lambda t, f, te: (te[t], 0, f)),
                      pl.BlockSpec((None, tf, d), lambda t, f, te: (te[t], f, 0))],
            out_specs=rows,
            scratch_shapes=[pltpu.VMEM((tm, d), BF16), pltpu.VMEM((tm, d), F32)]),
        compiler_params=_cp(("arbitrary", "arbitrary"), VMEM_LIMIT),
        name="moe_group_ffn",
    )(tile_expert, x2, wg, wu, wd)
    return out.reshape(n_rows, SLAB, LANES)


def _combine_kernel(pos_hbm, ys_hbm, x_ref, gate_ref, w_ref, fg_ref, o_ref, idx_smem, ybuf, isem, sem,
                    *, tm, final_norm):
    step = pl.program_id(0) * pl.num_programs(1) + pl.program_id(1)
    icopy = pltpu.make_async_copy(pos_hbm.at[pl.ds(step * 2 * tm, 2 * tm)], idx_smem, isem)
    icopy.start()
    icopy.wait()

    def row_copy(i, k):
        return pltpu.make_async_copy(ys_hbm.at[idx_smem[2 * i + k]], ybuf.at[k, pl.ds(i * SLAB, SLAB), :], sem)

    def issue(i, carry):
        row_copy(i, 0).start()
        row_copy(i, 1).start()
        return carry

    def drain(i, carry):
        row_copy(i, 0).wait()
        row_copy(i, 1).wait()
        return carry

    lax.fori_loop(0, tm, issue, 0, unroll=8)
    lax.fori_loop(0, tm, drain, 0, unroll=8)
    w = w_ref[...]
    moe = w[:, 0:1] * _from_slabs(ybuf.at[0], tm) + w[:, 1:2] * _from_slabs(ybuf.at[1], tm)
    y = x_ref[...] + gate_ref[...] * moe
    if final_norm:
        ms = jnp.mean(y * y, axis=-1, keepdims=True)
        y = y * lax.rsqrt(ms + RMS_EPS) * fg_ref[...]
    o_ref[...] = y


def _combine(ys, pos, x, gate, w, final_g):
    b, l, d = x.shape
    tm = min(l, ROUTE_TM)
    nl = l // tm
    final_norm = final_g is not None
    fg = final_g if final_norm else jnp.ones((1, d), F32)
    row = lambda bi, i: (bi, i, 0)
    return pl.pallas_call(
        functools.partial(_combine_kernel, tm=tm, final_norm=final_norm),
        out_shape=jax.ShapeDtypeStruct((b, l, d), F32),
        grid=(b, nl),
        in_specs=[pl.BlockSpec(memory_space=pl.ANY),
                  pl.BlockSpec(memory_space=pl.ANY),
                  pl.BlockSpec((None, tm, d), row),
                  pl.BlockSpec((None, 1, d), lambda bi, i: (bi, 0, 0)),
                  pl.BlockSpec((tm, LANES), lambda bi, i: (bi * nl + i, 0)),
                  pl.BlockSpec((1, d), lambda bi, i: (0, 0))],
        out_specs=pl.BlockSpec((None, tm, d), row),
        scratch_shapes=[pltpu.SMEM((2 * tm,), jnp.int32), pltpu.VMEM((2, tm * SLAB, LANES), F32),
                        pltpu.SemaphoreType.DMA, pltpu.SemaphoreType.DMA],
        compiler_params=_cp(("arbitrary", "arbitrary"), VMEM_LIMIT),
        name="moe_combine",
    )(pos.reshape(-1), ys, x, gate, w, fg)


def _moe(x, g, sh, sc, gate, router_w, router_b, wg, wu, wd, final_g):
    b, l, d = x.shape
    n = b * l
    ne = router_w.shape[-1]
    rw = jnp.pad(router_w.astype(F32), ((0, 0), (0, LANES - ne)))
    rwh, rwl = _split_bf16(rw)
    rb = jnp.pad(router_b.astype(F32), (0, LANES - ne), constant_values=NEG_BIG).reshape(1, LANES)
    xs, meta, w, counts = _route(x, g, sh, sc, rwh, rwl, rb)
    counts = counts[0, :ne]
    padded = (counts + GROUP_TM - 1) // GROUP_TM * GROUP_TM
    ends = jnp.cumsum(padded)
    offs = ends - padded
    experts = jnp.arange(ne, dtype=jnp.int32)
    off_of = lambda e: jnp.sum(jnp.where(e[:, None] == experts[None, :], offs[None, :], 0), axis=1)
    pos = jnp.stack([off_of(meta[:, 0]) + meta[:, 2], off_of(meta[:, 1]) + meta[:, 3]], axis=1).astype(jnp.int32)
    n_rows = TOP_K * n + ne * GROUP_TM
    nt = n_rows // GROUP_TM
    tile_start = jnp.arange(nt, dtype=jnp.int32) * GROUP_TM
    tile_e = jnp.sum(tile_start[:, None] >= ends[None, :], axis=1)
    n_live = ends[-1] // GROUP_TM
    last_e = jnp.sum((n_live - 1) * GROUP_TM >= ends)
    tile_e = jnp.where(tile_start < ends[-1], tile_e, last_e)
    tile_expert = jnp.concatenate([tile_e, n_live[None]]).astype(jnp.int32)
    xd = _dispatch(xs, pos, n_rows)
    ys = _group_ffn(xd, tile_expert, wg, wu, wd, tf=512)
    return _combine(ys, pos, x, gate, w, final_g)


def _hyena_filters(l, fw1, fb1, freq1, fw2, fb2, freq2, fw3):
    half = l // 2
    pos = np.concatenate([np.arange(half), l - 1 - np.arange(half)]).astype(np.float32)[:, None]
    t = jnp.asarray(pos) / (l - 1)
    w = (2.0 * math.pi / l) * jnp.asarray(pos)
    bands = jnp.linspace(1e-4, HYENA_BANDS - 1, HYENA_BANDS, dtype=F32)[None, :]
    z = jnp.concatenate([t, jnp.cos(bands * w), -jnp.sin(bands * w)], axis=-1)
    h = jnp.sin(freq1.astype(F32) * (jnp.dot(z, fw1.astype(F32), precision=HI) + fb1.astype(F32)))
    h = jnp.sin(freq2.astype(F32) * (jnp.dot(h, fw2.astype(F32), precision=HI) + fb2.astype(F32)))
    h = jnp.dot(h, fw3.astype(F32), precision=HI).reshape(l, 2, HYENA_ORDER, D_HYENA)
    max_decay = math.log(HYENA_DECAY_TARGET) / HYENA_FAST_DECAY_PCT
    min_decay = math.log(HYENA_DECAY_TARGET) / HYENA_SLOW_DECAY_PCT
    deltas = jnp.abs(jnp.linspace(min_decay, max_decay, D_HYENA, dtype=F32))
    return h * jnp.exp(-t[:, :, None, None] * deltas)


def _hyena_spectra(l, mats_fwd, *filter_params):
    half = l // 2
    n2 = 2 * l
    h = _hyena_filters(l, *filter_params)
    tot = jnp.sum(jnp.abs(h), axis=0)
    hb0 = h[0, 1]
    denom = tot[0] + tot[1] - jnp.abs(hb0)
    sig = h.reshape(2, half, 2, HYENA_ORDER, D_HYENA).transpose(2, 3, 0, 1, 4)
    sig = sig.reshape(2 * HYENA_ORDER, 2, half, D_HYENA)
    fr, fi = _hy_fwd(sig, mats_fwd, None, out_dtype=F32)
    fr = fr.reshape(2, HYENA_ORDER, 2, half, D_HYENA)
    fi = fi.reshape(2, HYENA_ORDER, 2, half, D_HYENA)
    freq = 2 * np.arange(half)[None, :] + np.arange(2)[:, None]
    phi = np.pi * freq / (2.0 * l)
    cph = jnp.asarray(np.cos(phi), F32)[:, :, None]
    sph = jnp.asarray(np.sin(phi), F32)[:, :, None]
    re = fr * cph - fi * sph
    im = fr * sph + fi * cph
    dc_slot = jnp.asarray(freq == 0)[:, :, None]
    k_re = re[0] + re[1] - hb0[:, None, None, :]
    k_im = im[0] - im[1]
    k_nyq = fi[0] + fi[1] - hb0[:, None, None, :]
    wgt = jnp.where(dc_slot, 1.0, 2.0) / (n2 * denom[:, None, None, :])
    kr = k_re * wgt
    ki = jnp.where(dc_slot, 0.0, k_im * wgt)
    kd = jnp.where(dc_slot, k_nyq * wgt, kr)
    return [jnp.stack([kr[o], ki[o], kd[o]], axis=1) for o in range(HYENA_ORDER)]


def _fnet_maps(w_f, l):
    d = FNET_HEAD_DIM
    idx = np.arange(d)
    ang = 2.0 * np.pi * ((idx[:, None] * idx[None, :]) % d) / d
    scale = 1.0 / math.sqrt(d * l)
    cd = jnp.asarray(np.cos(ang) * scale, F32)
    sd = jnp.asarray(np.sin(ang) * scale, F32)
    wf = w_f.astype(F32)
    a = jnp.einsum('de,hef->hdf', cd, wf, precision=HI)
    bm = -jnp.einsum('de,hef->hdf', sd, wf, precision=HI)
    eye = jnp.eye(FNET_HEADS, dtype=F32)
    blk = lambda m: jnp.einsum('hdf,hg->hdgf', m, eye).reshape(D_FNET, D_FNET).astype(BF16)
    return blk(a), blk(bm)


def kernel(x, c, ctx, c_ctx, ada_w, ada_b, norm_mix_g, norm_ffn_g, w_in, w_out, fnet_w, hy_conv_w, hy_conv_b,
           hy_fw1, hy_fb1, hy_freq1, hy_fw2, hy_fb2, hy_freq2, hy_fw3, hy_bias, s5_a_re, s5_a_im, s5_log_dt,
           s5_b_re, s5_b_im, s5_c_re, s5_c_im, s5_d, s5_glu_w, s5_glu_b, ffn_w_gate, ffn_w_up, ffn_w_down,
           moe_router_w, moe_router_b, moe_w_gate, moe_w_up, moe_w_down, final_g):
    b, l, d = x.shape
    lc = ctx.shape[1]
    depth = ada_w.shape[0]
    assert l % LANES == 0 and lc % LANES == 0 and d % LANES == 0

    dft = {}
    for n in sorted({l, lc}):
        fwd = _hyena_dft(n)
        dft[n] = dict(hy_fwd=fwd, hy_inv={k: m.T for k, m in fwd.items()}, fn=_fnet_dft(n))

    cc = jnp.concatenate([c, c_ctx[None, :]], axis=0).astype(F32)
    rows = -(-(b + 1) // 8) * 8
    cc = jnp.pad(cc, ((0, rows - (b + 1)), (0, 0)))

    def mix(li, xs, sh, sc, grid_mode, s0f, s0b, s5m, states_only):
        n_tok = xs.shape[1]
        g_mix = norm_mix_g[li].reshape(1, d)
        if states_only:
            (ps,) = _inproj(xs, g_mix, sh, sc, w_in[li][:, S5_OFF:].astype(BF16),
                            jnp.zeros((3, LANES), F32), jnp.zeros((1, LANES), F32),
                            grid_mode=grid_mode, full=False)
            _, hf, hb = _s5(ps, s0f, s0b, s5m, s5_d[li])
            return None, hf, hb
        pf, qv, qx1, qx2, ps = _inproj(
            xs, g_mix, sh, sc, w_in[li].astype(BF16), hy_conv_w[li].astype(F32),
            hy_conv_b[li].astype(F32).reshape(1, -1), grid_mode=grid_mode, full=True)
        tabs = dft[n_tok]
        amat, bmat = _fnet_maps(fnet_w[li], n_tok)
        yf = _fnet(pf, *tabs['fn'], amat, bmat)
        spectra = _hyena_spectra(n_tok, tabs['hy_fwd'], hy_fw1[li], hy_fb1[li], hy_freq1[li], hy_fw2[li],
                                 hy_fb2[li], hy_freq2[li], hy_fw3[li])
        z = qv
        for o, xg in enumerate((qx1, qx2)):
            pr, pi = _hy_fwd(z, tabs['hy_fwd'], spectra[o])
            z = _hy_inv(pr, pi, tabs['hy_inv'], xg, z, hy_bias[li][o].astype(F32))
        ys, hf, hb = _s5(ps, s0f, s0b, s5m, s5_d[li])
        return (yf, z, ys), hf, hb

    def out_proj(li, parts, xs, gate):
        wo = w_out[li].astype(BF16)
        return _outproj(parts[0], parts[1], parts[2], xs, gate,
                        wo[:D_FNET], wo[D_FNET:D_FNET + D_HYENA], wo[D_FNET + D_HYENA:],
                        s5_glu_w[li].astype(BF16), s5_glu_b[li].astype(F32).reshape(1, -1))

    def channel_mix(li, xs, sh, sc, gate, fin):
        i = li // 2
        g_ffn = norm_ffn_g[li].reshape(1, d)
        if li % 2 == 0:
            return _ffn(xs, g_ffn, sh, sc, gate, ffn_w_gate[i].astype(BF16),
                        ffn_w_up[i].astype(BF16), ffn_w_down[i].astype(BF16), fin, tf=1408)
        return _moe(xs, g_ffn, sh, sc, gate, moe_router_w[i], moe_router_b[i], moe_w_gate[i].astype(BF16),
                    moe_w_up[i].astype(BF16), moe_w_down[i].astype(BF16), fin)

    zero_state = jnp.zeros((b, N_S5_BLOCKS, SLAB, LANES), F32)
    ctx_s = ctx
    for li in range(depth):
        last = li == depth - 1
        mod = _ada(cc, ada_w.astype(F32), li, ada_b[li].astype(F32))
        lat = [m.reshape(b, 1, d) for m in jnp.split(mod[:b], N_MOD, axis=-1)]
        cm = [jnp.broadcast_to(m.reshape(1, 1, d), (b, 1, d)) for m in jnp.split(mod[b:b + 1], N_MOD, axis=-1)]
        s5m = _s5_matrices(s5_a_re[li], s5_a_im[li], s5_log_dt[li], s5_b_re[li], s5_b_im[li],
                           s5_c_re[li], s5_c_im[li])

        parts, hf_c, hb_c = mix(li, ctx_s, cm[0], cm[1], False, zero_state, zero_state, s5m, last)
        if not last:
            ctx_s = out_proj(li, parts, ctx_s, cm[2])
            ctx_s = channel_mix(li, ctx_s, cm[3], cm[4], cm[5], None)

        parts, _, _ = mix(li, x, lat[0], lat[1], True, hf_c, hb_c, s5m, False)
        x = out_proj(li, parts, x, lat[2])
        x = channel_mix(li, x, lat[3], lat[4], lat[5], final_g.reshape(1, d) if last else None)

    if depth == 0:
        raise ValueError("depth must be positive")
    return x
```

```python
import functools
import math

import numpy as np
import jax
import jax.numpy as jnp
from jax import lax
from jax.experimental import pallas as pl
from jax.experimental.pallas import tpu as pltpu

F32 = jnp.float32
BF16 = jnp.bfloat16
HI = lax.Precision.HIGHEST

LANES = 128
GRID_ROW = 64
FNET_HEADS, FNET_HEAD_DIM = 4, 64
D_FNET = FNET_HEADS * FNET_HEAD_DIM
D_HYENA = 384
HYENA_ORDER = 2
HYENA_EMB = 33
HYENA_BANDS = (HYENA_EMB - 1) // 2
HYENA_FAST_DECAY_PCT, HYENA_SLOW_DECAY_PCT, HYENA_DECAY_TARGET = 0.3, 1.5, 0.01
S5_GROUP, S5_GROUPS, S5_STATE = 16, 24, 64
D_S5 = S5_GROUP * S5_GROUPS
HY_OFF = D_FNET
S5_OFF = D_FNET + (HYENA_ORDER + 1) * D_HYENA
N_MOD = 6
TOP_K = 2
RMS_EPS = 1e-6
S5_CHUNK = 16
GROUPS_PER_BLOCK = LANES // S5_GROUP
N_S5_BLOCKS = D_S5 // LANES
STATE_HALF = GROUPS_PER_BLOCK * S5_STATE
NEG_BIG = float(np.finfo(np.float32).min)
VMEM_LIMIT = 56 * 1024 * 1024


def _cp(sem, vmem=None):
    return pltpu.CompilerParams(dimension_semantics=sem, vmem_limit_bytes=vmem)


def _dot(a, b):
    return jnp.dot(a, b, preferred_element_type=F32)


def _split_bf16(a):
    hi = a.astype(BF16)
    lo = (a - hi.astype(F32)).astype(BF16)
    return hi, lo


def _sigmoid(x):
    return 1.0 / (1.0 + jnp.exp(-x))


def _rms_mod(x, g, sh, sc):
    ms = jnp.mean(x * x, axis=-1, keepdims=True)
    return (x * lax.rsqrt(ms + RMS_EPS) * g) * (1.0 + sc) + sh


def _ada_kernel(c_ref, w_ref, b_ref, o_ref):
    c = c_ref[...]
    s = c * _sigmoid(c)
    s_hi, s_lo = _split_bf16(s)
    w_hi, w_lo = _split_bf16(w_ref[...])
    o_ref[...] = _dot(s_hi, w_hi) + _dot(s_hi, w_lo) + _dot(s_lo, w_hi) + b_ref[...]


def _ada(cc, w_all, li, b):
    r, d = cc.shape
    n = w_all.shape[2]
    tn = 512
    return pl.pallas_call(
        _ada_kernel,
        out_shape=jax.ShapeDtypeStruct((r, n), F32),
        grid=(n // tn,),
        in_specs=[pl.BlockSpec((r, d), lambda j: (0, 0)),
                  pl.BlockSpec((None, d, tn), lambda j: (li, 0, j)),
                  pl.BlockSpec((1, tn), lambda j: (0, j))],
        out_specs=pl.BlockSpec((r, tn), lambda j: (0, j)),
        compiler_params=_cp(("parallel",)),
        name="ada_mod",
    )(cc, w_all, b.reshape(1, n))


def _expand_kernel(a1_ref, a2_ref, b1_ref, b2_ref, o_ref, *, nh):
    a2 = a2_ref[...]
    b2 = b2_ref[...]
    for h in range(nh):
        blk = a1_ref[:, h:h + 1] * a2 + b1_ref[:, h:h + 1] * b2
        o_ref[:, h * LANES:(h + 1) * LANES] = blk.astype(o_ref.dtype)


def _expand(a1, a2, b1, b2):
    r, nh = a1.shape
    tr = min(r, 256)
    return pl.pallas_call(
        functools.partial(_expand_kernel, nh=nh),
        out_shape=jax.ShapeDtypeStruct((r, nh * LANES), BF16),
        grid=(r // tr,),
        in_specs=[pl.BlockSpec((tr, nh), lambda i: (i, 0)),
                  pl.BlockSpec((tr, LANES), lambda i: (i, 0)),
                  pl.BlockSpec((tr, nh), lambda i: (i, 0)),
                  pl.BlockSpec((tr, LANES), lambda i: (i, 0))],
        out_specs=pl.BlockSpec((tr, nh * LANES), lambda i: (i, 0)),
        compiler_params=_cp(("parallel",)),
        name="dft_expand",
    )(a1, a2, b1, b2)


def _fnet_dft(l):
    h = l // 2
    nh = h // LANES
    period = 2 * l
    k = np.arange(l, dtype=np.int64)[:, None]
    qa = (k * (2 * LANES) * np.arange(nh, dtype=np.int64)[None, :]) % period
    qb = (k * (2 * np.arange(LANES, dtype=np.int64)[None, :] + 1)) % period
    aa = 2.0 * np.pi * qa / period
    ab = 2.0 * np.pi * qb / period
    f = lambda t: jnp.asarray(t, F32)
    c1, s1, c2, s2 = np.cos(aa), np.sin(aa), np.cos(ab), np.sin(ab)
    phase = np.zeros((l, LANES), np.float32)
    phase[:, 0] = np.cos(np.pi * k[:, 0] / l)
    phase[:, 1] = np.sin(np.pi * k[:, 0] / l)
    return _expand(f(c1), f(c2), f(-s1), f(s2)), _expand(f(-s1), f(c2), f(-c1), f(s2)), jnp.asarray(phase)


def _hyena_dft(l):
    h = l // 2
    nh = h // LANES
    period = 4 * l
    kp = np.arange(h, dtype=np.int64)[:, None]
    f = lambda t: jnp.asarray(t, F32)
    out = {}
    for name, k in (("e", 2 * kp), ("o", 2 * kp + 1)):
        qa = (k * (2 * LANES) * np.arange(nh, dtype=np.int64)[None, :]) % period
        qb = (k * (2 * np.arange(LANES, dtype=np.int64)[None, :] + 1)) % period
        aa = 2.0 * np.pi * qa / period
        ab = 2.0 * np.pi * qb / period
        c1, s1, c2, s2 = np.cos(aa), np.sin(aa), np.cos(ab), np.sin(ab)
        a1, a2, b1 = -s1, c2.copy(), -c1
        if name == "e":
            a1[0, :] = 1.0
            a2[0, :] = (-1.0) ** np.arange(LANES)
            b1[0, :] = 0.0
        out["c" + name] = _expand(f(c1), f(c2), f(-s1), f(s2))
        out["s" + name] = _expand(f(a1), f(a2), f(b1), f(s2))
    return out


def _flip_matrix(n):
    return jnp.asarray(np.eye(n, dtype=np.float32)[::-1], BF16)


def _flip_rows(v, j):
    sb = j.shape[0]
    nblk = v.shape[0] // sb
    v_hi, v_lo = _split_bf16(v)
    parts = []
    for s in range(nblk):
        r = slice((nblk - 1 - s) * sb, (nblk - s) * sb)
        parts.append(_dot(j, v_hi[r]) + _dot(j, v_lo[r]))
    return parts[0] if nblk == 1 else jnp.concatenate(parts, axis=0)


def _inproj_kernel(x_ref, g_ref, sh_ref, sc_ref, w_ref, cw_ref, cb_ref, j_ref, *out_refs, tm, nt, grid_mode, full):
    h = _rms_mod(x_ref[...], g_ref[...], sh_ref[...], sc_ref[...])
    acc = _dot(h.astype(BF16), w_ref[...])
    if not full:
        out_refs[0][...] = acc
        return
    pf_ref, qv_ref, qx1_ref, qx2_ref, ps_ref = out_refs
    ps_ref[...] = acc[:, S5_OFF:]
    hy = acc[:, HY_OFF:S5_OFF]
    row = lax.broadcasted_iota(jnp.int32, (tm, 1), 0)
    if grid_mode:
        first = (row % GRID_ROW) == 0
        last = (row % GRID_ROW) == GRID_ROW - 1
    else:
        first = row == 0
        last = row == tm - 1
    prev = jnp.where(first, 0.0, pltpu.roll(hy, 1, 0))
    nxt = jnp.where(last, 0.0, pltpu.roll(hy, tm - 1, 0))
    q = prev * cw_ref[0:1, :] + hy * cw_ref[1:2, :] + nxt * cw_ref[2:3, :] + cb_ref[...]
    q = jnp.concatenate([acc[:, :HY_OFF], q], axis=1)
    folded = (pf_ref, qv_ref, qx1_ref, qx2_ref)
    edges = (0, D_FNET, D_FNET + D_HYENA, D_FNET + 2 * D_HYENA, D_FNET + 3 * D_HYENA)
    cols = [slice(edges[n], edges[n + 1]) for n in range(len(folded))]
    if nt == 1:
        half = tm // 2
        q_hi = _flip_rows(q[half:], j_ref[...])
        for r, c in zip(folded, cols):
            r[0] = q[:half, c]
            r[1] = q_hi[:, c]
        return
    i = pl.program_id(1)

    @pl.when(i < nt // 2)
    def _():
        for r, c in zip(folded, cols):
            r[...] = q[:, c]

    @pl.when(i >= nt // 2)
    def _():
        qf = _flip_rows(q, j_ref[...])
        for r, c in zip(folded, cols):
            r[...] = qf[:, c]


def _fold_spec(tm, nt, width):
    if nt == 1:
        return pl.BlockSpec((None, 2, tm // 2, width), lambda bi, i: (bi, 0, 0, 0))
    hn = nt // 2
    return pl.BlockSpec((None, None, tm, width),
                        lambda bi, i: (bi, i // hn, jnp.where(i < hn, i, nt - 1 - i), 0))


def _inproj(x, g, sh, sc, w, cw, cb, *, grid_mode, full):
    b, l, d = x.shape
    n = w.shape[1]
    tm = min(l, 512)
    nt = l // tm
    if not grid_mode:
        assert tm == l
    else:
        assert tm % GRID_ROW == 0
    assert nt == 1 or nt % 2 == 0
    row = lambda bi, i: (bi, i, 0)
    vec = lambda bi, i: (bi, 0, 0)
    const = lambda bi, i: (0, 0)
    flip = _flip_matrix(min(256, tm // 2 if nt == 1 else tm))
    plain = lambda wd: (jax.ShapeDtypeStruct((b, l, wd), F32), pl.BlockSpec((None, tm, wd), row))
    fold = lambda wd: (jax.ShapeDtypeStruct((b, 2, l // 2, wd), F32), _fold_spec(tm, nt, wd))
    outs = ([fold(D_FNET), fold(D_HYENA), fold(D_HYENA), fold(D_HYENA), plain(D_S5)] if full else [plain(n)])
    return pl.pallas_call(
        functools.partial(_inproj_kernel, tm=tm, nt=nt, grid_mode=grid_mode, full=full),
        out_shape=tuple(o[0] for o in outs),
        grid=(b, nt),
        in_specs=[pl.BlockSpec((None, tm, d), row),
                  pl.BlockSpec((1, d), const),
                  pl.BlockSpec((None, 1, d), vec),
                  pl.BlockSpec((None, 1, d), vec),
                  pl.BlockSpec((d, n), const),
                  pl.BlockSpec(cw.shape, const),
                  pl.BlockSpec(cb.shape, const),
                  pl.BlockSpec(flip.shape, const)],
        out_specs=tuple(o[1] for o in outs),
        compiler_params=_cp(("parallel", "parallel"), VMEM_LIMIT),
        name="inproj",
    )(x, g, sh, sc, w, cw, cb, flip)


def _fnet_kernel(x_ref, mc_ref, ms_ref, ph_ref, a_ref, b_ref, o_ref, accr, acci, *, nb):
    k = pl.program_id(2)

    @pl.when(k == 0)
    def _():
        accr[...] = jnp.zeros_like(accr)
        acci[...] = jnp.zeros_like(acci)

    mc = mc_ref[...]
    ms = ms_ref[...]
    for bi in range(nb):
        lo = x_ref[bi, 0]
        hi = x_ref[bi, 1]
        accr[bi] += _dot(mc, (lo + hi).astype(BF16))
        acci[bi] += _dot(ms, (lo - hi).astype(BF16))

    @pl.when(k == pl.num_programs(2) - 1)
    def _():
        ck = ph_ref[:, 0:1]
        sk = ph_ref[:, 1:2]
        for bi in range(nb):
            zr, zi = accr[bi], acci[bi]
            cos_part = ck * zr - sk * zi
            sin_part = -(sk * zr + ck * zi)
            o_ref[bi] = (_dot(cos_part.astype(BF16), a_ref[...])
                         + _dot(sin_part.astype(BF16), b_ref[...]))


def _fnet(pf, mc, ms, phase, amat, bmat):
    b, _, h, c = pf.shape
    l = 2 * h
    nb = min(b, 2)
    t = min(l, 1024)
    tk = min(h, 1024)
    return pl.pallas_call(
        functools.partial(_fnet_kernel, nb=nb),
        out_shape=jax.ShapeDtypeStruct((b, l, c), F32),
        grid=(b // nb, l // t, h // tk),
        in_specs=[pl.BlockSpec((nb, 2, tk, c), lambda bb, i, k: (bb, 0, k, 0)),
                  pl.BlockSpec((t, tk), lambda bb, i, k: (i, k)),
                  pl.BlockSpec((t, tk), lambda bb, i, k: (i, k)),
                  pl.BlockSpec((t, LANES), lambda bb, i, k: (i, 0)),
                  pl.BlockSpec((c, c), lambda bb, i, k: (0, 0)),
                  pl.BlockSpec((c, c), lambda bb, i, k: (0, 0))],
        out_specs=pl.BlockSpec((nb, t, c), lambda bb, i, k: (bb, i, 0)),
        scratch_shapes=[pltpu.VMEM((nb, t, c), F32), pltpu.VMEM((nb, t, c), F32)],
        compiler_params=_cp(("parallel", "parallel", "arbitrary"), VMEM_LIMIT),
        name="fnet_dft",
    )(pf, mc, ms, phase, amat, bmat)


def _hy_fwd_kernel(z_ref, ce_ref, se_ref, co_ref, so_ref, *rest, nb, filtered):
    if filtered:
        kf_ref, pr_ref, pi_ref, acc = rest
    else:
        pr_ref, pi_ref, acc = rest
    k = pl.program_id(2)

    @pl.when(k == 0)
    def _():
        acc[...] = jnp.zeros_like(acc)

    c = z_ref.shape[-1]
    ev = jnp.concatenate([(z_ref[bi, 0] + z_ref[bi, 1]).astype(BF16) for bi in range(nb)], axis=1)
    od = jnp.concatenate([(z_ref[bi, 0] - z_ref[bi, 1]).astype(BF16) for bi in range(nb)], axis=1)
    acc[0] += _dot(ce_ref[...], ev)
    acc[1] += _dot(se_ref[...], od)
    acc[2] += _dot(co_ref[...], od)
    acc[3] += _dot(so_ref[...], ev)

    @pl.when(k == pl.num_programs(2) - 1)
    def _():
        for bi in range(nb):
            cols = slice(bi * c, (bi + 1) * c)
            for p in range(2):
                zr, zi = acc[2 * p, :, cols], acc[2 * p + 1, :, cols]
                if filtered:
                    kr, ki, kd = kf_ref[p, 0], kf_ref[p, 1], kf_ref[p, 2]
                    zr, zi = zr * kr - zi * ki, zr * ki + zi * kd
                pr_ref[bi, p] = zr.astype(pr_ref.dtype)
                pi_ref[bi, p] = zi.astype(pi_ref.dtype)


def _hy_tiles(h):
    return min(h, 1024), min(h, 512)


def _once(block_shape, index_map):
    return pl.BlockSpec(block_shape, index_map, pipeline_mode=pl.Buffered(1))


def _hy_fwd(z, mats, kf, out_dtype=BF16):
    b, _, h, c = z.shape
    nb = min(b, 2)
    ti, tk = _hy_tiles(h)
    mspec = pl.BlockSpec((ti, tk), lambda i, bb, k: (i, k))
    ospec = pl.BlockSpec((nb, 2, ti, c), lambda i, bb, k: (bb, 0, i, 0))
    filtered = kf is not None
    extra_specs = [_once((2, 3, ti, c), lambda i, bb, k: (0, 0, i, 0))] if filtered else []
    extra_args = (kf,) if filtered else ()
    return pl.pallas_call(
        functools.partial(_hy_fwd_kernel, nb=nb, filtered=filtered),
        out_shape=(jax.ShapeDtypeStruct((b, 2, h, c), out_dtype), jax.ShapeDtypeStruct((b, 2, h, c), out_dtype)),
        grid=(h // ti, b // nb, h // tk),
        in_specs=[pl.BlockSpec((nb, 2, tk, c), lambda i, bb, k: (bb, 0, k, 0)),
                  mspec, mspec, mspec, mspec] + extra_specs,
        out_specs=(ospec, ospec),
        scratch_shapes=[pltpu.VMEM((4, ti, nb * c), F32)],
        compiler_params=_cp(("parallel", "parallel", "arbitrary"), VMEM_LIMIT),
        name="hyena_fwd_dft",
    )(z, mats["ce"], mats["se"], mats["co"], mats["so"], *extra_args)


def _hy_inv_kernel(pr_ref, pi_ref, ce_ref, se_ref, co_ref, so_ref, xg_ref, zin_ref, bias_ref, o_ref, acc, *, nb):
    k = pl.program_id(2)

    @pl.when(k == 0)
    def _():
        acc[...] = jnp.zeros_like(acc)

    c = xg_ref.shape[-1]
    side = lambda ref, p: jnp.concatenate([ref[bi, p] for bi in range(nb)], axis=1)
    acc[0] += _dot(ce_ref[...], side(pr_ref, 0)) + _dot(so_ref[...], side(pi_ref, 1))
    acc[1] += _dot(se_ref[...], side(pi_ref, 0)) + _dot(co_ref[...], side(pr_ref, 1))

    @pl.when(k == pl.num_programs(2) - 1)
    def _():
        bias = bias_ref[...]
        for bi in range(nb):
            cols = slice(bi * c, (bi + 1) * c)
            u, v = acc[0, :, cols], acc[1, :, cols]
            o_ref[bi, 0] = xg_ref[bi, 0] * (u + v + zin_ref[bi, 0] * bias)
            o_ref[bi, 1] = xg_ref[bi, 1] * (u - v + zin_ref[bi, 1] * bias)


def _hy_inv(pr, pi, mats_t, xg, zin, bias):
    b, _, h, c = xg.shape
    nb = min(b, 2)
    ti, tk = min(h, 512), min(h, 1024)
    pspec = pl.BlockSpec((nb, 2, tk, c), lambda bb, i, k: (bb, 0, k, 0))
    mspec = pl.BlockSpec((ti, tk), lambda bb, i, k: (i, k))
    ospec = pl.BlockSpec((nb, 2, ti, c), lambda bb, i, k: (bb, 0, i, 0))
    return pl.pallas_call(
        functools.partial(_hy_inv_kernel, nb=nb),
        out_shape=jax.ShapeDtypeStruct((b, 2, h, c), F32),
        grid=(b // nb, h // ti, h // tk),
        in_specs=[pspec, pspec, mspec, mspec, mspec, mspec, ospec, ospec,
                  pl.BlockSpec((1, c), lambda bb, i, k: (0, 0))],
        out_specs=ospec,
        scratch_shapes=[pltpu.VMEM((2, ti, nb * c), F32)],
        compiler_params=_cp(("parallel", "parallel", "arbitrary"), VMEM_LIMIT),
        name="hyena_inv_dft",
    )(pr, pi, mats_t["ce"], mats_t["se"], mats_t["co"], mats_t["so"], xg, zin, bias.reshape(1, c))


def _s5_kernel(u_ref, s0f_ref, s0b_ref, mi_ref, mof_ref, mob_ref, mnf_ref, mnb_ref, atf_ref, atb_ref,
               d_ref, y_ref, hf_ref, hb_ref, x_scr, sf_scr, sb_scr, hinf_scr, hinb_scr, *, nc, t_chunk):
    for s in range(t_chunk):
        x_scr[:, s * LANES:(s + 1) * LANES] = u_ref[pl.ds(s, nc, stride=t_chunk), :].astype(BF16)
    x = x_scr[...]
    _to_slabs(sf_scr, _dot(x, mof_ref[...]), nc)
    _to_slabs(sb_scr, _dot(x, mob_ref[...]), nc)
    cf1, cf2 = atf_ref[0], atf_ref[1]
    cb1, cb2 = atb_ref[0], atb_ref[1]

    def body(c, carry):
        hf, hb = carry
        rf = pl.ds(pl.multiple_of(c * SLAB, SLAB), SLAB)
        rb = pl.ds(pl.multiple_of((nc - 1 - c) * SLAB, SLAB), SLAB)
        hinf_scr[rf, :] = hf
        hinb_scr[rb, :] = hb
        hf = cf1 * hf + cf2 * pltpu.roll(hf, SLAB // 2, 0) + sf_scr[rf, :]
        hb = cb1 * hb + cb2 * pltpu.roll(hb, SLAB // 2, 0) + sb_scr[rb, :]
        return hf, hb

    hf, hb = lax.fori_loop(0, nc, body, (s0f_ref[...], s0b_ref[...]), unroll=4)
    hf_ref[...] = hf
    hb_ref[...] = hb
    y = (_dot(x, mi_ref[...])
         + _dot(_from_slabs(hinf_scr, nc).astype(BF16), mnf_ref[...])
         + _dot(_from_slabs(hinb_scr, nc).astype(BF16), mnb_ref[...]))
    d = d_ref[...]
    for t in range(t_chunk):
        rows = pl.ds(t, nc, stride=t_chunk)
        y_ref[rows, :] = y[:, t * LANES:(t + 1) * LANES] + d * u_ref[rows, :]


def _s5(u, s0f, s0b, mats, d_skip):
    b, l, _ = u.shape
    tc = S5_CHUNK
    nc = l // tc
    kin = tc * LANES
    ns = 2 * STATE_HALF
    mi, mof, mob, mnf, mnb, atf, atb = mats
    st_spec = pl.BlockSpec((None, None, SLAB, LANES), lambda j, bi: (bi, j, 0, 0))
    mode = dict(pipeline_mode=pl.Buffered(1)) if l > 1024 else {}
    w3 = lambda s1, s2: pl.BlockSpec((None, s1, s2), lambda j, bi: (j, 0, 0), **mode)
    coef = pl.BlockSpec((None, 2, SLAB, LANES), lambda j, bi: (j, 0, 0, 0))
    u_spec = pl.BlockSpec((None, l, LANES), lambda j, bi: (bi, 0, j))
    state = jax.ShapeDtypeStruct((b, N_S5_BLOCKS, SLAB, LANES), F32)
    return pl.pallas_call(
        functools.partial(_s5_kernel, nc=nc, t_chunk=tc),
        out_shape=(jax.ShapeDtypeStruct((b, l, D_S5), F32), state, state),
        grid=(N_S5_BLOCKS, b),
        in_specs=[u_spec, st_spec, st_spec,
                  w3(kin, kin), w3(kin, ns), w3(kin, ns), w3(ns, kin), w3(ns, kin),
                  coef, coef,
                  pl.BlockSpec((1, LANES), lambda j, bi: (0, j))],
        out_specs=(u_spec, st_spec, st_spec),
        scratch_shapes=[pltpu.VMEM((nc, kin), BF16)] + [pltpu.VMEM((nc * SLAB, LANES), F32)] * 4,
        compiler_params=_cp(("parallel", "parallel"), VMEM_LIMIT),
        name="s5_scan",
    )(u, s0f, s0b, mi, mof, mob, mnf, mnb, atf, atb, d_skip.reshape(1, D_S5))


def _s5_matrices(a_re, a_im, log_dt, b_re, b_im, c_re, c_im):
    tc = S5_CHUNK
    g, p, hd = S5_GROUPS, S5_STATE, S5_GROUP
    lam = lax.complex(a_re.astype(F32), a_im.astype(F32))
    dt = jnp.exp(log_dt.astype(F32))[..., None]
    a_bar = jnp.exp(lam * dt)
    b_bar = ((a_bar - 1.0) / lam)[..., None] * lax.complex(b_re.astype(F32), b_im.astype(F32))
    cc = lax.complex(c_re.astype(F32), c_im.astype(F32))
    ks = jnp.arange(tc + 1, dtype=F32)
    pw = jnp.exp((lam * dt)[:, :, None, :] * ks[None, None, :, None])
    kern = jnp.real(jnp.einsum('dgip,dgkp,dgpj->dgkij', cc, pw[:, :, :tc], b_bar, precision=HI))
    lag = np.arange(tc)[None, :] - np.arange(tc)[:, None]
    sel = np.stack([lag[None] == np.arange(tc)[:, None, None], -lag[None] == np.arange(tc)[:, None, None]])
    sel = jnp.asarray(sel.astype(np.float32))
    kern_j = kern.reshape(2, N_S5_BLOCKS, GROUPS_PER_BLOCK, tc, hd, hd)
    kt = jnp.einsum('dJgkij,dkst->Jsgjti', kern_j, sel, precision=HI)
    def place(slab, inner, row_group_width):
        width = slab.shape[-1]
        src = np.arange(width)
        e = np.zeros((width, GROUPS_PER_BLOCK * width), np.float32)
        for hh in range(GROUPS_PER_BLOCK):
            e[src, (src // inner) * GROUPS_PER_BLOCK * inner + hh * inner + src % inner] = 1.0
        wide = jnp.einsum('Jrk,kc->Jrc', slab, jnp.asarray(e), precision=HI)
        rows = np.arange(wide.shape[1])[:, None]
        cols = np.arange(wide.shape[2])[None, :]
        keep = (rows // row_group_width) % GROUPS_PER_BLOCK == (cols // inner) % GROUPS_PER_BLOCK
        return jnp.where(jnp.asarray(keep)[None], wide, 0.0)

    mi = place(kt.reshape(N_S5_BLOCKS, tc * LANES, tc * hd), hd, hd)

    def parts(z):
        return jnp.stack([jnp.real(z), jnp.imag(z)], axis=0)

    def out_mat(d, powers):
        z = powers[:, :, :, None] * b_bar[d][:, None, :, :]
        z = parts(z).reshape(2, N_S5_BLOCKS, GROUPS_PER_BLOCK, tc, p, hd)
        z = jnp.transpose(z, (1, 3, 2, 5, 0, 4))
        return place(z.reshape(N_S5_BLOCKS, tc * LANES, 2 * p), p, hd)

    def in_mat(d, powers):
        z = cc[d][:, None, :, :] * powers[:, :, None, :]
        z = jnp.stack([jnp.real(z), -jnp.imag(z)], axis=0)
        z = z.reshape(2, N_S5_BLOCKS, GROUPS_PER_BLOCK, tc, hd, p)
        z = jnp.transpose(z, (1, 0, 2, 5, 3, 4))
        return place(z.reshape(N_S5_BLOCKS, 2 * STATE_HALF, tc * hd), hd, p)

    rev = jnp.arange(tc - 1, -1, -1)
    mof = out_mat(0, pw[0][:, rev])
    mob = out_mat(1, pw[1][:, :tc])
    mnf = in_mat(0, pw[0][:, 1:tc + 1])
    mnb = in_mat(1, pw[1][:, tc - jnp.arange(tc)])

    def step(d):
        ar = jnp.real(pw[d][:, tc]).reshape(N_S5_BLOCKS, STATE_HALF)
        ai = jnp.imag(pw[d][:, tc]).reshape(N_S5_BLOCKS, STATE_HALF)
        c1 = jnp.concatenate([ar, ar], axis=1)
        c2 = jnp.concatenate([-ai, ai], axis=1)
        return jnp.stack([c1, c2], axis=1).reshape(N_S5_BLOCKS, 2, SLAB, LANES)

    bf = lambda m: m.astype(BF16)
    return bf(mi), bf(mof), bf(mob), bf(mnf), bf(mnb), step(0), step(1)


def _outproj_kernel(yf_ref, yh_ref, ys_ref, x_ref, gate_ref, w1_ref, w2_ref, w3_ref, gw_ref, gb_ref, j_ref, o_ref,
                    *, nt):
    y = ys_ref[...]
    y = 0.5 * y * (1.0 + jnp.tanh(math.sqrt(2.0 / math.pi) * (y + 0.044715 * (y * y * y))))
    y = y * _sigmoid(_dot(y.astype(BF16), gw_ref[...]) + gb_ref[...])
    acc = _dot(yf_ref[...].astype(BF16), w1_ref[...]) + _dot(y.astype(BF16), w3_ref[...])

    def finish(yh):
        o_ref[...] = x_ref[...] + gate_ref[...] * (acc + _dot(yh.astype(BF16), w2_ref[...]))

    if nt == 1:
        finish(jnp.concatenate([yh_ref[0], _flip_rows(yh_ref[1], j_ref[...])], axis=0))
        return
    i = pl.program_id(1)

    @pl.when(i < nt // 2)
    def _():
        finish(yh_ref[...])

    @pl.when(i >= nt // 2)
    def _():
        finish(_flip_rows(yh_ref[...], j_ref[...]))


def _outproj(yf, yh, ys, x, gate, w1, w2, w3, gw, gb):
    b, l, d = x.shape
    tm = min(l, 512)
    nt = l // tm
    row = lambda bi, i: (bi, i, 0)
    const = lambda bi, i: (0, 0)
    full = lambda a: pl.BlockSpec(a.shape, const)
    flip = _flip_matrix(min(256, tm // 2 if nt == 1 else tm))
    return pl.pallas_call(
        functools.partial(_outproj_kernel, nt=nt),
        out_shape=jax.ShapeDtypeStruct((b, l, d), F32),
        grid=(b, nt),
        in_specs=[pl.BlockSpec((None, tm, yf.shape[2]), row),
                  _fold_spec(tm, nt, yh.shape[3]),
                  pl.BlockSpec((None, tm, ys.shape[2]), row),
                  pl.BlockSpec((None, tm, d), row),
                  pl.BlockSpec((None, 1, d), lambda bi, i: (bi, 0, 0)),
                  full(w1), full(w2), full(w3), full(gw), full(gb), full(flip)],
        out_specs=pl.BlockSpec((None, tm, d), row),
        compiler_params=_cp(("parallel", "parallel"), VMEM_LIMIT),
        name="outproj",
    )(yf, yh, ys, x, gate, w1, w2, w3, gw, gb, flip)


def _ffn_kernel(x_ref, g_ref, sh_ref, sc_ref, gate_ref, wg_ref, wu_ref, wd_ref, fg_ref, o_ref, xn_scr, acc_scr,
                *, final_norm):
    f = pl.program_id(2)

    @pl.when(f == 0)
    def _():
        xn_scr[...] = _rms_mod(x_ref[...], g_ref[...], sh_ref[...], sc_ref[...]).astype(BF16)
        acc_scr[...] = jnp.zeros_like(acc_scr)

    xn = xn_scr[...]
    gt = _dot(xn, wg_ref[...])
    up = _dot(xn, wu_ref[...])
    acc_scr[...] += _dot((gt * _sigmoid(gt) * up).astype(BF16), wd_ref[...])

    @pl.when(f == pl.num_programs(2) - 1)
    def _():
        y = x_ref[...] + gate_ref[...] * acc_scr[...]
        if final_norm:
            ms = jnp.mean(y * y, axis=-1, keepdims=True)
            y = y * lax.rsqrt(ms + RMS_EPS) * fg_ref[...]
        o_ref[...] = y


def _ffn(x, g, sh, sc, gate, wg, wu, wd, final_g, *, tf):
    b, l, d = x.shape
    ff = wg.shape[1]
    final_norm = final_g is not None
    tm = min(l, 1024)
    fg = final_g if final_norm else jnp.ones((1, d), F32)
    row = lambda bi, i, f: (bi, i, 0)
    vec = lambda bi, i, f: (bi, 0, 0)
    const = lambda bi, i, f: (0, 0)
    return pl.pallas_call(
        functools.partial(_ffn_kernel, final_norm=final_norm),
        out_shape=jax.ShapeDtypeStruct((b, l, d), F32),
        grid=(b, l // tm, ff // tf),
        in_specs=[pl.BlockSpec((None, tm, d), row),
                  pl.BlockSpec((1, d), const),
                  pl.BlockSpec((None, 1, d), vec),
                  pl.BlockSpec((None, 1, d), vec),
                  pl.BlockSpec((None, 1, d), vec),
                  pl.BlockSpec((d, tf), lambda bi, i, f: (0, f)),
                  pl.BlockSpec((d, tf), lambda bi, i, f: (0, f)),
                  pl.BlockSpec((tf, d), lambda bi, i, f: (f, 0)),
                  pl.BlockSpec((1, d), const)],
        out_specs=pl.BlockSpec((None, tm, d), row),
        scratch_shapes=[pltpu.VMEM((tm, d), BF16), pltpu.VMEM((tm, d), F32)],
        compiler_params=_cp(("parallel", "parallel", "arbitrary"), VMEM_LIMIT),
        name="dense_ffn",
    )(x, g, sh, sc, gate, wg, wu, wd, fg)


SLAB = 8
ROUTE_TM = 512
GROUP_TM = 1024


def _to_slabs(ref, val, n):
    for s in range(SLAB):
        ref[pl.ds(s, n, stride=SLAB), :] = val[:, s * LANES:(s + 1) * LANES]


def _from_slabs(ref, n):
    return jnp.concatenate([ref[pl.ds(s, n, stride=SLAB), :] for s in range(SLAB)], axis=1)


def _route_kernel(x_ref, g_ref, sh_ref, sc_ref, rwh_ref, rwl_ref, rb_ref, tri_ref,
                  xs_ref, meta_ref, gate_ref, cnt_ref, carry_scr, *, tm):
    step = pl.program_id(0) * pl.num_programs(1) + pl.program_id(1)

    @pl.when(step == 0)
    def _():
        carry_scr[...] = jnp.zeros_like(carry_scr)

    h = _rms_mod(x_ref[...], g_ref[...], sh_ref[...], sc_ref[...])
    _to_slabs(xs_ref, h, tm)
    h_hi, h_lo = _split_bf16(h)
    rwh = rwh_ref[...]
    lg = _dot(h_hi, rwh) + _dot(h_hi, rwl_ref[...]) + _dot(h_lo, rwh) + rb_ref[...]
    lane = lax.broadcasted_iota(jnp.int32, lg.shape, 1)
    m1 = jnp.max(lg, axis=-1, keepdims=True)
    i1 = jnp.min(jnp.where(lg == m1, lane, LANES), axis=-1, keepdims=True)
    lg2 = jnp.where(lane == i1, NEG_BIG, lg)
    m2 = jnp.max(lg2, axis=-1, keepdims=True)
    i2 = jnp.min(jnp.where(lg2 == m2, lane, LANES), axis=-1, keepdims=True)
    ex = jnp.exp(m2 - m1)
    den = 1.0 + ex
    gate_ref[...] = jnp.where(lane == 0, 1.0 / den, jnp.where(lane == 1, ex / den, 0.0))
    sel1 = lane == i1
    sel2 = lane == i2
    tri = tri_ref[...]
    p1 = _dot(tri, sel1.astype(BF16))
    p2 = _dot(tri, sel2.astype(BF16))
    tot1 = p1[tm - 1:tm, :]
    tot2 = p2[tm - 1:tm, :]
    carry = carry_scr[...]
    r1 = jnp.sum(jnp.where(sel1, carry + p1 - 1.0, 0.0), axis=-1, keepdims=True)
    r2 = jnp.sum(jnp.where(sel2, carry + tot1 + p2 - 1.0, 0.0), axis=-1, keepdims=True)
    carry = carry + tot1 + tot2
    carry_scr[...] = carry
    cnt_ref[...] = carry.astype(jnp.int32)
    meta_ref[...] = jnp.where(lane == 0, i1, jnp.where(lane == 1, i2, jnp.where(
        lane == 2, r1.astype(jnp.int32), jnp.where(lane == 3, r2.astype(jnp.int32), 0))))


def _route(x, g, sh, sc, rwh, rwl, rb):
    b, l, d = x.shape
    tm = min(l, ROUTE_TM)
    n = b * l
    nl = l // tm
    tri = jnp.tril(jnp.ones((tm, tm), F32)).astype(BF16)
    row = lambda bi, i: (bi, i, 0)
    vec = lambda bi, i: (bi, 0, 0)
    const = lambda bi, i: (0, 0)
    flat = lambda bi, i: (bi * nl + i, 0)
    return pl.pallas_call(
        functools.partial(_route_kernel, tm=tm),
        out_shape=(jax.ShapeDtypeStruct((n * SLAB, LANES), F32),
                   jax.ShapeDtypeStruct((n, LANES), jnp.int32),
                   jax.ShapeDtypeStruct((n, LANES), F32),
                   jax.ShapeDtypeStruct((1, LANES), jnp.int32)),
        grid=(b, nl),
        in_specs=[pl.BlockSpec((None, tm, d), row),
                  pl.BlockSpec((1, d), const),
                  pl.BlockSpec((None, 1, d), vec),
                  pl.BlockSpec((None, 1, d), vec),
                  pl.BlockSpec((d, LANES), const),
                  pl.BlockSpec((d, LANES), const),
                  pl.BlockSpec((1, LANES), const),
                  pl.BlockSpec((tm, tm), const)],
        out_specs=(pl.BlockSpec((tm * SLAB, LANES), flat),
                   pl.BlockSpec((tm, LANES), flat),
                   pl.BlockSpec((tm, LANES), flat),
                   pl.BlockSpec((1, LANES), const)),
        scratch_shapes=[pltpu.VMEM((1, LANES), F32)],
        compiler_params=_cp(("arbitrary", "arbitrary"), VMEM_LIMIT),
        name="moe_route",
    )(x, g, sh, sc, rwh, rwl, rb, tri)


def _dispatch_kernel(pos_hbm, xs_ref, init_hbm, out_hbm, idx_smem, isem, sem, *, tm):
    del init_hbm
    step = pl.program_id(0)
    icopy = pltpu.make_async_copy(pos_hbm.at[pl.ds(step * 2 * tm, 2 * tm)], idx_smem, isem)
    icopy.start()
    icopy.wait()

    def row_copy(i, k):
        return pltpu.make_async_copy(xs_ref.at[pl.ds(i * SLAB, SLAB), :], out_hbm.at[idx_smem[2 * i + k]], sem)

    def issue(i, carry):
        row_copy(i, 0).start()
        row_copy(i, 1).start()
        return carry

    def drain(i, carry):
        row_copy(i, 0).wait()
        row_copy(i, 1).wait()
        return carry

    lax.fori_loop(0, tm, issue, 0, unroll=8)
    lax.fori_loop(0, tm, drain, 0, unroll=8)


def _dispatch(xs, pos, n_rows):
    n = xs.shape[0] // SLAB
    tm = min(n, ROUTE_TM)
    init = jnp.zeros((n_rows, SLAB, LANES), F32)
    return pl.pallas_call(
        functools.partial(_dispatch_kernel, tm=tm),
        out_shape=jax.ShapeDtypeStruct((n_rows, SLAB, LANES), F32),
        grid=(n // tm,),
        in_specs=[pl.BlockSpec(memory_space=pl.ANY),
                  pl.BlockSpec((tm * SLAB, LANES), lambda i: (i, 0)),
                  pl.BlockSpec(memory_space=pl.ANY)],
        out_specs=pl.BlockSpec(memory_space=pl.ANY),
        scratch_shapes=[pltpu.SMEM((2 * tm,), jnp.int32), pltpu.SemaphoreType.DMA, pltpu.SemaphoreType.DMA],
        input_output_aliases={2: 0},
        compiler_params=_cp(("arbitrary",), VMEM_LIMIT),
        name="moe_dispatch",
    )(pos.reshape(-1), xs, init)


def _group_ffn_kernel(te_ref, x_ref, wg_ref, wu_ref, wd_ref, o_ref, xn_scr, acc_scr, *, tm, nt):
    t = pl.program_id(0)
    f = pl.program_id(1)
    valid = t < te_ref[nt]
    last = f == pl.num_programs(1) - 1

    @pl.when(jnp.logical_and(valid, f == 0))
    def _():
        xn_scr[...] = _from_slabs(x_ref, tm).astype(BF16)
        acc_scr[...] = jnp.zeros_like(acc_scr)

    @pl.when(valid)
    def _():
        xn = xn_scr[...]
        gt = _dot(xn, wg_ref[...])
        up = _dot(xn, wu_ref[...])
        acc_scr[...] += _dot((gt * _sigmoid(gt) * up).astype(BF16), wd_ref[...])

    @pl.when(jnp.logical_and(valid, last))
    def _():
        _to_slabs(o_ref, acc_scr[...], tm)

    @pl.when(jnp.logical_and(jnp.logical_not(valid), last))
    def _():
        o_ref[...] = jnp.zeros_like(o_ref)


def _group_ffn(xs, tile_expert, wg, wu, wd, *, tf):
    n_rows = xs.shape[0]
    tm = GROUP_TM
    nt = n_rows // tm
    _, d, ff = wg.shape
    x2 = xs.reshape(n_rows * SLAB, LANES)
    rows = pl.BlockSpec((tm * SLAB, LANES), lambda t, f, te: (t, 0))
    out = pl.pallas_call(
        functools.partial(_group_ffn_kernel, tm=tm, nt=nt),
        out_shape=jax.ShapeDtypeStruct((n_rows * SLAB, LANES), F32),
        grid_spec=pltpu.PrefetchScalarGridSpec(
            num_scalar_prefetch=1,
            grid=(nt, ff // tf),
            in_specs=[rows,
                      pl.BlockSpec((None, d, tf), lambda t, f, te: (te[t], 0, f)),
                      pl.BlockSpec((None, d, tf), lambda t, f, te: (te[t], 0, f)),
                      pl.BlockSpec((None, tf, d), lambda t, f, te: (te[t], f, 0))],
            out_specs=rows,
            scratch_shapes=[pltpu.VMEM((tm, d), BF16), pltpu.VMEM((tm, d), F32)]),
        compiler_params=_cp(("arbitrary", "arbitrary"), VMEM_LIMIT),
        name="moe_group_ffn",
    )(tile_expert, x2, wg, wu, wd)
    return out.reshape(n_rows, SLAB, LANES)


def _combine_kernel(pos_hbm, ys_hbm, x_ref, gate_ref, w_ref, fg_ref, o_ref, idx_smem, ybuf, isem, sem,
                    *, tm, final_norm):
    step = pl.program_id(0) * pl.num_programs(1) + pl.program_id(1)
    icopy = pltpu.make_async_copy(pos_hbm.at[pl.ds(step * 2 * tm, 2 * tm)], idx_smem, isem)
    icopy.start()
    icopy.wait()

    def row_copy(i, k):
        return pltpu.make_async_copy(ys_hbm.at[idx_smem[2 * i + k]], ybuf.at[k, pl.ds(i * SLAB, SLAB), :], sem)

    def issue(i, carry):
        row_copy(i, 0).start()
        row_copy(i, 1).start()
        return carry

    def drain(i, carry):
        row_copy(i, 0).wait()
        row_copy(i, 1).wait()
        return carry

    lax.fori_loop(0, tm, issue, 0, unroll=8)
    lax.fori_loop(0, tm, drain, 0, unroll=8)
    w = w_ref[...]
    moe = w[:, 0:1] * _from_slabs(ybuf.at[0], tm) + w[:, 1:2] * _from_slabs(ybuf.at[1], tm)
    y = x_ref[...] + gate_ref[...] * moe
    if final_norm:
        ms = jnp.mean(y * y, axis=-1, keepdims=True)
        y = y * lax.rsqrt(ms + RMS_EPS) * fg_ref[...]
    o_ref[...] = y


def _combine(ys, pos, x, gate, w, final_g):
    b, l, d = x.shape
    tm = min(l, ROUTE_TM)
    nl = l // tm
    final_norm = final_g is not None
    fg = final_g if final_norm else jnp.ones((1, d), F32)
    row = lambda bi, i: (bi, i, 0)
    return pl.pallas_call(
        functools.partial(_combine_kernel, tm=tm, final_norm=final_norm),
        out_shape=jax.ShapeDtypeStruct((b, l, d), F32),
        grid=(b, nl),
        in_specs=[pl.BlockSpec(memory_space=pl.ANY),
                  pl.BlockSpec(memory_space=pl.ANY),
                  pl.BlockSpec((None, tm, d), row),
                  pl.BlockSpec((None, 1, d), lambda bi, i: (bi, 0, 0)),
                  pl.BlockSpec((tm, LANES), lambda bi, i: (bi * nl + i, 0)),
                  pl.BlockSpec((1, d), lambda bi, i: (0, 0))],
        out_specs=pl.BlockSpec((None, tm, d), row),
        scratch_shapes=[pltpu.SMEM((2 * tm,), jnp.int32), pltpu.VMEM((2, tm * SLAB, LANES), F32),
                        pltpu.SemaphoreType.DMA, pltpu.SemaphoreType.DMA],
        compiler_params=_cp(("arbitrary", "arbitrary"), VMEM_LIMIT),
        name="moe_combine",
    )(pos.reshape(-1), ys, x, gate, w, fg)


def _moe(x, g, sh, sc, gate, router_w, router_b, wg, wu, wd, final_g):
    b, l, d = x.shape
    n = b * l
    ne = router_w.shape[-1]
    rw = jnp.pad(router_w.astype(F32), ((0, 0), (0, LANES - ne)))
    rwh, rwl = _split_bf16(rw)
    rb = jnp.pad(router_b.astype(F32), (0, LANES - ne), constant_values=NEG_BIG).reshape(1, LANES)
    xs, meta, w, counts = _route(x, g, sh, sc, rwh, rwl, rb)
    counts = counts[0, :ne]
    padded = (counts + GROUP_TM - 1) // GROUP_TM * GROUP_TM
    ends = jnp.cumsum(padded)
    offs = ends - padded
    experts = jnp.arange(ne, dtype=jnp.int32)
    off_of = lambda e: jnp.sum(jnp.where(e[:, None] == experts[None, :], offs[None, :], 0), axis=1)
    pos = jnp.stack([off_of(meta[:, 0]) + meta[:, 2], off_of(meta[:, 1]) + meta[:, 3]], axis=1).astype(jnp.int32)
    n_rows = TOP_K * n + ne * GROUP_TM
    nt = n_rows // GROUP_TM
    tile_start = jnp.arange(nt, dtype=jnp.int32) * GROUP_TM
    tile_e = jnp.sum(tile_start[:, None] >= ends[None, :], axis=1)
    n_live = ends[-1] // GROUP_TM
    last_e = jnp.sum((n_live - 1) * GROUP_TM >= ends)
    tile_e = jnp.where(tile_start < ends[-1], tile_e, last_e)
    tile_expert = jnp.concatenate([tile_e, n_live[None]]).astype(jnp.int32)
    xd = _dispatch(xs, pos, n_rows)
    ys = _group_ffn(xd, tile_expert, wg, wu, wd, tf=512)
    return _combine(ys, pos, x, gate, w, final_g)


def _hyena_filters(l, fw1, fb1, freq1, fw2, fb2, freq2, fw3):
    half = l // 2
    pos = np.concatenate([np.arange(half), l - 1 - np.arange(half)]).astype(np.float32)[:, None]
    t = jnp.asarray(pos) / (l - 1)
    w = (2.0 * math.pi / l) * jnp.asarray(pos)
    bands = jnp.linspace(1e-4, HYENA_BANDS - 1, HYENA_BANDS, dtype=F32)[None, :]
    z = jnp.concatenate([t, jnp.cos(bands * w), -jnp.sin(bands * w)], axis=-1)
    h = jnp.sin(freq1.astype(F32) * (jnp.dot(z, fw1.astype(F32), precision=HI) + fb1.astype(F32)))
    h = jnp.sin(freq2.astype(F32) * (jnp.dot(h, fw2.astype(F32), precision=HI) + fb2.astype(F32)))
    h = jnp.dot(h, fw3.astype(F32), precision=HI).reshape(l, 2, HYENA_ORDER, D_HYENA)
    max_decay = math.log(HYENA_DECAY_TARGET) / HYENA_FAST_DECAY_PCT
    min_decay = math.log(HYENA_DECAY_TARGET) / HYENA_SLOW_DECAY_PCT
    deltas = jnp.abs(jnp.linspace(min_decay, max_decay, D_HYENA, dtype=F32))
    return h * jnp.exp(-t[:, :, None, None] * deltas)


def _hyena_spectra(l, mats_fwd, *filter_params):
    half = l // 2
    n2 = 2 * l
    h = _hyena_filters(l, *filter_params)
    tot = jnp.sum(jnp.abs(h), axis=0)
    hb0 = h[0, 1]
    denom = tot[0] + tot[1] - jnp.abs(hb0)
    sig = h.reshape(2, half, 2, HYENA_ORDER, D_HYENA).transpose(2, 3, 0, 1, 4)
    sig = sig.reshape(2 * HYENA_ORDER, 2, half, D_HYENA)
    fr, fi = _hy_fwd(sig, mats_fwd, None, out_dtype=F32)
    fr = fr.reshape(2, HYENA_ORDER, 2, half, D_HYENA)
    fi = fi.reshape(2, HYENA_ORDER, 2, half, D_HYENA)
    freq = 2 * np.arange(half)[None, :] + np.arange(2)[:, None]
    phi = np.pi * freq / (2.0 * l)
    cph = jnp.asarray(np.cos(phi), F32)[:, :, None]
    sph = jnp.asarray(np.sin(phi), F32)[:, :, None]
    re = fr * cph - fi * sph
    im = fr * sph + fi * cph
    dc_slot = jnp.asarray(freq == 0)[:, :, None]
    k_re = re[0] + re[1] - hb0[:, None, None, :]
    k_im = im[0] - im[1]
    k_nyq = fi[0] + fi[1] - hb0[:, None, None, :]
    wgt = jnp.where(dc_slot, 1.0, 2.0) / (n2 * denom[:, None, None, :])
    kr = k_re * wgt
    ki = jnp.where(dc_slot, 0.0, k_im * wgt)
    kd = jnp.where(dc_slot, k_nyq * wgt, kr)
    return [jnp.stack([kr[o], ki[o], kd[o]], axis=1) for o in range(HYENA_ORDER)]


def _fnet_maps(w_f, l):
    d = FNET_HEAD_DIM
    idx = np.arange(d)
    ang = 2.0 * np.pi * ((idx[:, None] * idx[None, :]) % d) / d
    scale = 1.0 / math.sqrt(d * l)
    cd = jnp.asarray(np.cos(ang) * scale, F32)
    sd = jnp.asarray(np.sin(ang) * scale, F32)
    wf = w_f.astype(F32)
    a = jnp.einsum('de,hef->hdf', cd, wf, precision=HI)
    bm = -jnp.einsum('de,hef->hdf', sd, wf, precision=HI)
    eye = jnp.eye(FNET_HEADS, dtype=F32)
    blk = lambda m: jnp.einsum('hdf,hg->hdgf', m, eye).reshape(D_FNET, D_FNET).astype(BF16)
    return blk(a), blk(bm)


def kernel(x, c, ctx, c_ctx, ada_w, ada_b, norm_mix_g, norm_ffn_g, w_in, w_out, fnet_w, hy_conv_w, hy_conv_b,
           hy_fw1, hy_fb1, hy_freq1, hy_fw2, hy_fb2, hy_freq2, hy_fw3, hy_bias, s5_a_re, s5_a_im, s5_log_dt,
           s5_b_re, s5_b_im, s5_c_re, s5_c_im, s5_d, s5_glu_w, s5_glu_b, ffn_w_gate, ffn_w_up, ffn_w_down,
           moe_router_w, moe_router_b, moe_w_gate, moe_w_up, moe_w_down, final_g):
    b, l, d = x.shape
    lc = ctx.shape[1]
    depth = ada_w.shape[0]
    assert l % LANES == 0 and lc % LANES == 0 and d % LANES == 0

    dft = {}
    for n in sorted({l, lc}):
        fwd = _hyena_dft(n)
        dft[n] = dict(hy_fwd=fwd, hy_inv={k: m.T for k, m in fwd.items()}, fn=_fnet_dft(n))

    cc = jnp.concatenate([c, c_ctx[None, :]], axis=0).astype(F32)
    rows = -(-(b + 1) // 8) * 8
    cc = jnp.pad(cc, ((0, rows - (b + 1)), (0, 0)))

    def mix(li, xs, sh, sc, grid_mode, s0f, s0b, s5m, states_only):
        n_tok = xs.shape[1]
        g_mix = norm_mix_g[li].reshape(1, d)
        if states_only:
            (ps,) = _inproj(xs, g_mix, sh, sc, w_in[li][:, S5_OFF:].astype(BF16),
                            jnp.zeros((3, LANES), F32), jnp.zeros((1, LANES), F32),
                            grid_mode=grid_mode, full=False)
            _, hf, hb = _s5(ps, s0f, s0b, s5m, s5_d[li])
            return None, hf, hb
        pf, qv, qx1, qx2, ps = _inproj(
            xs, g_mix, sh, sc, w_in[li].astype(BF16), hy_conv_w[li].astype(F32),
            hy_conv_b[li].astype(F32).reshape(1, -1), grid_mode=grid_mode, full=True)
        tabs = dft[n_tok]
        amat, bmat = _fnet_maps(fnet_w[li], n_tok)
        yf = _fnet(pf, *tabs['fn'], amat, bmat)
        spectra = _hyena_spectra(n_tok, tabs['hy_fwd'], hy_fw1[li], hy_fb1[li], hy_freq1[li], hy_fw2[li],
                                 hy_fb2[li], hy_freq2[li], hy_fw3[li])
        z = qv
        for o, xg in enumerate((qx1, qx2)):
            pr, pi = _hy_fwd(z, tabs['hy_fwd'], spectra[o])
            z = _hy_inv(pr, pi, tabs['hy_inv'], xg, z, hy_bias[li][o].astype(F32))
        ys, hf, hb = _s5(ps, s0f, s0b, s5m, s5_d[li])
        return (yf, z, ys), hf, hb

    def out_proj(li, parts, xs, gate):
        wo = w_out[li].astype(BF16)
        return _outproj(parts[0], parts[1], parts[2], xs, gate,
                        wo[:D_FNET], wo[D_FNET:D_FNET + D_HYENA], wo[D_FNET + D_HYENA:],
                        s5_glu_w[li].astype(BF16), s5_glu_b[li].astype(F32).reshape(1, -1))

    def channel_mix(li, xs, sh, sc, gate, fin):
        i = li // 2
        g_ffn = norm_ffn_g[li].reshape(1, d)
        if li % 2 == 0:
            return _ffn(xs, g_ffn, sh, sc, gate, ffn_w_gate[i].astype(BF16),
                        ffn_w_up[i].astype(BF16), ffn_w_down[i].astype(BF16), fin, tf=1408)
        return _moe(xs, g_ffn, sh, sc, gate, moe_router_w[i], moe_router_b[i], moe_w_gate[i].astype(BF16),
                    moe_w_up[i].astype(BF16), moe_w_down[i].astype(BF16), fin)

    zero_state = jnp.zeros((b, N_S5_BLOCKS, SLAB, LANES), F32)
    ctx_s = ctx
    for li in range(depth):
        last = li == depth - 1
        mod = _ada(cc, ada_w.astype(F32), li, ada_b[li].astype(F32))
        lat = [m.reshape(b, 1, d) for m in jnp.split(mod[:b], N_MOD, axis=-1)]
        cm = [jnp.broadcast_to(m.reshape(1, 1, d), (b, 1, d)) for m in jnp.split(mod[b:b + 1], N_MOD, axis=-1)]
        s5m = _s5_matrices(s5_a_re[li], s5_a_im[li], s5_log_dt[li], s5_b_re[li], s5_b_im[li],
                           s5_c_re[li], s5_c_im[li])

        parts, hf_c, hb_c = mix(li, ctx_s, cm[0], cm[1], False, zero_state, zero_state, s5m, last)
        if not last:
            ctx_s = out_proj(li, parts, ctx_s, cm[2])
            ctx_s = channel_mix(li, ctx_s, cm[3], cm[4], cm[5], None)

        parts, _, _ = mix(li, x, lat[0], lat[1], True, hf_c, hb_c, s5m, False)
        x = out_proj(li, parts, x, lat[2])
        x = channel_mix(li, x, lat[3], lat[4], lat[5], final_g.reshape(1, d) if last else None)

    if depth == 0:
        raise ValueError("depth must be positive")
    return x
```

```python
import functools
import math

import numpy as np
import jax
import jax.numpy as jnp
from jax import lax
from jax.experimental import pallas as pl
from jax.experimental.pallas import tpu as pltpu

F32 = jnp.float32
BF16 = jnp.bfloat16
HI = lax.Precision.HIGHEST

LANES = 128
GRID_ROW = 64
FNET_HEADS, FNET_HEAD_DIM = 4, 64
D_FNET = FNET_HEADS * FNET_HEAD_DIM
D_HYENA = 384
HYENA_ORDER = 2
HYENA_EMB = 33
HYENA_BANDS = (HYENA_EMB - 1) // 2
HYENA_FAST_DECAY_PCT, HYENA_SLOW_DECAY_PCT, HYENA_DECAY_TARGET = 0.3, 1.5, 0.01
S5_GROUP, S5_GROUPS, S5_STATE = 16, 24, 64
D_S5 = S5_GROUP * S5_GROUPS
HY_OFF = D_FNET
S5_OFF = D_FNET + (HYENA_ORDER + 1) * D_HYENA
N_MOD = 6
TOP_K = 2
RMS_EPS = 1e-6
S5_CHUNK = 16
GROUPS_PER_BLOCK = LANES // S5_GROUP
N_S5_BLOCKS = D_S5 // LANES
STATE_HALF = GROUPS_PER_BLOCK * S5_STATE
NEG_BIG = float(np.finfo(np.float32).min)
VMEM_LIMIT = 56 * 1024 * 1024


def _cp(sem, vmem=None):
    return pltpu.CompilerParams(dimension_semantics=sem, vmem_limit_bytes=vmem)


def _dot(a, b):
    return jnp.dot(a, b, preferred_element_type=F32)


def _split_bf16(a):
    hi = a.astype(BF16)
    lo = (a - hi.astype(F32)).astype(BF16)
    return hi, lo


def _sigmoid(x):
    return 1.0 / (1.0 + jnp.exp(-x))


def _rms_mod(x, g, sh, sc):
    ms = jnp.mean(x * x, axis=-1, keepdims=True)
    return (x * lax.rsqrt(ms + RMS_EPS) * g) * (1.0 + sc) + sh


def _ada_kernel(c_ref, w_ref, b_ref, o_ref):
    c = c_ref[...]
    s = c * _sigmoid(c)
    s_hi, s_lo = _split_bf16(s)
    w_hi, w_lo = _split_bf16(w_ref[...])
    o_ref[...] = _dot(s_hi, w_hi) + _dot(s_hi, w_lo) + _dot(s_lo, w_hi) + b_ref[...]


def _ada(cc, w_all, li, b):
    r, d = cc.shape
    n = w_all.shape[2]
    tn = 512
    return pl.pallas_call(
        _ada_kernel,
        out_shape=jax.ShapeDtypeStruct((r, n), F32),
        grid=(n // tn,),
        in_specs=[pl.BlockSpec((r, d), lambda j: (0, 0)),
                  pl.BlockSpec((None, d, tn), lambda j: (li, 0, j)),
                  pl.BlockSpec((1, tn), lambda j: (0, j))],
        out_specs=pl.BlockSpec((r, tn), lambda j: (0, j)),
        compiler_params=_cp(("parallel",)),
        name="ada_mod",
    )(cc, w_all, b.reshape(1, n))


def _expand_kernel(a1_ref, a2_ref, b1_ref, b2_ref, o_ref, *, nh):
    a2 = a2_ref[...]
    b2 = b2_ref[...]
    for h in range(nh):
        blk = a1_ref[:, h:h + 1] * a2 + b1_ref[:, h:h + 1] * b2
        o_ref[:, h * LANES:(h + 1) * LANES] = blk.astype(o_ref.dtype)


def _expand(a1, a2, b1, b2):
    r, nh = a1.shape
    tr = min(r, 256)
    return pl.pallas_call(
        functools.partial(_expand_kernel, nh=nh),
        out_shape=jax.ShapeDtypeStruct((r, nh * LANES), BF16),
        grid=(r // tr,),
        in_specs=[pl.BlockSpec((tr, nh), lambda i: (i, 0)),
                  pl.BlockSpec((tr, LANES), lambda i: (i, 0)),
                  pl.BlockSpec((tr, nh), lambda i: (i, 0)),
                  pl.BlockSpec((tr, LANES), lambda i: (i, 0))],
        out_specs=pl.BlockSpec((tr, nh * LANES), lambda i: (i, 0)),
        compiler_params=_cp(("parallel",)),
        name="dft_expand",
    )(a1, a2, b1, b2)


def _fnet_dft(l):
    h = l // 2
    nh = h // LANES
    period = 2 * l
    k = np.arange(l, dtype=np.int64)[:, None]
    qa = (k * (2 * LANES) * np.arange(nh, dtype=np.int64)[None, :]) % period
    qb = (k * (2 * np.arange(LANES, dtype=np.int64)[None, :] + 1)) % period
    aa = 2.0 * np.pi * qa / period
    ab = 2.0 * np.pi * qb / period
    f = lambda t: jnp.asarray(t, F32)
    c1, s1, c2, s2 = np.cos(aa), np.sin(aa), np.cos(ab), np.sin(ab)
    phase = np.zeros((l, LANES), np.float32)
    phase[:, 0] = np.cos(np.pi * k[:, 0] / l)
    phase[:, 1] = np.sin(np.pi * k[:, 0] / l)
    return _expand(f(c1), f(c2), f(-s1), f(s2)), _expand(f(-s1), f(c2), f(-c1), f(s2)), jnp.asarray(phase)


def _hyena_dft(l):
    h = l // 2
    nh = h // LANES
    period = 4 * l
    kp = np.arange(h, dtype=np.int64)[:, None]
    f = lambda t: jnp.asarray(t, F32)
    out = {}
    for name, k in (("e", 2 * kp), ("o", 2 * kp + 1)):
        qa = (k * (2 * LANES) * np.arange(nh, dtype=np.int64)[None, :]) % period
        qb = (k * (2 * np.arange(LANES, dtype=np.int64)[None, :] + 1)) % period
        aa = 2.0 * np.pi * qa / period
        ab = 2.0 * np.pi * qb / period
        c1, s1, c2, s2 = np.cos(aa), np.sin(aa), np.cos(ab), np.sin(ab)
        a1, a2, b1 = -s1, c2.copy(), -c1
        if name == "e":
            a1[0, :] = 1.0
            a2[0, :] = (-1.0) ** np.arange(LANES)
            b1[0, :] = 0.0
        out["c" + name] = _expand(f(c1), f(c2), f(-s1), f(s2))
        out["s" + name] = _expand(f(a1), f(a2), f(b1), f(s2))
    return out


def _flip_matrix(n):
    return jnp.asarray(np.eye(n, dtype=np.float32)[::-1], BF16)


def _flip_rows(v, j):
    sb = j.shape[0]
    nblk = v.shape[0] // sb
    v_hi, v_lo = _split_bf16(v)
    parts = []
    for s in range(nblk):
        r = slice((nblk - 1 - s) * sb, (nblk - s) * sb)
        parts.append(_dot(j, v_hi[r]) + _dot(j, v_lo[r]))
    return parts[0] if nblk == 1 else jnp.concatenate(parts, axis=0)


def _inproj_kernel(x_ref, g_ref, sh_ref, sc_ref, w_ref, cw_ref, cb_ref, j_ref, *out_refs, tm, nt, grid_mode, full):
    h = _rms_mod(x_ref[...], g_ref[...], sh_ref[...], sc_ref[...])
    acc = _dot(h.astype(BF16), w_ref[...])
    if not full:
        out_refs[0][...] = acc
        return
    pf_ref, qv_ref, qx1_ref, qx2_ref, ps_ref = out_refs
    ps_ref[...] = acc[:, S5_OFF:]
    hy = acc[:, HY_OFF:S5_OFF]
    row = lax.broadcasted_iota(jnp.int32, (tm, 1), 0)
    if grid_mode:
        first = (row % GRID_ROW) == 0
        last = (row % GRID_ROW) == GRID_ROW - 1
    else:
        first = row == 0
        last = row == tm - 1
    prev = jnp.where(first, 0.0, pltpu.roll(hy, 1, 0))
    nxt = jnp.where(last, 0.0, pltpu.roll(hy, tm - 1, 0))
    q = prev * cw_ref[0:1, :] + hy * cw_ref[1:2, :] + nxt * cw_ref[2:3, :] + cb_ref[...]
    q = jnp.concatenate([acc[:, :HY_OFF], q], axis=1)
    folded = (pf_ref, qv_ref, qx1_ref, qx2_ref)
    edges = (0, D_FNET, D_FNET + D_HYENA, D_FNET + 2 * D_HYENA, D_FNET + 3 * D_HYENA)
    cols = [slice(edges[n], edges[n + 1]) for n in range(len(folded))]
    if nt == 1:
        half = tm // 2
        q_hi = _flip_rows(q[half:], j_ref[...])
        for r, c in zip(folded, cols):
            r[0] = q[:half, c]
            r[1] = q_hi[:, c]
        return
    i = pl.program_id(1)

    @pl.when(i < nt // 2)
    def _():
        for r, c in zip(folded, cols):
            r[...] = q[:, c]

    @pl.when(i >= nt // 2)
    def _():
        qf = _flip_rows(q, j_ref[...])
        for r, c in zip(folded, cols):
            r[...] = qf[:, c]


def _fold_spec(tm, nt, width):
    if nt == 1:
        return pl.BlockSpec((None, 2, tm // 2, width), lambda bi, i: (bi, 0, 0, 0))
    hn = nt // 2
    return pl.BlockSpec((None, None, tm, width),
                        lambda bi, i: (bi, i // hn, jnp.where(i < hn, i, nt - 1 - i), 0))


def _inproj(x, g, sh, sc, w, cw, cb, *, grid_mode, full):
    b, l, d = x.shape
    n = w.shape[1]
    tm = min(l, 512)
    nt = l // tm
    if not grid_mode:
        assert tm == l
    else:
        assert tm % GRID_ROW == 0
    assert nt == 1 or nt % 2 == 0
    row = lambda bi, i: (bi, i, 0)
    vec = lambda bi, i: (bi, 0, 0)
    const = lambda bi, i: (0, 0)
    flip = _flip_matrix(min(256, tm // 2 if nt == 1 else tm))
    plain = lambda wd: (jax.ShapeDtypeStruct((b, l, wd), F32), pl.BlockSpec((None, tm, wd), row))
    fold = lambda wd: (jax.ShapeDtypeStruct((b, 2, l // 2, wd), F32), _fold_spec(tm, nt, wd))
    outs = ([fold(D_FNET), fold(D_HYENA), fold(D_HYENA), fold(D_HYENA), plain(D_S5)] if full else [plain(n)])
    return pl.pallas_call(
        functools.partial(_inproj_kernel, tm=tm, nt=nt, grid_mode=grid_mode, full=full),
        out_shape=tuple(o[0] for o in outs),
        grid=(b, nt),
        in_specs=[pl.BlockSpec((None, tm, d), row),
                  pl.BlockSpec((1, d), const),
                  pl.BlockSpec((None, 1, d), vec),
                  pl.BlockSpec((None, 1, d), vec),
                  pl.BlockSpec((d, n), const),
                  pl.BlockSpec(cw.shape, const),
                  pl.BlockSpec(cb.shape, const),
                  pl.BlockSpec(flip.shape, const)],
        out_specs=tuple(o[1] for o in outs),
        compiler_params=_cp(("parallel", "parallel"), VMEM_LIMIT),
        name="inproj",
    )(x, g, sh, sc, w, cw, cb, flip)


def _fnet_kernel(x_ref, mc_ref, ms_ref, ph_ref, a_ref, b_ref, o_ref, accr, acci, *, nb):
    k = pl.program_id(2)

    @pl.when(k == 0)
    def _():
        accr[...] = jnp.zeros_like(accr)
        acci[...] = jnp.zeros_like(acci)

    mc = mc_ref[...]
    ms = ms_ref[...]
    for bi in range(nb):
        lo = x_ref[bi, 0]
        hi = x_ref[bi, 1]
        accr[bi] += _dot(mc, (lo + hi).astype(BF16))
        acci[bi] += _dot(ms, (lo - hi).astype(BF16))

    @pl.when(k == pl.num_programs(2) - 1)
    def _():
        ck = ph_ref[:, 0:1]
        sk = ph_ref[:, 1:2]
        for bi in range(nb):
            zr, zi = accr[bi], acci[bi]
            cos_part = ck * zr - sk * zi
            sin_part = -(sk * zr + ck * zi)
            o_ref[bi] = (_dot(cos_part.astype(BF16), a_ref[...])
                         + _dot(sin_part.astype(BF16), b_ref[...]))


def _fnet(pf, mc, ms, phase, amat, bmat):
    b, _, h, c = pf.shape
    l = 2 * h
    nb = min(b, 2)
    t = min(l, 1024)
    tk = min(h, 1024)
    return pl.pallas_call(
        functools.partial(_fnet_kernel, nb=nb),
        out_shape=jax.ShapeDtypeStruct((b, l, c), F32),
        grid=(b // nb, l // t, h // tk),
        in_specs=[pl.BlockSpec((nb, 2, tk, c), lambda bb, i, k: (bb, 0, k, 0)),
                  pl.BlockSpec((t, tk), lambda bb, i, k: (i, k)),
                  pl.BlockSpec((t, tk), lambda bb, i, k: (i, k)),
                  pl.BlockSpec((t, LANES), lambda bb, i, k: (i, 0)),
                  pl.BlockSpec((c, c), lambda bb, i, k: (0, 0)),
                  pl.BlockSpec((c, c), lambda bb, i, k: (0, 0))],
        out_specs=pl.BlockSpec((nb, t, c), lambda bb, i, k: (bb, i, 0)),
        scratch_shapes=[pltpu.VMEM((nb, t, c), F32), pltpu.VMEM((nb, t, c), F32)],
        compiler_params=_cp(("parallel", "parallel", "arbitrary"), VMEM_LIMIT),
        name="fnet_dft",
    )(pf, mc, ms, phase, amat, bmat)


def _hy_fwd_kernel(z_ref, ce_ref, se_ref, co_ref, so_ref, *rest, nb, filtered):
    if filtered:
        kf_ref, pr_ref, pi_ref, acc = rest
    else:
        pr_ref, pi_ref, acc = rest
    k = pl.program_id(2)

    @pl.when(k == 0)
    def _():
        acc[...] = jnp.zeros_like(acc)

    c = z_ref.shape[-1]
    ev = jnp.concatenate([(z_ref[bi, 0] + z_ref[bi, 1]).astype(BF16) for bi in range(nb)], axis=1)
    od = jnp.concatenate([(z_ref[bi, 0] - z_ref[bi, 1]).astype(BF16) for bi in range(nb)], axis=1)
    acc[0] += _dot(ce_ref[...], ev)
    acc[1] += _dot(se_ref[...], od)
    acc[2] += _dot(co_ref[...], od)
    acc[3] += _dot(so_ref[...], ev)

    @pl.when(k == pl.num_programs(2) - 1)
    def _():
        for bi in range(nb):
            cols = slice(bi * c, (bi + 1) * c)
            for p in range(2):
                zr, zi = acc[2 * p, :, cols], acc[2 * p + 1, :, cols]
                if filtered:
                    kr, ki, kd = kf_ref[p, 0], kf_ref[p, 1], kf_ref[p, 2]
                    zr, zi = zr * kr - zi * ki, zr * ki + zi * kd
                pr_ref[bi, p] = zr.astype(pr_ref.dtype)
                pi_ref[bi, p] = zi.astype(pi_ref.dtype)


def _hy_tiles(h):
    return min(h, 1024), min(h, 512)


def _once(block_shape, index_map):
    return pl.BlockSpec(block_shape, index_map, pipeline_mode=pl.Buffered(1))


def _hy_fwd(z, mats, kf, out_dtype=BF16):
    b, _, h, c = z.shape
    nb = min(b, 2)
    ti, tk = _hy_tiles(h)
    mspec = pl.BlockSpec((ti, tk), lambda i, bb, k: (i, k))
    ospec = pl.BlockSpec((nb, 2, ti, c), lambda i, bb, k: (bb, 0, i, 0))
    filtered = kf is not None
    extra_specs = [_once((2, 3, ti, c), lambda i, bb, k: (0, 0, i, 0))] if filtered else []
    extra_args = (kf,) if filtered else ()
    return pl.pallas_call(
        functools.partial(_hy_fwd_kernel, nb=nb, filtered=filtered),
        out_shape=(jax.ShapeDtypeStruct((b, 2, h, c), out_dtype), jax.ShapeDtypeStruct((b, 2, h, c), out_dtype)),
        grid=(h // ti, b // nb, h // tk),
        in_specs=[pl.BlockSpec((nb, 2, tk, c), lambda i, bb, k: (bb, 0, k, 0)),
                  mspec, mspec, mspec, mspec] + extra_specs,
        out_specs=(ospec, ospec),
        scratch_shapes=[pltpu.VMEM((4, ti, nb * c), F32)],
        compiler_params=_cp(("parallel", "parallel", "arbitrary"), VMEM_LIMIT),
        name="hyena_fwd_dft",
    )(z, mats["ce"], mats["se"], mats["co"], mats["so"], *extra_args)


def _hy_inv_kernel(pr_ref, pi_ref, ce_ref, se_ref, co_ref, so_ref, xg_ref, zin_ref, bias_ref, o_ref, acc, *, nb):
    k = pl.program_id(2)

    @pl.when(k == 0)
    def _():
        acc[...] = jnp.zeros_like(acc)

    c = xg_ref.shape[-1]
    side = lambda ref, p: jnp.concatenate([ref[bi, p] for bi in range(nb)], axis=1)
    acc[0] += _dot(ce_ref[...], side(pr_ref, 0)) + _dot(so_ref[...], side(pi_ref, 1))
    acc[1] += _dot(se_ref[...], side(pi_ref, 0)) + _dot(co_ref[...], side(pr_ref, 1))

    @pl.when(k == pl.num_programs(2) - 1)
    def _():
        bias = bias_ref[...]
        for bi in range(nb):
            cols = slice(bi * c, (bi + 1) * c)
            u, v = acc[0, :, cols], acc[1, :, cols]
            o_ref[bi, 0] = xg_ref[bi, 0] * (u + v + zin_ref[bi, 0] * bias)
            o_ref[bi, 1] = xg_ref[bi, 1] * (u - v + zin_ref[bi, 1] * bias)


def _hy_inv(pr, pi, mats_t, xg, zin, bias):
    b, _, h, c = xg.shape
    nb = min(b, 2)
    ti, tk = min(h, 512), min(h, 1024)
    pspec = pl.BlockSpec((nb, 2, tk, c), lambda bb, i, k: (bb, 0, k, 0))
    mspec = pl.BlockSpec((ti, tk), lambda bb, i, k: (i, k))
    ospec = pl.BlockSpec((nb, 2, ti, c), lambda bb, i, k: (bb, 0, i, 0))
    return pl.pallas_call(
        functools.partial(_hy_inv_kernel, nb=nb),
        out_shape=jax.ShapeDtypeStruct((b, 2, h, c), F32),
        grid=(b // nb, h // ti, h // tk),
        in_specs=[pspec, pspec, mspec, mspec, mspec, mspec, ospec, ospec,
                  pl.BlockSpec((1, c), lambda bb, i, k: (0, 0))],
        out_specs=ospec,
        scratch_shapes=[pltpu.VMEM((2, ti, nb * c), F32)],
        compiler_params=_cp(("parallel", "parallel", "arbitrary"), VMEM_LIMIT),
        name="hyena_inv_dft",
    )(pr, pi, mats_t["ce"], mats_t["se"], mats_t["co"], mats_t["so"], xg, zin, bias.reshape(1, c))


def _s5_kernel(u_ref, s0f_ref, s0b_ref, mi_ref, mof_ref, mob_ref, mnf_ref, mnb_ref, atf_ref, atb_ref,
               d_ref, y_ref, hf_ref, hb_ref, x_scr, sf_scr, sb_scr, hinf_scr, hinb_scr, *, nc, t_chunk, nbb):
    for bi in range(nbb):
        for s in range(t_chunk):
            x_scr[bi * nc:(bi + 1) * nc, s * LANES:(s + 1) * LANES] = (
                u_ref[bi, pl.ds(s, nc, stride=t_chunk), :].astype(BF16))
    x = x_scr[...]
    rows_all = nbb * nc
    _to_slabs(sf_scr, _dot(x, mof_ref[...]), rows_all)
    _to_slabs(sb_scr, _dot(x, mob_ref[...]), rows_all)
    cf1, cf2 = atf_ref[0], atf_ref[1]
    cb1, cb2 = atb_ref[0], atb_ref[1]

    def body(c, carry):
        out = []
        for bi in range(nbb):
            hf, hb = carry[2 * bi], carry[2 * bi + 1]
            rf = pl.ds(pl.multiple_of((bi * nc + c) * SLAB, SLAB), SLAB)
            rb = pl.ds(pl.multiple_of((bi * nc + nc - 1 - c) * SLAB, SLAB), SLAB)
            hinf_scr[rf, :] = hf
            hinb_scr[rb, :] = hb
            out.append(cf1 * hf + cf2 * pltpu.roll(hf, SLAB // 2, 0) + sf_scr[rf, :])
            out.append(cb1 * hb + cb2 * pltpu.roll(hb, SLAB // 2, 0) + sb_scr[rb, :])
        return tuple(out)

    init = []
    for bi in range(nbb):
        init += [s0f_ref[bi], s0b_ref[bi]]
    fin = lax.fori_loop(0, nc, body, tuple(init), unroll=4 if nbb == 1 else 1)
    for bi in range(nbb):
        hf_ref[bi] = fin[2 * bi]
        hb_ref[bi] = fin[2 * bi + 1]
    y = (_dot(x, mi_ref[...])
         + _dot(_from_slabs(hinf_scr, rows_all).astype(BF16), mnf_ref[...])
         + _dot(_from_slabs(hinb_scr, rows_all).astype(BF16), mnb_ref[...]))
    d = d_ref[...]
    for bi in range(nbb):
        for t in range(t_chunk):
            rows = pl.ds(t, nc, stride=t_chunk)
            y_ref[bi, rows, :] = (y[bi * nc:(bi + 1) * nc, t * LANES:(t + 1) * LANES] + d * u_ref[bi, rows, :])


def _s5(u, s0f, s0b, mats, d_skip):
    b, l, _ = u.shape
    tc = S5_CHUNK
    nc = l // tc
    kin = tc * LANES
    ns = 2 * STATE_HALF
    mi, mof, mob, mnf, mnb, atf, atb = mats
    nbb = b if l <= 1024 else 1
    st_spec = pl.BlockSpec((nbb, None, SLAB, LANES), lambda j, bi: (bi, j, 0, 0))
    w3 = lambda s1, s2: pl.BlockSpec((None, s1, s2), lambda j, bi: (j, 0, 0), pipeline_mode=pl.Buffered(1))
    coef = pl.BlockSpec((None, 2, SLAB, LANES), lambda j, bi: (j, 0, 0, 0))
    u_spec = pl.BlockSpec((nbb, l, LANES), lambda j, bi: (bi, 0, j))
    state = jax.ShapeDtypeStruct((b, N_S5_BLOCKS, SLAB, LANES), F32)
    return pl.pallas_call(
        functools.partial(_s5_kernel, nc=nc, t_chunk=tc, nbb=nbb),
        out_shape=(jax.ShapeDtypeStruct((b, l, D_S5), F32), state, state),
        grid=(N_S5_BLOCKS, b // nbb),
        in_specs=[u_spec, st_spec, st_spec,
                  w3(kin, kin), w3(kin, ns), w3(kin, ns), w3(ns, kin), w3(ns, kin),
                  coef, coef,
                  pl.BlockSpec((1, LANES), lambda j, bi: (0, j))],
        out_specs=(u_spec, st_spec, st_spec),
        scratch_shapes=([pltpu.VMEM((nbb * nc, kin), BF16)]
                        + [pltpu.VMEM((nbb * nc * SLAB, LANES), F32)] * 4),
        compiler_params=_cp(("parallel", "parallel"), VMEM_LIMIT),
        name="s5_scan",
    )(u, s0f, s0b, mi, mof, mob, mnf, mnb, atf, atb, d_skip.reshape(1, D_S5))


def _s5_matrices(a_re, a_im, log_dt, b_re, b_im, c_re, c_im):
    tc = S5_CHUNK
    g, p, hd = S5_GROUPS, S5_STATE, S5_GROUP
    lam = lax.complex(a_re.astype(F32), a_im.astype(F32))
    dt = jnp.exp(log_dt.astype(F32))[..., None]
    a_bar = jnp.exp(lam * dt)
    b_bar = ((a_bar - 1.0) / lam)[..., None] * lax.complex(b_re.astype(F32), b_im.astype(F32))
    cc = lax.complex(c_re.astype(F32), c_im.astype(F32))
    ks = jnp.arange(tc + 1, dtype=F32)
    pw = jnp.exp((lam * dt)[:, :, None, :] * ks[None, None, :, None])
    kern = jnp.real(jnp.einsum('dgip,dgkp,dgpj->dgkij', cc, pw[:, :, :tc], b_bar, precision=HI))
    lag = np.arange(tc)[None, :] - np.arange(tc)[:, None]
    sel = np.stack([lag[None] == np.arange(tc)[:, None, None], -lag[None] == np.arange(tc)[:, None, None]])
    sel = jnp.asarray(sel.astype(np.float32))
    kern_j = kern.reshape(2, N_S5_BLOCKS, GROUPS_PER_BLOCK, tc, hd, hd)
    kt = jnp.einsum('dJgkij,dkst->Jsgjti', kern_j, sel, precision=HI)
    def place(slab, inner, row_group_width):
        width = slab.shape[-1]
        src = np.arange(width)
        e = np.zeros((width, GROUPS_PER_BLOCK * width), np.float32)
        for hh in range(GROUPS_PER_BLOCK):
            e[src, (src // inner) * GROUPS_PER_BLOCK * inner + hh * inner + src % inner] = 1.0
        wide = jnp.einsum('Jrk,kc->Jrc', slab, jnp.asarray(e), precision=HI)
        rows = np.arange(wide.shape[1])[:, None]
        cols = np.arange(wide.shape[2])[None, :]
        keep = (rows // row_group_width) % GROUPS_PER_BLOCK == (cols // inner) % GROUPS_PER_BLOCK
        return jnp.where(jnp.asarray(keep)[None], wide, 0.0)

    mi = place(kt.reshape(N_S5_BLOCKS, tc * LANES, tc * hd), hd, hd)

    def parts(z):
        return jnp.stack([jnp.real(z), jnp.imag(z)], axis=0)

    def out_mat(d, powers):
        z = powers[:, :, :, None] * b_bar[d][:, None, :, :]
        z = parts(z).reshape(2, N_S5_BLOCKS, GROUPS_PER_BLOCK, tc, p, hd)
        z = jnp.transpose(z, (1, 3, 2, 5, 0, 4))
        return place(z.reshape(N_S5_BLOCKS, tc * LANES, 2 * p), p, hd)

    def in_mat(d, powers):
        z = cc[d][:, None, :, :] * powers[:, :, None, :]
        z = jnp.stack([jnp.real(z), -jnp.imag(z)], axis=0)
        z = z.reshape(2, N_S5_BLOCKS, GROUPS_PER_BLOCK, tc, hd, p)
        z = jnp.transpose(z, (1, 0, 2, 5, 3, 4))
        return place(z.reshape(N_S5_BLOCKS, 2 * STATE_HALF, tc * hd), hd, p)

    rev = jnp.arange(tc - 1, -1, -1)
    mof = out_mat(0, pw[0][:, rev])
    mob = out_mat(1, pw[1][:, :tc])
    mnf = in_mat(0, pw[0][:, 1:tc + 1])
    mnb = in_mat(1, pw[1][:, tc - jnp.arange(tc)])

    def step(d):
        ar = jnp.real(pw[d][:, tc]).reshape(N_S5_BLOCKS, STATE_HALF)
        ai = jnp.imag(pw[d][:, tc]).reshape(N_S5_BLOCKS, STATE_HALF)
        c1 = jnp.concatenate([ar, ar], axis=1)
        c2 = jnp.concatenate([-ai, ai], axis=1)
        return jnp.stack([c1, c2], axis=1).reshape(N_S5_BLOCKS, 2, SLAB, LANES)

    bf = lambda m: m.astype(BF16)
    return bf(mi), bf(mof), bf(mob), bf(mnf), bf(mnb), step(0), step(1)


def _outproj_kernel(yf_ref, yh_ref, ys_ref, x_ref, gate_ref, w1_ref, w2_ref, w3_ref, gw_ref, gb_ref, j_ref, o_ref,
                    *, nt):
    y = ys_ref[...]
    y = 0.5 * y * (1.0 + jnp.tanh(math.sqrt(2.0 / math.pi) * (y + 0.044715 * (y * y * y))))
    y = y * _sigmoid(_dot(y.astype(BF16), gw_ref[...]) + gb_ref[...])
    acc = _dot(yf_ref[...].astype(BF16), w1_ref[...]) + _dot(y.astype(BF16), w3_ref[...])

    def finish(yh):
        o_ref[...] = x_ref[...] + gate_ref[...] * (acc + _dot(yh.astype(BF16), w2_ref[...]))

    if nt == 1:
        finish(jnp.concatenate([yh_ref[0], _flip_rows(yh_ref[1], j_ref[...])], axis=0))
        return
    i = pl.program_id(1)

    @pl.when(i < nt // 2)
    def _():
        finish(yh_ref[...])

    @pl.when(i >= nt // 2)
    def _():
        finish(_flip_rows(yh_ref[...], j_ref[...]))


def _outproj(yf, yh, ys, x, gate, w1, w2, w3, gw, gb):
    b, l, d = x.shape
    tm = min(l, 512)
    nt = l // tm
    row = lambda bi, i: (bi, i, 0)
    const = lambda bi, i: (0, 0)
    full = lambda a: pl.BlockSpec(a.shape, const)
    flip = _flip_matrix(min(256, tm // 2 if nt == 1 else tm))
    return pl.pallas_call(
        functools.partial(_outproj_kernel, nt=nt),
        out_shape=jax.ShapeDtypeStruct((b, l, d), F32),
        grid=(b, nt),
        in_specs=[pl.BlockSpec((None, tm, yf.shape[2]), row),
                  _fold_spec(tm, nt, yh.shape[3]),
                  pl.BlockSpec((None, tm, ys.shape[2]), row),
                  pl.BlockSpec((None, tm, d), row),
                  pl.BlockSpec((None, 1, d), lambda bi, i: (bi, 0, 0)),
                  full(w1), full(w2), full(w3), full(gw), full(gb), full(flip)],
        out_specs=pl.BlockSpec((None, tm, d), row),
        compiler_params=_cp(("parallel", "parallel"), VMEM_LIMIT),
        name="outproj",
    )(yf, yh, ys, x, gate, w1, w2, w3, gw, gb, flip)


def _ffn_kernel(x_ref, g_ref, sh_ref, sc_ref, gate_ref, wg_ref, wu_ref, wd_ref, fg_ref, o_ref, xn_scr, acc_scr,
                *, final_norm):
    f = pl.program_id(2)

    @pl.when(f == 0)
    def _():
        xn_scr[...] = _rms_mod(x_ref[...], g_ref[...], sh_ref[...], sc_ref[...]).astype(BF16)
        acc_scr[...] = jnp.zeros_like(acc_scr)

    xn = xn_scr[...]
    gt = _dot(xn, wg_ref[...])
    up = _dot(xn, wu_ref[...])
    acc_scr[...] += _dot((gt * _sigmoid(gt) * up).astype(BF16), wd_ref[...])

    @pl.when(f == pl.num_programs(2) - 1)
    def _():
        y = x_ref[...] + gate_ref[...] * acc_scr[...]
        if final_norm:
            ms = jnp.mean(y * y, axis=-1, keepdims=True)
            y = y * lax.rsqrt(ms + RMS_EPS) * fg_ref[...]
        o_ref[...] = y


def _ffn(x, g, sh, sc, gate, wg, wu, wd, final_g, *, tf):
    b, l, d = x.shape
    ff = wg.shape[1]
    final_norm = final_g is not None
    tm = min(l, 1024)
    fg = final_g if final_norm else jnp.ones((1, d), F32)
    row = lambda bi, i, f: (bi, i, 0)
    vec = lambda bi, i, f: (bi, 0, 0)
    const = lambda bi, i, f: (0, 0)
    return pl.pallas_call(
        functools.partial(_ffn_kernel, final_norm=final_norm),
        out_shape=jax.ShapeDtypeStruct((b, l, d), F32),
        grid=(b, l // tm, ff // tf),
        in_specs=[pl.BlockSpec((None, tm, d), row),
                  pl.BlockSpec((1, d), const),
                  pl.BlockSpec((None, 1, d), vec),
                  pl.BlockSpec((None, 1, d), vec),
                  pl.BlockSpec((None, 1, d), vec),
                  pl.BlockSpec((d, tf), lambda bi, i, f: (0, f)),
                  pl.BlockSpec((d, tf), lambda bi, i, f: (0, f)),
                  pl.BlockSpec((tf, d), lambda bi, i, f: (f, 0)),
                  pl.BlockSpec((1, d), const)],
        out_specs=pl.BlockSpec((None, tm, d), row),
        scratch_shapes=[pltpu.VMEM((tm, d), BF16), pltpu.VMEM((tm, d), F32)],
        compiler_params=_cp(("parallel", "parallel", "arbitrary"), VMEM_LIMIT),
        name="dense_ffn",
    )(x, g, sh, sc, gate, wg, wu, wd, fg)


SLAB = 8
ROUTE_TM = 512
GROUP_TM = 1024


def _to_slabs(ref, val, n):
    for s in range(SLAB):
        ref[pl.ds(s, n, stride=SLAB), :] = val[:, s * LANES:(s + 1) * LANES]


def _from_slabs(ref, n):
    return jnp.concatenate([ref[pl.ds(s, n, stride=SLAB), :] for s in range(SLAB)], axis=1)


def _route_kernel(x_ref, g_ref, sh_ref, sc_ref, rwh_ref, rwl_ref, rb_ref, tri_ref,
                  xs_ref, meta_ref, gate_ref, cnt_ref, carry_scr, *, tm):
    step = pl.program_id(0) * pl.num_programs(1) + pl.program_id(1)

    @pl.when(step == 0)
    def _():
        carry_scr[...] = jnp.zeros_like(carry_scr)

    h = _rms_mod(x_ref[...], g_ref[...], sh_ref[...], sc_ref[...])
    _to_slabs(xs_ref, h, tm)
    h_hi, h_lo = _split_bf16(h)
    rwh = rwh_ref[...]
    lg = _dot(h_hi, rwh) + _dot(h_hi, rwl_ref[...]) + _dot(h_lo, rwh) + rb_ref[...]
    lane = lax.broadcasted_iota(jnp.int32, lg.shape, 1)
    m1 = jnp.max(lg, axis=-1, keepdims=True)
    i1 = jnp.min(jnp.where(lg == m1, lane, LANES), axis=-1, keepdims=True)
    lg2 = jnp.where(lane == i1, NEG_BIG, lg)
    m2 = jnp.max(lg2, axis=-1, keepdims=True)
    i2 = jnp.min(jnp.where(lg2 == m2, lane, LANES), axis=-1, keepdims=True)
    ex = jnp.exp(m2 - m1)
    den = 1.0 + ex
    gate_ref[...] = jnp.where(lane == 0, 1.0 / den, jnp.where(lane == 1, ex / den, 0.0))
    sel1 = lane == i1
    sel2 = lane == i2
    tri = tri_ref[...]
    p1 = _dot(tri, sel1.astype(BF16))
    p2 = _dot(tri, sel2.astype(BF16))
    tot1 = p1[tm - 1:tm, :]
    tot2 = p2[tm - 1:tm, :]
    carry = carry_scr[...]
    r1 = jnp.sum(jnp.where(sel1, carry + p1 - 1.0, 0.0), axis=-1, keepdims=True)
    r2 = jnp.sum(jnp.where(sel2, carry + tot1 + p2 - 1.0, 0.0), axis=-1, keepdims=True)
    carry = carry + tot1 + tot2
    carry_scr[...] = carry
    cnt_ref[...] = carry.astype(jnp.int32)
    meta_ref[...] = jnp.where(lane == 0, i1, jnp.where(lane == 1, i2, jnp.where(
        lane == 2, r1.astype(jnp.int32), jnp.where(lane == 3, r2.astype(jnp.int32), 0))))


def _route(x, g, sh, sc, rwh, rwl, rb):
    b, l, d = x.shape
    tm = min(l, ROUTE_TM)
    n = b * l
    nl = l // tm
    tri = jnp.tril(jnp.ones((tm, tm), F32)).astype(BF16)
    row = lambda bi, i: (bi, i, 0)
    vec = lambda bi, i: (bi, 0, 0)
    const = lambda bi, i: (0, 0)
    flat = lambda bi, i: (bi * nl + i, 0)
    return pl.pallas_call(
        functools.partial(_route_kernel, tm=tm),
        out_shape=(jax.ShapeDtypeStruct((n * SLAB, LANES), F32),
                   jax.ShapeDtypeStruct((n, LANES), jnp.int32),
                   jax.ShapeDtypeStruct((n, LANES), F32),
                   jax.ShapeDtypeStruct((1, LANES), jnp.int32)),
        grid=(b, nl),
        in_specs=[pl.BlockSpec((None, tm, d), row),
                  pl.BlockSpec((1, d), const),
                  pl.BlockSpec((None, 1, d), vec),
                  pl.BlockSpec((None, 1, d), vec),
                  pl.BlockSpec((d, LANES), const),
                  pl.BlockSpec((d, LANES), const),
                  pl.BlockSpec((1, LANES), const),
                  pl.BlockSpec((tm, tm), const)],
        out_specs=(pl.BlockSpec((tm * SLAB, LANES), flat),
                   pl.BlockSpec((tm, LANES), flat),
                   pl.BlockSpec((tm, LANES), flat),
                   pl.BlockSpec((1, LANES), const)),
        scratch_shapes=[pltpu.VMEM((1, LANES), F32)],
        compiler_params=_cp(("arbitrary", "arbitrary"), VMEM_LIMIT),
        name="moe_route",
    )(x, g, sh, sc, rwh, rwl, rb, tri)


def _dispatch_kernel(pos_hbm, xs_ref, init_hbm, out_hbm, idx_smem, isem, sem, *, tm):
    del init_hbm
    step = pl.program_id(0)
    icopy = pltpu.make_async_copy(pos_hbm.at[pl.ds(step * 2 * tm, 2 * tm)], idx_smem, isem)
    icopy.start()
    icopy.wait()

    def row_copy(i, k):
        return pltpu.make_async_copy(xs_ref.at[pl.ds(i * SLAB, SLAB), :], out_hbm.at[idx_smem[2 * i + k]], sem)

    def issue(i, carry):
        row_copy(i, 0).start()
        row_copy(i, 1).start()
        return carry

    def drain(i, carry):
        row_copy(i, 0).wait()
        row_copy(i, 1).wait()
        return carry

    lax.fori_loop(0, tm, issue, 0, unroll=8)
    lax.fori_loop(0, tm, drain, 0, unroll=8)


def _dispatch(xs, pos, n_rows):
    n = xs.shape[0] // SLAB
    tm = min(n, ROUTE_TM)
    init = jnp.zeros((n_rows, SLAB, LANES), F32)
    return pl.pallas_call(
        functools.partial(_dispatch_kernel, tm=tm),
        out_shape=jax.ShapeDtypeStruct((n_rows, SLAB, LANES), F32),
        grid=(n // tm,),
        in_specs=[pl.BlockSpec(memory_space=pl.ANY),
                  pl.BlockSpec((tm * SLAB, LANES), lambda i: (i, 0)),
                  pl.BlockSpec(memory_space=pl.ANY)],
        out_specs=pl.BlockSpec(memory_space=pl.ANY),
        scratch_shapes=[pltpu.SMEM((2 * tm,), jnp.int32), pltpu.SemaphoreType.DMA, pltpu.SemaphoreType.DMA],
        input_output_aliases={2: 0},
        compiler_params=_cp(("arbitrary",), VMEM_LIMIT),
        name="moe_dispatch",
    )(pos.reshape(-1), xs, init)


def _group_ffn_kernel(te_ref, x_ref, wg_ref, wu_ref, wd_ref, o_ref, xn_scr, acc_scr, *, tm, nt):
    t = pl.program_id(0)
    f = pl.program_id(1)
    valid = t < te_ref[nt]
    last = f == pl.num_programs(1) - 1

    @pl.when(jnp.logical_and(valid, f == 0))
    def _():
        xn_scr[...] = _from_slabs(x_ref, tm).astype(BF16)
        acc_scr[...] = jnp.zeros_like(acc_scr)

    @pl.when(valid)
    def _():
        xn = xn_scr[...]
        gt = _dot(xn, wg_ref[...])
        up = _dot(xn, wu_ref[...])
        acc_scr[...] += _dot((gt * _sigmoid(gt) * up).astype(BF16), wd_ref[...])

    @pl.when(jnp.logical_and(valid, last))
    def _():
        _to_slabs(o_ref, acc_scr[...], tm)

    @pl.when(jnp.logical_and(jnp.logical_not(valid), last))
    def _():
        o_ref[...] = jnp.zeros_like(o_ref)


def _group_ffn(xs, tile_expert, wg, wu, wd, *, tf):
    n_rows = xs.shape[0]
    tm = GROUP_TM
    nt = n_rows // tm
    _, d, ff = wg.shape
    x2 = xs.reshape(n_rows * SLAB, LANES)
    rows = pl.BlockSpec((tm * SLAB, LANES), lambda t, f, te: (t, 0))
    out = pl.pallas_call(
        functools.partial(_group_ffn_kernel, tm=tm, nt=nt),
        out_shape=jax.ShapeDtypeStruct((n_rows * SLAB, LANES), F32),
        grid_spec=pltpu.PrefetchScalarGridSpec(
            num_scalar_prefetch=1,
            grid=(nt, ff // tf),
            in_specs=[rows,
                      pl.BlockSpec((None, d, tf), lambda t, f, te: (te[t], 0, f)),
                      pl.BlockSpec((None, d, tf), lambda t, f, te: (te[t], 0, f)),
                      pl.BlockSpec((None, tf, d), lambda t, f, te: (te[t], f, 0))],
            out_specs=rows,
            scratch_shapes=[pltpu.VMEM((tm, d), BF16), pltpu.VMEM((tm, d), F32)]),
        compiler_params=_cp(("arbitrary", "arbitrary"), VMEM_LIMIT),
        name="moe_group_ffn",
    )(tile_expert, x2, wg, wu, wd)
    return out.reshape(n_rows, SLAB, LANES)


def _combine_kernel(pos_hbm, ys_hbm, x_ref, gate_ref, w_ref, fg_ref, o_ref, idx_smem, ybuf, isem, sem,
                    *, tm, final_norm):
    step = pl.program_id(0) * pl.num_programs(1) + pl.program_id(1)
    icopy = pltpu.make_async_copy(pos_hbm.at[pl.ds(step * 2 * tm, 2 * tm)], idx_smem, isem)
    icopy.start()
    icopy.wait()

    def row_copy(i, k):
        return pltpu.make_async_copy(ys_hbm.at[idx_smem[2 * i + k]], ybuf.at[k, pl.ds(i * SLAB, SLAB), :], sem)

    def issue(i, carry):
        row_copy(i, 0).start()
        row_copy(i, 1).start()
        return carry

    def drain(i, carry):
        row_copy(i, 0).wait()
        row_copy(i, 1).wait()
        return carry

    lax.fori_loop(0, tm, issue, 0, unroll=8)
    lax.fori_loop(0, tm, drain, 0, unroll=8)
    w = w_ref[...]
    moe = w[:, 0:1] * _from_slabs(ybuf.at[0], tm) + w[:, 1:2] * _from_slabs(ybuf.at[1], tm)
    y = x_ref[...] + gate_ref[...] * moe
    if final_norm:
        ms = jnp.mean(y * y, axis=-1, keepdims=True)
        y = y * lax.rsqrt(ms + RMS_EPS) * fg_ref[...]
    o_ref[...] = y


def _combine(ys, pos, x, gate, w, final_g):
    b, l, d = x.shape
    tm = min(l, ROUTE_TM)
    nl = l // tm
    final_norm = final_g is not None
    fg = final_g if final_norm else jnp.ones((1, d), F32)
    row = lambda bi, i: (bi, i, 0)
    return pl.pallas_call(
        functools.partial(_combine_kernel, tm=tm, final_norm=final_norm),
        out_shape=jax.ShapeDtypeStruct((b, l, d), F32),
        grid=(b, nl),
        in_specs=[pl.BlockSpec(memory_space=pl.ANY),
                  pl.BlockSpec(memory_space=pl.ANY),
                  pl.BlockSpec((None, tm, d), row),
                  pl.BlockSpec((None, 1, d), lambda bi, i: (bi, 0, 0)),
                  pl.BlockSpec((tm, LANES), lambda bi, i: (bi * nl + i, 0)),
                  pl.BlockSpec((1, d), lambda bi, i: (0, 0))],
        out_specs=pl.BlockSpec((None, tm, d), row),
        scratch_shapes=[pltpu.SMEM((2 * tm,), jnp.int32), pltpu.VMEM((2, tm * SLAB, LANES), F32),
                        pltpu.SemaphoreType.DMA, pltpu.SemaphoreType.DMA],
        compiler_params=_cp(("arbitrary", "arbitrary"), VMEM_LIMIT),
        name="moe_combine",
    )(pos.reshape(-1), ys, x, gate, w, fg)


def _moe(x, g, sh, sc, gate, router_w, router_b, wg, wu, wd, final_g):
    b, l, d = x.shape
    n = b * l
    ne = router_w.shape[-1]
    rw = jnp.pad(router_w.astype(F32), ((0, 0), (0, LANES - ne)))
    rwh, rwl = _split_bf16(rw)
    rb = jnp.pad(router_b.astype(F32), (0, LANES - ne), constant_values=NEG_BIG).reshape(1, LANES)
    xs, meta, w, counts = _route(x, g, sh, sc, rwh, rwl, rb)
    counts = counts[0, :ne]
    padded = (counts + GROUP_TM - 1) // GROUP_TM * GROUP_TM
    ends = jnp.cumsum(padded)
    offs = ends - padded
    experts = jnp.arange(ne, dtype=jnp.int32)
    off_of = lambda e: jnp.sum(jnp.where(e[:, None] == experts[None, :], offs[None, :], 0), axis=1)
    pos = jnp.stack([off_of(meta[:, 0]) + meta[:, 2], off_of(meta[:, 1]) + meta[:, 3]], axis=1).astype(jnp.int32)
    n_rows = TOP_K * n + ne * GROUP_TM
    nt = n_rows // GROUP_TM
    tile_start = jnp.arange(nt, dtype=jnp.int32) * GROUP_TM
    tile_e = jnp.sum(tile_start[:, None] >= ends[None, :], axis=1)
    n_live = ends[-1] // GROUP_TM
    last_e = jnp.sum((n_live - 1) * GROUP_TM >= ends)
    tile_e = jnp.where(tile_start < ends[-1], tile_e, last_e)
    tile_expert = jnp.concatenate([tile_e, n_live[None]]).astype(jnp.int32)
    xd = _dispatch(xs, pos, n_rows)
    ys = _group_ffn(xd, tile_expert, wg, wu, wd, tf=512)
    return _combine(ys, pos, x, gate, w, final_g)


def _hyena_filters(l, fw1, fb1, freq1, fw2, fb2, freq2, fw3):
    half = l // 2
    pos = np.concatenate([np.arange(half), l - 1 - np.arange(half)]).astype(np.float32)[:, None]
    t = jnp.asarray(pos) / (l - 1)
    w = (2.0 * math.pi / l) * jnp.asarray(pos)
    bands = jnp.linspace(1e-4, HYENA_BANDS - 1, HYENA_BANDS, dtype=F32)[None, :]
    z = jnp.concatenate([t, jnp.cos(bands * w), -jnp.sin(bands * w)], axis=-1)
    h = jnp.sin(freq1.astype(F32) * (jnp.dot(z, fw1.astype(F32), precision=HI) + fb1.astype(F32)))
    h = jnp.sin(freq2.astype(F32) * (jnp.dot(h, fw2.astype(F32), precision=HI) + fb2.astype(F32)))
    h = jnp.dot(h, fw3.astype(F32), precision=HI).reshape(l, 2, HYENA_ORDER, D_HYENA)
    max_decay = math.log(HYENA_DECAY_TARGET) / HYENA_FAST_DECAY_PCT
    min_decay = math.log(HYENA_DECAY_TARGET) / HYENA_SLOW_DECAY_PCT
    deltas = jnp.abs(jnp.linspace(min_decay, max_decay, D_HYENA, dtype=F32))
    return h * jnp.exp(-t[:, :, None, None] * deltas)


def _hyena_spectra(l, mats_fwd, *filter_params):
    half = l // 2
    n2 = 2 * l
    h = _hyena_filters(l, *filter_params)
    tot = jnp.sum(jnp.abs(h), axis=0)
    hb0 = h[0, 1]
    denom = tot[0] + tot[1] - jnp.abs(hb0)
    sig = h.reshape(2, half, 2, HYENA_ORDER, D_HYENA).transpose(2, 3, 0, 1, 4)
    sig = sig.reshape(2 * HYENA_ORDER, 2, half, D_HYENA)
    fr, fi = _hy_fwd(sig, mats_fwd, None, out_dtype=F32)
    fr = fr.reshape(2, HYENA_ORDER, 2, half, D_HYENA)
    fi = fi.reshape(2, HYENA_ORDER, 2, half, D_HYENA)
    freq = 2 * np.arange(half)[None, :] + np.arange(2)[:, None]
    phi = np.pi * freq / (2.0 * l)
    cph = jnp.asarray(np.cos(phi), F32)[:, :, None]
    sph = jnp.asarray(np.sin(phi), F32)[:, :, None]
    re = fr * cph - fi * sph
    im = fr * sph + fi * cph
    dc_slot = jnp.asarray(freq == 0)[:, :, None]
    k_re = re[0] + re[1] - hb0[:, None, None, :]
    k_im = im[0] - im[1]
    k_nyq = fi[0] + fi[1] - hb0[:, None, None, :]
    wgt = jnp.where(dc_slot, 1.0, 2.0) / (n2 * denom[:, None, None, :])
    kr = k_re * wgt
    ki = jnp.where(dc_slot, 0.0, k_im * wgt)
    kd = jnp.where(dc_slot, k_nyq * wgt, kr)
    return [jnp.stack([kr[o], ki[o], kd[o]], axis=1) for o in range(HYENA_ORDER)]


def _fnet_maps(w_f, l):
    d = FNET_HEAD_DIM
    idx = np.arange(d)
    ang = 2.0 * np.pi * ((idx[:, None] * idx[None, :]) % d) / d
    scale = 1.0 / math.sqrt(d * l)
    cd = jnp.asarray(np.cos(ang) * scale, F32)
    sd = jnp.asarray(np.sin(ang) * scale, F32)
    wf = w_f.astype(F32)
    a = jnp.einsum('de,hef->hdf', cd, wf, precision=HI)
    bm = -jnp.einsum('de,hef->hdf', sd, wf, precision=HI)
    eye = jnp.eye(FNET_HEADS, dtype=F32)
    blk = lambda m: jnp.einsum('hdf,hg->hdgf', m, eye).reshape(D_FNET, D_FNET).astype(BF16)
    return blk(a), blk(bm)


def kernel(x, c, ctx, c_ctx, ada_w, ada_b, norm_mix_g, norm_ffn_g, w_in, w_out, fnet_w, hy_conv_w, hy_conv_b,
           hy_fw1, hy_fb1, hy_freq1, hy_fw2, hy_fb2, hy_freq2, hy_fw3, hy_bias, s5_a_re, s5_a_im, s5_log_dt,
           s5_b_re, s5_b_im, s5_c_re, s5_c_im, s5_d, s5_glu_w, s5_glu_b, ffn_w_gate, ffn_w_up, ffn_w_down,
           moe_router_w, moe_router_b, moe_w_gate, moe_w_up, moe_w_down, final_g):
    b, l, d = x.shape
    lc = ctx.shape[1]
    depth = ada_w.shape[0]
    assert l % LANES == 0 and lc % LANES == 0 and d % LANES == 0

    dft = {}
    for n in sorted({l, lc}):
        fwd = _hyena_dft(n)
        dft[n] = dict(hy_fwd=fwd, hy_inv={k: m.T for k, m in fwd.items()}, fn=_fnet_dft(n))

    cc = jnp.concatenate([c, c_ctx[None, :]], axis=0).astype(F32)
    rows = -(-(b + 1) // 8) * 8
    cc = jnp.pad(cc, ((0, rows - (b + 1)), (0, 0)))

    def mix(li, xs, sh, sc, grid_mode, s0f, s0b, s5m, states_only):
        n_tok = xs.shape[1]
        g_mix = norm_mix_g[li].reshape(1, d)
        if states_only:
            (ps,) = _inproj(xs, g_mix, sh, sc, w_in[li][:, S5_OFF:].astype(BF16),
                            jnp.zeros((3, LANES), F32), jnp.zeros((1, LANES), F32),
                            grid_mode=grid_mode, full=False)
            _, hf, hb = _s5(ps, s0f, s0b, s5m, s5_d[li])
            return None, hf, hb
        pf, qv, qx1, qx2, ps = _inproj(
            xs, g_mix, sh, sc, w_in[li].astype(BF16), hy_conv_w[li].astype(F32),
            hy_conv_b[li].astype(F32).reshape(1, -1), grid_mode=grid_mode, full=True)
        tabs = dft[n_tok]
        amat, bmat = _fnet_maps(fnet_w[li], n_tok)
        yf = _fnet(pf, *tabs['fn'], amat, bmat)
        spectra = _hyena_spectra(n_tok, tabs['hy_fwd'], hy_fw1[li], hy_fb1[li], hy_freq1[li], hy_fw2[li],
                                 hy_fb2[li], hy_freq2[li], hy_fw3[li])
        z = qv
        for o, xg in enumerate((qx1, qx2)):
            pr, pi = _hy_fwd(z, tabs['hy_fwd'], spectra[o])
            z = _hy_inv(pr, pi, tabs['hy_inv'], xg, z, hy_bias[li][o].astype(F32))
        ys, hf, hb = _s5(ps, s0f, s0b, s5m, s5_d[li])
        return (yf, z, ys), hf, hb

    def out_proj(li, parts, xs, gate):
        wo = w_out[li].astype(BF16)
        return _outproj(parts[0], parts[1], parts[2], xs, gate,
                        wo[:D_FNET], wo[D_FNET:D_FNET + D_HYENA], wo[D_FNET + D_HYENA:],
                        s5_glu_w[li].astype(BF16), s5_glu_b[li].astype(F32).reshape(1, -1))

    def channel_mix(li, xs, sh, sc, gate, fin):
        i = li // 2
        g_ffn = norm_ffn_g[li].reshape(1, d)
        if li % 2 == 0:
            return _ffn(xs, g_ffn, sh, sc, gate, ffn_w_gate[i].astype(BF16),
                        ffn_w_up[i].astype(BF16), ffn_w_down[i].astype(BF16), fin, tf=1408)
        return _moe(xs, g_ffn, sh, sc, gate, moe_router_w[i], moe_router_b[i], moe_w_gate[i].astype(BF16),
                    moe_w_up[i].astype(BF16), moe_w_down[i].astype(BF16), fin)

    zero_state = jnp.zeros((b, N_S5_BLOCKS, SLAB, LANES), F32)
    ctx_s = ctx
    for li in range(depth):
        last = li == depth - 1
        mod = _ada(cc, ada_w.astype(F32), li, ada_b[li].astype(F32))
        lat = [m.reshape(b, 1, d) for m in jnp.split(mod[:b], N_MOD, axis=-1)]
        cm = [jnp.broadcast_to(m.reshape(1, 1, d), (b, 1, d)) for m in jnp.split(mod[b:b + 1], N_MOD, axis=-1)]
        s5m = _s5_matrices(s5_a_re[li], s5_a_im[li], s5_log_dt[li], s5_b_re[li], s5_b_im[li],
                           s5_c_re[li], s5_c_im[li])

        parts, hf_c, hb_c = mix(li, ctx_s, cm[0], cm[1], False, zero_state, zero_state, s5m, last)
        if not last:
            ctx_s = out_proj(li, parts, ctx_s, cm[2])
            ctx_s = channel_mix(li, ctx_s, cm[3], cm[4], cm[5], None)

        parts, _, _ = mix(li, x, lat[0], lat[1], True, hf_c, hb_c, s5m, False)
        x = out_proj(li, parts, x, lat[2])
        x = channel_mix(li, x, lat[3], lat[4], lat[5], final_g.reshape(1, d) if last else None)

    if depth == 0:
        raise ValueError("depth must be positive")
    return x
```

```python
import functools
import math

import numpy as np
import jax
import jax.numpy as jnp
from jax import lax
from jax.experimental import pallas as pl
from jax.experimental.pallas import tpu as pltpu

F32 = jnp.float32
BF16 = jnp.bfloat16
HI = lax.Precision.HIGHEST

LANES = 128
GRID_ROW = 64
FNET_HEADS, FNET_HEAD_DIM = 4, 64
D_FNET = FNET_HEADS * FNET_HEAD_DIM
D_HYENA = 384
HYENA_ORDER = 2
HYENA_EMB = 33
HYENA_BANDS = (HYENA_EMB - 1) // 2
HYENA_FAST_DECAY_PCT, HYENA_SLOW_DECAY_PCT, HYENA_DECAY_TARGET = 0.3, 1.5, 0.01
S5_GROUP, S5_GROUPS, S5_STATE = 16, 24, 64
D_S5 = S5_GROUP * S5_GROUPS
HY_OFF = D_FNET
S5_OFF = D_FNET + (HYENA_ORDER + 1) * D_HYENA
N_MOD = 6
TOP_K = 2
RMS_EPS = 1e-6
S5_CHUNK = 16
GROUPS_PER_BLOCK = LANES // S5_GROUP
N_S5_BLOCKS = D_S5 // LANES
STATE_HALF = GROUPS_PER_BLOCK * S5_STATE
NEG_BIG = float(np.finfo(np.float32).min)
VMEM_LIMIT = 56 * 1024 * 1024


def _cp(sem, vmem=None):
    return pltpu.CompilerParams(dimension_semantics=sem, vmem_limit_bytes=vmem)


def _dot(a, b):
    return jnp.dot(a, b, preferred_element_type=F32)


def _split_bf16(a):
    hi = a.astype(BF16)
    lo = (a - hi.astype(F32)).astype(BF16)
    return hi, lo


def _sigmoid(x):
    return 1.0 / (1.0 + jnp.exp(-x))


def _rms_mod(x, g, sh, sc):
    ms = jnp.mean(x * x, axis=-1, keepdims=True)
    return (x * lax.rsqrt(ms + RMS_EPS) * g) * (1.0 + sc) + sh


def _ada_kernel(c_ref, w_ref, b_ref, o_ref):
    c = c_ref[...]
    s = c * _sigmoid(c)
    s_hi, s_lo = _split_bf16(s)
    w_hi, w_lo = _split_bf16(w_ref[...])
    o_ref[...] = _dot(s_hi, w_hi) + _dot(s_hi, w_lo) + _dot(s_lo, w_hi) + b_ref[...]


def _ada(cc, w_all, li, b):
    r, d = cc.shape
    n = w_all.shape[2]
    tn = 512
    return pl.pallas_call(
        _ada_kernel,
        out_shape=jax.ShapeDtypeStruct((r, n), F32),
        grid=(n // tn,),
        in_specs=[pl.BlockSpec((r, d), lambda j: (0, 0)),
                  pl.BlockSpec((None, d, tn), lambda j: (li, 0, j)),
                  pl.BlockSpec((1, tn), lambda j: (0, j))],
        out_specs=pl.BlockSpec((r, tn), lambda j: (0, j)),
        compiler_params=_cp(("parallel",)),
        name="ada_mod",
    )(cc, w_all, b.reshape(1, n))


def _expand_kernel(a1_ref, a2_ref, b1_ref, b2_ref, o_ref, *, nh):
    a2 = a2_ref[...]
    b2 = b2_ref[...]
    for h in range(nh):
        blk = a1_ref[:, h:h + 1] * a2 + b1_ref[:, h:h + 1] * b2
        o_ref[:, h * LANES:(h + 1) * LANES] = blk.astype(o_ref.dtype)


def _expand(a1, a2, b1, b2):
    r, nh = a1.shape
    tr = min(r, 256)
    return pl.pallas_call(
        functools.partial(_expand_kernel, nh=nh),
        out_shape=jax.ShapeDtypeStruct((r, nh * LANES), BF16),
        grid=(r // tr,),
        in_specs=[pl.BlockSpec((tr, nh), lambda i: (i, 0)),
                  pl.BlockSpec((tr, LANES), lambda i: (i, 0)),
                  pl.BlockSpec((tr, nh), lambda i: (i, 0)),
                  pl.BlockSpec((tr, LANES), lambda i: (i, 0))],
        out_specs=pl.BlockSpec((tr, nh * LANES), lambda i: (i, 0)),
        compiler_params=_cp(("parallel",)),
        name="dft_expand",
    )(a1, a2, b1, b2)


def _fnet_dft(l):
    h = l // 2
    nh = h // LANES
    period = 2 * l
    k = np.arange(l, dtype=np.int64)[:, None]
    qa = (k * (2 * LANES) * np.arange(nh, dtype=np.int64)[None, :]) % period
    qb = (k * (2 * np.arange(LANES, dtype=np.int64)[None, :] + 1)) % period
    aa = 2.0 * np.pi * qa / period
    ab = 2.0 * np.pi * qb / period
    f = lambda t: jnp.asarray(t, F32)
    c1, s1, c2, s2 = np.cos(aa), np.sin(aa), np.cos(ab), np.sin(ab)
    phase = np.zeros((l, LANES), np.float32)
    phase[:, 0] = np.cos(np.pi * k[:, 0] / l)
    phase[:, 1] = np.sin(np.pi * k[:, 0] / l)
    return _expand(f(c1), f(c2), f(-s1), f(s2)), _expand(f(-s1), f(c2), f(-c1), f(s2)), jnp.asarray(phase)


def _hyena_dft(l):
    h = l // 2
    nh = h // LANES
    period = 4 * l
    kp = np.arange(h, dtype=np.int64)[:, None]
    f = lambda t: jnp.asarray(t, F32)
    out = {}
    for name, k in (("e", 2 * kp), ("o", 2 * kp + 1)):
        qa = (k * (2 * LANES) * np.arange(nh, dtype=np.int64)[None, :]) % period
        qb = (k * (2 * np.arange(LANES, dtype=np.int64)[None, :] + 1)) % period
        aa = 2.0 * np.pi * qa / period
        ab = 2.0 * np.pi * qb / period
        c1, s1, c2, s2 = np.cos(aa), np.sin(aa), np.cos(ab), np.sin(ab)
        a1, a2, b1 = -s1, c2.copy(), -c1
        if name == "e":
            a1[0, :] = 1.0
            a2[0, :] = (-1.0) ** np.arange(LANES)
            b1[0, :] = 0.0
        out["c" + name] = _expand(f(c1), f(c2), f(-s1), f(s2))
        out["s" + name] = _expand(f(a1), f(a2), f(b1), f(s2))
    return out


def _flip_matrix(n):
    return jnp.asarray(np.eye(n, dtype=np.float32)[::-1], BF16)


def _flip_rows(v, j):
    sb = j.shape[0]
    nblk = v.shape[0] // sb
    v_hi, v_lo = _split_bf16(v)
    parts = []
    for s in range(nblk):
        r = slice((nblk - 1 - s) * sb, (nblk - s) * sb)
        parts.append(_dot(j, v_hi[r]) + _dot(j, v_lo[r]))
    return parts[0] if nblk == 1 else jnp.concatenate(parts, axis=0)


def _inproj_kernel(x_ref, g_ref, sh_ref, sc_ref, w_ref, cw_ref, cb_ref, j_ref, *out_refs, tm, nt, grid_mode, full):
    h = _rms_mod(x_ref[...], g_ref[...], sh_ref[...], sc_ref[...])
    acc = _dot(h.astype(BF16), w_ref[...])
    if not full:
        out_refs[0][...] = acc
        return
    pf_ref, qv_ref, qx1_ref, qx2_ref, ps_ref = out_refs
    ps_ref[...] = acc[:, S5_OFF:]
    hy = acc[:, HY_OFF:S5_OFF]
    row = lax.broadcasted_iota(jnp.int32, (tm, 1), 0)
    if grid_mode:
        first = (row % GRID_ROW) == 0
        last = (row % GRID_ROW) == GRID_ROW - 1
    else:
        first = row == 0
        last = row == tm - 1
    prev = jnp.where(first, 0.0, pltpu.roll(hy, 1, 0))
    nxt = jnp.where(last, 0.0, pltpu.roll(hy, tm - 1, 0))
    q = prev * cw_ref[0:1, :] + hy * cw_ref[1:2, :] + nxt * cw_ref[2:3, :] + cb_ref[...]
    q = jnp.concatenate([acc[:, :HY_OFF], q], axis=1)
    folded = (pf_ref, qv_ref, qx1_ref, qx2_ref)
    edges = (0, D_FNET, D_FNET + D_HYENA, D_FNET + 2 * D_HYENA, D_FNET + 3 * D_HYENA)
    cols = [slice(edges[n], edges[n + 1]) for n in range(len(folded))]
    if nt == 1:
        half = tm // 2
        q_hi = _flip_rows(q[half:], j_ref[...])
        for r, c in zip(folded, cols):
            r[0] = q[:half, c]
            r[1] = q_hi[:, c]
        return
    i = pl.program_id(1)

    @pl.when(i < nt // 2)
    def _():
        for r, c in zip(folded, cols):
            r[...] = q[:, c]

    @pl.when(i >= nt // 2)
    def _():
        qf = _flip_rows(q, j_ref[...])
        for r, c in zip(folded, cols):
            r[...] = qf[:, c]


def _fold_spec(tm, nt, width):
    if nt == 1:
        return pl.BlockSpec((None, 2, tm // 2, width), lambda bi, i: (bi, 0, 0, 0))
    hn = nt // 2
    return pl.BlockSpec((None, None, tm, width),
                        lambda bi, i: (bi, i // hn, jnp.where(i < hn, i, nt - 1 - i), 0))


def _inproj(x, g, sh, sc, w, cw, cb, *, grid_mode, full):
    b, l, d = x.shape
    n = w.shape[1]
    tm = min(l, 512)
    nt = l // tm
    if not grid_mode:
        assert tm == l
    else:
        assert tm % GRID_ROW == 0
    assert nt == 1 or nt % 2 == 0
    row = lambda bi, i: (bi, i, 0)
    vec = lambda bi, i: (bi, 0, 0)
    const = lambda bi, i: (0, 0)
    flip = _flip_matrix(min(256, tm // 2 if nt == 1 else tm))
    plain = lambda wd: (jax.ShapeDtypeStruct((b, l, wd), F32), pl.BlockSpec((None, tm, wd), row))
    fold = lambda wd: (jax.ShapeDtypeStruct((b, 2, l // 2, wd), F32), _fold_spec(tm, nt, wd))
    outs = ([fold(D_FNET), fold(D_HYENA), fold(D_HYENA), fold(D_HYENA), plain(D_S5)] if full else [plain(n)])
    return pl.pallas_call(
        functools.partial(_inproj_kernel, tm=tm, nt=nt, grid_mode=grid_mode, full=full),
        out_shape=tuple(o[0] for o in outs),
        grid=(b, nt),
        in_specs=[pl.BlockSpec((None, tm, d), row),
                  pl.BlockSpec((1, d), const),
                  pl.BlockSpec((None, 1, d), vec),
                  pl.BlockSpec((None, 1, d), vec),
                  pl.BlockSpec((d, n), const),
                  pl.BlockSpec(cw.shape, const),
                  pl.BlockSpec(cb.shape, const),
                  pl.BlockSpec(flip.shape, const)],
        out_specs=tuple(o[1] for o in outs),
        compiler_params=_cp(("parallel", "parallel"), VMEM_LIMIT),
        name="inproj",
    )(x, g, sh, sc, w, cw, cb, flip)


def _fnet_kernel(x_ref, mc_ref, ms_ref, ph_ref, a_ref, b_ref, o_ref, accr, acci, *, nb):
    k = pl.program_id(2)

    @pl.when(k == 0)
    def _():
        accr[...] = jnp.zeros_like(accr)
        acci[...] = jnp.zeros_like(acci)

    mc = mc_ref[...]
    ms = ms_ref[...]
    for bi in range(nb):
        lo = x_ref[bi, 0]
        hi = x_ref[bi, 1]
        accr[bi] += _dot(mc, (lo + hi).astype(BF16))
        acci[bi] += _dot(ms, (lo - hi).astype(BF16))

    @pl.when(k == pl.num_programs(2) - 1)
    def _():
        ck = ph_ref[:, 0:1]
        sk = ph_ref[:, 1:2]
        for bi in range(nb):
            zr, zi = accr[bi], acci[bi]
            cos_part = ck * zr - sk * zi
            sin_part = -(sk * zr + ck * zi)
            o_ref[bi] = (_dot(cos_part.astype(BF16), a_ref[...])
                         + _dot(sin_part.astype(BF16), b_ref[...]))


def _fnet(pf, mc, ms, phase, amat, bmat):
    b, _, h, c = pf.shape
    l = 2 * h
    nb = min(b, 2)
    t = min(l, 1024)
    tk = min(h, 1024)
    return pl.pallas_call(
        functools.partial(_fnet_kernel, nb=nb),
        out_shape=jax.ShapeDtypeStruct((b, l, c), F32),
        grid=(b // nb, l // t, h // tk),
        in_specs=[pl.BlockSpec((nb, 2, tk, c), lambda bb, i, k: (bb, 0, k, 0)),
                  pl.BlockSpec((t, tk), lambda bb, i, k: (i, k)),
                  pl.BlockSpec((t, tk), lambda bb, i, k: (i, k)),
                  pl.BlockSpec((t, LANES), lambda bb, i, k: (i, 0)),
                  pl.BlockSpec((c, c), lambda bb, i, k: (0, 0)),
                  pl.BlockSpec((c, c), lambda bb, i, k: (0, 0))],
        out_specs=pl.BlockSpec((nb, t, c), lambda bb, i, k: (bb, i, 0)),
        scratch_shapes=[pltpu.VMEM((nb, t, c), F32), pltpu.VMEM((nb, t, c), F32)],
        compiler_params=_cp(("parallel", "parallel", "arbitrary"), VMEM_LIMIT),
        name="fnet_dft",
    )(pf, mc, ms, phase, amat, bmat)


def _hy_fwd_kernel(z_ref, ce_ref, se_ref, co_ref, so_ref, *rest, nb, filtered):
    if filtered:
        kf_ref, pr_ref, pi_ref, acc = rest
    else:
        pr_ref, pi_ref, acc = rest
    k = pl.program_id(2)

    @pl.when(k == 0)
    def _():
        acc[...] = jnp.zeros_like(acc)

    c = z_ref.shape[-1]
    ev = jnp.concatenate([(z_ref[bi, 0] + z_ref[bi, 1]).astype(BF16) for bi in range(nb)], axis=1)
    od = jnp.concatenate([(z_ref[bi, 0] - z_ref[bi, 1]).astype(BF16) for bi in range(nb)], axis=1)
    acc[0] += _dot(ce_ref[...], ev)
    acc[1] += _dot(se_ref[...], od)
    acc[2] += _dot(co_ref[...], od)
    acc[3] += _dot(so_ref[...], ev)

    @pl.when(k == pl.num_programs(2) - 1)
    def _():
        for bi in range(nb):
            cols = slice(bi * c, (bi + 1) * c)
            for p in range(2):
                zr, zi = acc[2 * p, :, cols], acc[2 * p + 1, :, cols]
                if filtered:
                    kr, ki, kd = kf_ref[p, 0], kf_ref[p, 1], kf_ref[p, 2]
                    zr, zi = zr * kr - zi * ki, zr * ki + zi * kd
                pr_ref[bi, p] = zr.astype(pr_ref.dtype)
                pi_ref[bi, p] = zi.astype(pi_ref.dtype)


def _hy_tiles(h):
    return min(h, 1024), min(h, 512)


def _once(block_shape, index_map):
    return pl.BlockSpec(block_shape, index_map, pipeline_mode=pl.Buffered(1))


def _hy_fwd(z, mats, kf, out_dtype=BF16):
    b, _, h, c = z.shape
    nb = min(b, 2)
    ti, tk = _hy_tiles(h)
    mspec = pl.BlockSpec((ti, tk), lambda i, bb, k: (i, k))
    ospec = pl.BlockSpec((nb, 2, ti, c), lambda i, bb, k: (bb, 0, i, 0))
    filtered = kf is not None
    extra_specs = [_once((2, 3, ti, c), lambda i, bb, k: (0, 0, i, 0))] if filtered else []
    extra_args = (kf,) if filtered else ()
    return pl.pallas_call(
        functools.partial(_hy_fwd_kernel, nb=nb, filtered=filtered),
        out_shape=(jax.ShapeDtypeStruct((b, 2, h, c), out_dtype), jax.ShapeDtypeStruct((b, 2, h, c), out_dtype)),
        grid=(h // ti, b // nb, h // tk),
        in_specs=[pl.BlockSpec((nb, 2, tk, c), lambda i, bb, k: (bb, 0, k, 0)),
                  mspec, mspec, mspec, mspec] + extra_specs,
        out_specs=(ospec, ospec),
        scratch_shapes=[pltpu.VMEM((4, ti, nb * c), F32)],
        compiler_params=_cp(("parallel", "parallel", "arbitrary"), VMEM_LIMIT),
        name="hyena_fwd_dft",
    )(z, mats["ce"], mats["se"], mats["co"], mats["so"], *extra_args)


def _hy_inv_kernel(pr_ref, pi_ref, ce_ref, se_ref, co_ref, so_ref, xg_ref, zin_ref, bias_ref, o_ref, acc, *, nb):
    k = pl.program_id(2)

    @pl.when(k == 0)
    def _():
        acc[...] = jnp.zeros_like(acc)

    c = xg_ref.shape[-1]
    side = lambda ref, p: jnp.concatenate([ref[bi, p] for bi in range(nb)], axis=1)
    acc[0] += _dot(ce_ref[...], side(pr_ref, 0)) + _dot(so_ref[...], side(pi_ref, 1))
    acc[1] += _dot(se_ref[...], side(pi_ref, 0)) + _dot(co_ref[...], side(pr_ref, 1))

    @pl.when(k == pl.num_programs(2) - 1)
    def _():
        bias = bias_ref[...]
        for bi in range(nb):
            cols = slice(bi * c, (bi + 1) * c)
            u, v = acc[0, :, cols], acc[1, :, cols]
            o_ref[bi, 0] = xg_ref[bi, 0] * (u + v + zin_ref[bi, 0] * bias)
            o_ref[bi, 1] = xg_ref[bi, 1] * (u - v + zin_ref[bi, 1] * bias)


def _hy_inv(pr, pi, mats_t, xg, zin, bias):
    b, _, h, c = xg.shape
    nb = min(b, 2)
    ti, tk = min(h, 512), min(h, 1024)
    pspec = pl.BlockSpec((nb, 2, tk, c), lambda bb, i, k: (bb, 0, k, 0))
    mspec = pl.BlockSpec((ti, tk), lambda bb, i, k: (i, k))
    ospec = pl.BlockSpec((nb, 2, ti, c), lambda bb, i, k: (bb, 0, i, 0))
    return pl.pallas_call(
        functools.partial(_hy_inv_kernel, nb=nb),
        out_shape=jax.ShapeDtypeStruct((b, 2, h, c), F32),
        grid=(b // nb, h // ti, h // tk),
        in_specs=[pspec, pspec, mspec, mspec, mspec, mspec, ospec, ospec,
                  pl.BlockSpec((1, c), lambda bb, i, k: (0, 0))],
        out_specs=ospec,
        scratch_shapes=[pltpu.VMEM((2, ti, nb * c), F32)],
        compiler_params=_cp(("parallel", "parallel", "arbitrary"), VMEM_LIMIT),
        name="hyena_inv_dft",
    )(pr, pi, mats_t["ce"], mats_t["se"], mats_t["co"], mats_t["so"], xg, zin, bias.reshape(1, c))


def _s5_kernel(u_ref, s0f_ref, s0b_ref, mi_ref, mof_ref, mob_ref, mnf_ref, mnb_ref, atf_ref, atb_ref,
               d_ref, y_ref, hf_ref, hb_ref, x_scr, sf_scr, sb_scr, hinf_scr, hinb_scr, *, nc, t_chunk, nbb):
    for bi in range(nbb):
        for s in range(t_chunk):
            x_scr[bi * nc:(bi + 1) * nc, s * LANES:(s + 1) * LANES] = (
                u_ref[bi, pl.ds(s, nc, stride=t_chunk), :].astype(BF16))
    x = x_scr[...]
    rows_all = nbb * nc
    _to_slabs(sf_scr, _dot(x, mof_ref[...]), rows_all)
    _to_slabs(sb_scr, _dot(x, mob_ref[...]), rows_all)
    cf1, cf2 = atf_ref[0], atf_ref[1]
    cb1, cb2 = atb_ref[0], atb_ref[1]

    def body(c, carry):
        out = []
        for bi in range(nbb):
            hf, hb = carry[2 * bi], carry[2 * bi + 1]
            rf = pl.ds(pl.multiple_of((bi * nc + c) * SLAB, SLAB), SLAB)
            rb = pl.ds(pl.multiple_of((bi * nc + nc - 1 - c) * SLAB, SLAB), SLAB)
            hinf_scr[rf, :] = hf
            hinb_scr[rb, :] = hb
            out.append(cf1 * hf + cf2 * pltpu.roll(hf, SLAB // 2, 0) + sf_scr[rf, :])
            out.append(cb1 * hb + cb2 * pltpu.roll(hb, SLAB // 2, 0) + sb_scr[rb, :])
        return tuple(out)

    init = []
    for bi in range(nbb):
        init += [s0f_ref[bi], s0b_ref[bi]]
    fin = lax.fori_loop(0, nc, body, tuple(init), unroll=4 if nbb == 1 else 1)
    for bi in range(nbb):
        hf_ref[bi] = fin[2 * bi]
        hb_ref[bi] = fin[2 * bi + 1]
    y = (_dot(x, mi_ref[...])
         + _dot(_from_slabs(hinf_scr, rows_all).astype(BF16), mnf_ref[...])
         + _dot(_from_slabs(hinb_scr, rows_all).astype(BF16), mnb_ref[...]))
    d = d_ref[...]
    for bi in range(nbb):
        for t in range(t_chunk):
            rows = pl.ds(t, nc, stride=t_chunk)
            y_ref[bi, rows, :] = (y[bi * nc:(bi + 1) * nc, t * LANES:(t + 1) * LANES] + d * u_ref[bi, rows, :])


def _s5(u, s0f, s0b, mats, d_skip):
    b, l, _ = u.shape
    tc = S5_CHUNK
    nc = l // tc
    kin = tc * LANES
    ns = 2 * STATE_HALF
    mi, mof, mob, mnf, mnb, atf, atb = mats
    nbb = b if l <= 1024 else 1
    st_spec = pl.BlockSpec((nbb, None, SLAB, LANES), lambda j, bi: (bi, j, 0, 0))
    w3 = lambda s1, s2: pl.BlockSpec((None, s1, s2), lambda j, bi: (j, 0, 0), pipeline_mode=pl.Buffered(1))
    coef = pl.BlockSpec((None, 2, SLAB, LANES), lambda j, bi: (j, 0, 0, 0))
    u_spec = pl.BlockSpec((nbb, l, LANES), lambda j, bi: (bi, 0, j))
    state = jax.ShapeDtypeStruct((b, N_S5_BLOCKS, SLAB, LANES), F32)
    return pl.pallas_call(
        functools.partial(_s5_kernel, nc=nc, t_chunk=tc, nbb=nbb),
        out_shape=(jax.ShapeDtypeStruct((b, l, D_S5), F32), state, state),
        grid=(N_S5_BLOCKS, b // nbb),
        in_specs=[u_spec, st_spec, st_spec,
                  w3(kin, kin), w3(kin, ns), w3(kin, ns), w3(ns, kin), w3(ns, kin),
                  coef, coef,
                  pl.BlockSpec((1, LANES), lambda j, bi: (0, j))],
        out_specs=(u_spec, st_spec, st_spec),
        scratch_shapes=([pltpu.VMEM((nbb * nc, kin), BF16)]
                        + [pltpu.VMEM((nbb * nc * SLAB, LANES), F32)] * 4),
        compiler_params=_cp(("parallel", "parallel"), VMEM_LIMIT),
        name="s5_scan",
    )(u, s0f, s0b, mi, mof, mob, mnf, mnb, atf, atb, d_skip.reshape(1, D_S5))


def _s5_matrices(a_re, a_im, log_dt, b_re, b_im, c_re, c_im):
    tc = S5_CHUNK
    g, p, hd = S5_GROUPS, S5_STATE, S5_GROUP
    lam = lax.complex(a_re.astype(F32), a_im.astype(F32))
    dt = jnp.exp(log_dt.astype(F32))[..., None]
    a_bar = jnp.exp(lam * dt)
    b_bar = ((a_bar - 1.0) / lam)[..., None] * lax.complex(b_re.astype(F32), b_im.astype(F32))
    cc = lax.complex(c_re.astype(F32), c_im.astype(F32))
    ks = jnp.arange(tc + 1, dtype=F32)
    pw = jnp.exp((lam * dt)[:, :, None, :] * ks[None, None, :, None])
    kern = jnp.real(jnp.einsum('dgip,dgkp,dgpj->dgkij', cc, pw[:, :, :tc], b_bar, precision=HI))
    lag = np.arange(tc)[None, :] - np.arange(tc)[:, None]
    sel = np.stack([lag[None] == np.arange(tc)[:, None, None], -lag[None] == np.arange(tc)[:, None, None]])
    sel = jnp.asarray(sel.astype(np.float32))
    kern_j = kern.reshape(2, N_S5_BLOCKS, GROUPS_PER_BLOCK, tc, hd, hd)
    kt = jnp.einsum('dJgkij,dkst->Jsgjti', kern_j, sel, precision=HI)
    def place(slab, inner, row_group_width):
        width = slab.shape[-1]
        src = np.arange(width)
        e = np.zeros((width, GROUPS_PER_BLOCK * width), np.float32)
        for hh in range(GROUPS_PER_BLOCK):
            e[src, (src // inner) * GROUPS_PER_BLOCK * inner + hh * inner + src % inner] = 1.0
        wide = jnp.einsum('Jrk,kc->Jrc', slab, jnp.asarray(e), precision=HI)
        rows = np.arange(wide.shape[1])[:, None]
        cols = np.arange(wide.shape[2])[None, :]
        keep = (rows // row_group_width) % GROUPS_PER_BLOCK == (cols // inner) % GROUPS_PER_BLOCK
        return jnp.where(jnp.asarray(keep)[None], wide, 0.0)

    mi = place(kt.reshape(N_S5_BLOCKS, tc * LANES, tc * hd), hd, hd)

    def parts(z):
        return jnp.stack([jnp.real(z), jnp.imag(z)], axis=0)

    def out_mat(d, powers):
        z = powers[:, :, :, None] * b_bar[d][:, None, :, :]
        z = parts(z).reshape(2, N_S5_BLOCKS, GROUPS_PER_BLOCK, tc, p, hd)
        z = jnp.transpose(z, (1, 3, 2, 5, 0, 4))
        return place(z.reshape(N_S5_BLOCKS, tc * LANES, 2 * p), p, hd)

    def in_mat(d, powers):
        z = cc[d][:, None, :, :] * powers[:, :, None, :]
        z = jnp.stack([jnp.real(z), -jnp.imag(z)], axis=0)
        z = z.reshape(2, N_S5_BLOCKS, GROUPS_PER_BLOCK, tc, hd, p)
        z = jnp.transpose(z, (1, 0, 2, 5, 3, 4))
        return place(z.reshape(N_S5_BLOCKS, 2 * STATE_HALF, tc * hd), hd, p)

    rev = jnp.arange(tc - 1, -1, -1)
    mof = out_mat(0, pw[0][:, rev])
    mob = out_mat(1, pw[1][:, :tc])
    mnf = in_mat(0, pw[0][:, 1:tc + 1])
    mnb = in_mat(1, pw[1][:, tc - jnp.arange(tc)])

    def step(d):
        ar = jnp.real(pw[d][:, tc]).reshape(N_S5_BLOCKS, STATE_HALF)
        ai = jnp.imag(pw[d][:, tc]).reshape(N_S5_BLOCKS, STATE_HALF)
        c1 = jnp.concatenate([ar, ar], axis=1)
        c2 = jnp.concatenate([-ai, ai], axis=1)
        return jnp.stack([c1, c2], axis=1).reshape(N_S5_BLOCKS, 2, SLAB, LANES)

    bf = lambda m: m.astype(BF16)
    return bf(mi), bf(mof), bf(mob), bf(mnf), bf(mnb), step(0), step(1)


def _outproj_kernel(yf_ref, yh_ref, ys_ref, x_ref, gate_ref, w1_ref, w2_ref, w3_ref, gw_ref, gb_ref, j_ref, o_ref,
                    *, nt):
    y = ys_ref[...]
    y = 0.5 * y * (1.0 + jnp.tanh(math.sqrt(2.0 / math.pi) * (y + 0.044715 * (y * y * y))))
    y = y * _sigmoid(_dot(y.astype(BF16), gw_ref[...]) + gb_ref[...])
    acc = _dot(yf_ref[...].astype(BF16), w1_ref[...]) + _dot(y.astype(BF16), w3_ref[...])

    def finish(yh):
        o_ref[...] = x_ref[...] + gate_ref[...] * (acc + _dot(yh.astype(BF16), w2_ref[...]))

    if nt == 1:
        finish(jnp.concatenate([yh_ref[0], _flip_rows(yh_ref[1], j_ref[...])], axis=0))
        return
    i = pl.program_id(1)

    @pl.when(i < nt // 2)
    def _():
        finish(yh_ref[...])

    @pl.when(i >= nt // 2)
    def _():
        finish(_flip_rows(yh_ref[...], j_ref[...]))


def _outproj(yf, yh, ys, x, gate, w1, w2, w3, gw, gb):
    b, l, d = x.shape
    tm = min(l, 512)
    nt = l // tm
    row = lambda bi, i: (bi, i, 0)
    const = lambda bi, i: (0, 0)
    full = lambda a: pl.BlockSpec(a.shape, const)
    flip = _flip_matrix(min(256, tm // 2 if nt == 1 else tm))
    return pl.pallas_call(
        functools.partial(_outproj_kernel, nt=nt),
        out_shape=jax.ShapeDtypeStruct((b, l, d), F32),
        grid=(b, nt),
        in_specs=[pl.BlockSpec((None, tm, yf.shape[2]), row),
                  _fold_spec(tm, nt, yh.shape[3]),
                  pl.BlockSpec((None, tm, ys.shape[2]), row),
                  pl.BlockSpec((None, tm, d), row),
                  pl.BlockSpec((None, 1, d), lambda bi, i: (bi, 0, 0)),
                  full(w1), full(w2), full(w3), full(gw), full(gb), full(flip)],
        out_specs=pl.BlockSpec((None, tm, d), row),
        compiler_params=_cp(("parallel", "parallel"), VMEM_LIMIT),
        name="outproj",
    )(yf, yh, ys, x, gate, w1, w2, w3, gw, gb, flip)


def _ffn_kernel(x_ref, g_ref, sh_ref, sc_ref, gate_ref, wg_ref, wu_ref, wd_ref, fg_ref, o_ref, xn_scr, acc_scr,
                *, final_norm):
    f = pl.program_id(2)

    @pl.when(f == 0)
    def _():
        xn_scr[...] = _rms_mod(x_ref[...], g_ref[...], sh_ref[...], sc_ref[...]).astype(BF16)
        acc_scr[...] = jnp.zeros_like(acc_scr)

    xn = xn_scr[...]
    gt = _dot(xn, wg_ref[...])
    up = _dot(xn, wu_ref[...])
    acc_scr[...] += _dot((gt * _sigmoid(gt) * up).astype(BF16), wd_ref[...])

    @pl.when(f == pl.num_programs(2) - 1)
    def _():
        y = x_ref[...] + gate_ref[...] * acc_scr[...]
        if final_norm:
            ms = jnp.mean(y * y, axis=-1, keepdims=True)
            y = y * lax.rsqrt(ms + RMS_EPS) * fg_ref[...]
        o_ref[...] = y


def _ffn(x, g, sh, sc, gate, wg, wu, wd, final_g, *, tf):
    b, l, d = x.shape
    ff = wg.shape[1]
    final_norm = final_g is not None
    tm = min(l, 1024)
    fg = final_g if final_norm else jnp.ones((1, d), F32)
    row = lambda bi, i, f: (bi, i, 0)
    vec = lambda bi, i, f: (bi, 0, 0)
    const = lambda bi, i, f: (0, 0)
    return pl.pallas_call(
        functools.partial(_ffn_kernel, final_norm=final_norm),
        out_shape=jax.ShapeDtypeStruct((b, l, d), F32),
        grid=(b, l // tm, ff // tf),
        in_specs=[pl.BlockSpec((None, tm, d), row),
                  pl.BlockSpec((1, d), const),
                  pl.BlockSpec((None, 1, d), vec),
                  pl.BlockSpec((None, 1, d), vec),
                  pl.BlockSpec((None, 1, d), vec),
                  pl.BlockSpec((d, tf), lambda bi, i, f: (0, f)),
                  pl.BlockSpec((d, tf), lambda bi, i, f: (0, f)),
                  pl.BlockSpec((tf, d), lambda bi, i, f: (f, 0)),
                  pl.BlockSpec((1, d), const)],
        out_specs=pl.BlockSpec((None, tm, d), row),
        scratch_shapes=[pltpu.VMEM((tm, d), BF16), pltpu.VMEM((tm, d), F32)],
        compiler_params=_cp(("parallel", "parallel", "arbitrary"), VMEM_LIMIT),
        name="dense_ffn",
    )(x, g, sh, sc, gate, wg, wu, wd, fg)


SLAB = 8
ROUTE_TM = 512
GROUP_TM = 1024


def _to_slabs(ref, val, n):
    for s in range(SLAB):
        ref[pl.ds(s, n, stride=SLAB), :] = val[:, s * LANES:(s + 1) * LANES]


def _from_slabs(ref, n):
    return jnp.concatenate([ref[pl.ds(s, n, stride=SLAB), :] for s in range(SLAB)], axis=1)


def _route_kernel(x_ref, g_ref, sh_ref, sc_ref, rwh_ref, rwl_ref, rb_ref, tri_ref,
                  xs_ref, meta_ref, gate_ref, cnt_ref, carry_scr, *, tm):
    step = pl.program_id(0) * pl.num_programs(1) + pl.program_id(1)

    @pl.when(step == 0)
    def _():
        carry_scr[...] = jnp.zeros_like(carry_scr)

    h = _rms_mod(x_ref[...], g_ref[...], sh_ref[...], sc_ref[...])
    _to_slabs(xs_ref, h, tm)
    h_hi, h_lo = _split_bf16(h)
    rwh = rwh_ref[...]
    lg = _dot(h_hi, rwh) + _dot(h_hi, rwl_ref[...]) + _dot(h_lo, rwh) + rb_ref[...]
    lane = lax.broadcasted_iota(jnp.int32, lg.shape, 1)
    m1 = jnp.max(lg, axis=-1, keepdims=True)
    i1 = jnp.min(jnp.where(lg == m1, lane, LANES), axis=-1, keepdims=True)
    lg2 = jnp.where(lane == i1, NEG_BIG, lg)
    m2 = jnp.max(lg2, axis=-1, keepdims=True)
    i2 = jnp.min(jnp.where(lg2 == m2, lane, LANES), axis=-1, keepdims=True)
    ex = jnp.exp(m2 - m1)
    den = 1.0 + ex
    gate_ref[...] = jnp.where(lane == 0, 1.0 / den, jnp.where(lane == 1, ex / den, 0.0))
    sel1 = lane == i1
    sel2 = lane == i2
    tri = tri_ref[...]
    p1 = _dot(tri, sel1.astype(BF16))
    p2 = _dot(tri, sel2.astype(BF16))
    tot1 = p1[tm - 1:tm, :]
    tot2 = p2[tm - 1:tm, :]
    carry = carry_scr[...]
    r1 = jnp.sum(jnp.where(sel1, carry + p1 - 1.0, 0.0), axis=-1, keepdims=True)
    r2 = jnp.sum(jnp.where(sel2, carry + tot1 + p2 - 1.0, 0.0), axis=-1, keepdims=True)
    carry = carry + tot1 + tot2
    carry_scr[...] = carry
    cnt_ref[...] = carry.astype(jnp.int32)
    meta_ref[...] = jnp.where(lane == 0, i1, jnp.where(lane == 1, i2, jnp.where(
        lane == 2, r1.astype(jnp.int32), jnp.where(lane == 3, r2.astype(jnp.int32), 0))))


def _route(x, g, sh, sc, rwh, rwl, rb):
    b, l, d = x.shape
    tm = min(l, ROUTE_TM)
    n = b * l
    nl = l // tm
    tri = jnp.tril(jnp.ones((tm, tm), F32)).astype(BF16)
    row = lambda bi, i: (bi, i, 0)
    vec = lambda bi, i: (bi, 0, 0)
    const = lambda bi, i: (0, 0)
    flat = lambda bi, i: (bi * nl + i, 0)
    return pl.pallas_call(
        functools.partial(_route_kernel, tm=tm),
        out_shape=(jax.ShapeDtypeStruct((n * SLAB, LANES), F32),
                   jax.ShapeDtypeStruct((n, LANES), jnp.int32),
                   jax.ShapeDtypeStruct((n, LANES), F32),
                   jax.ShapeDtypeStruct((1, LANES), jnp.int32)),
        grid=(b, nl),
        in_specs=[pl.BlockSpec((None, tm, d), row),
                  pl.BlockSpec((1, d), const),
                  pl.BlockSpec((None, 1, d), vec),
                  pl.BlockSpec((None, 1, d), vec),
                  pl.BlockSpec((d, LANES), const),
                  pl.BlockSpec((d, LANES), const),
                  pl.BlockSpec((1, LANES), const),
                  pl.BlockSpec((tm, tm), const)],
        out_specs=(pl.BlockSpec((tm * SLAB, LANES), flat),
                   pl.BlockSpec((tm, LANES), flat),
                   pl.BlockSpec((tm, LANES), flat),
                   pl.BlockSpec((1, LANES), const)),
        scratch_shapes=[pltpu.VMEM((1, LANES), F32)],
        compiler_params=_cp(("arbitrary", "arbitrary"), VMEM_LIMIT),
        name="moe_route",
    )(x, g, sh, sc, rwh, rwl, rb, tri)


def _dispatch_kernel(pos_hbm, xs_ref, init_hbm, out_hbm, idx_smem, isem, sem, *, tm):
    del init_hbm
    step = pl.program_id(0)
    icopy = pltpu.make_async_copy(pos_hbm.at[pl.ds(step * 2 * tm, 2 * tm)], idx_smem, isem)
    icopy.start()
    icopy.wait()

    def row_copy(i, k):
        return pltpu.make_async_copy(xs_ref.at[pl.ds(i * SLAB, SLAB), :], out_hbm.at[idx_smem[2 * i + k]], sem)

    def issue(i, carry):
        row_copy(i, 0).start()
        row_copy(i, 1).start(priority=1)
        return carry

    def drain(i, carry):
        row_copy(i, 0).wait()
        row_copy(i, 1).wait()
        return carry

    lax.fori_loop(0, tm, issue, 0, unroll=8)
    lax.fori_loop(0, tm, drain, 0, unroll=8)


def _dispatch(xs, pos, n_rows):
    n = xs.shape[0] // SLAB
    tm = min(n, ROUTE_TM)
    init = jnp.zeros((n_rows, SLAB, LANES), F32)
    return pl.pallas_call(
        functools.partial(_dispatch_kernel, tm=tm),
        out_shape=jax.ShapeDtypeStruct((n_rows, SLAB, LANES), F32),
        grid=(n // tm,),
        in_specs=[pl.BlockSpec(memory_space=pl.ANY),
                  pl.BlockSpec((tm * SLAB, LANES), lambda i: (i, 0)),
                  pl.BlockSpec(memory_space=pl.ANY)],
        out_specs=pl.BlockSpec(memory_space=pl.ANY),
        scratch_shapes=[pltpu.SMEM((2 * tm,), jnp.int32), pltpu.SemaphoreType.DMA, pltpu.SemaphoreType.DMA],
        input_output_aliases={2: 0},
        compiler_params=_cp(("arbitrary",), VMEM_LIMIT),
        name="moe_dispatch",
    )(pos.reshape(-1), xs, init)


def _group_ffn_kernel(te_ref, x_ref, wg_ref, wu_ref, wd_ref, o_ref, xn_scr, acc_scr, *, tm, nt):
    t = pl.program_id(0)
    f = pl.program_id(1)
    valid = t < te_ref[nt]
    last = f == pl.num_programs(1) - 1

    @pl.when(jnp.logical_and(valid, f == 0))
    def _():
        xn_scr[...] = _from_slabs(x_ref, tm).astype(BF16)
        acc_scr[...] = jnp.zeros_like(acc_scr)

    @pl.when(valid)
    def _():
        xn = xn_scr[...]
        gt = _dot(xn, wg_ref[...])
        up = _dot(xn, wu_ref[...])
        acc_scr[...] += _dot((gt * _sigmoid(gt) * up).astype(BF16), wd_ref[...])

    @pl.when(jnp.logical_and(valid, last))
    def _():
        _to_slabs(o_ref, acc_scr[...], tm)

    @pl.when(jnp.logical_and(jnp.logical_not(valid), last))
    def _():
        o_ref[...] = jnp.zeros_like(o_ref)


def _group_ffn(xs, tile_expert, wg, wu, wd, *, tf):
    n_rows = xs.shape[0]
    tm = GROUP_TM
    nt = n_rows // tm
    _, d, ff = wg.shape
    x2 = xs.reshape(n_rows * SLAB, LANES)
    rows = pl.BlockSpec((tm * SLAB, LANES), lambda t, f, te: (t, 0))
    out = pl.pallas_call(
        functools.partial(_group_ffn_kernel, tm=tm, nt=nt),
        out_shape=jax.ShapeDtypeStruct((n_rows * SLAB, LANES), F32),
        grid_spec=pltpu.PrefetchScalarGridSpec(
            num_scalar_prefetch=1,
            grid=(nt, ff // tf),
            in_specs=[rows,
                      pl.BlockSpec((None, d, tf), lambda t, f, te: (te[t], 0, f)),
                      pl.BlockSpec((None, d, tf), lambda t, f, te: (te[t], 0, f)),
                      pl.BlockSpec((None, tf, d), lambda t, f, te: (te[t], f, 0))],
            out_specs=rows,
            scratch_shapes=[pltpu.VMEM((tm, d), BF16), pltpu.VMEM((tm, d), F32)]),
        compiler_params=_cp(("arbitrary", "arbitrary"), VMEM_LIMIT),
        name="moe_group_ffn",
    )(tile_expert, x2, wg, wu, wd)
    return out.reshape(n_rows, SLAB, LANES)


def _combine_kernel(pos_hbm, ys_hbm, x_ref, gate_ref, w_ref, fg_ref, o_ref, idx_smem, ybuf, isem, sem,
                    *, tm, final_norm):
    step = pl.program_id(0) * pl.num_programs(1) + pl.program_id(1)
    icopy = pltpu.make_async_copy(pos_hbm.at[pl.ds(step * 2 * tm, 2 * tm)], idx_smem, isem)
    icopy.start()
    icopy.wait()

    def row_copy(i, k):
        return pltpu.make_async_copy(ys_hbm.at[idx_smem[2 * i + k]], ybuf.at[k, pl.ds(i * SLAB, SLAB), :], sem)

    def issue(i, carry):
        row_copy(i, 0).start()
        row_copy(i, 1).start(priority=1)
        return carry

    def drain(i, carry):
        row_copy(i, 0).wait()
        row_copy(i, 1).wait()
        return carry

    lax.fori_loop(0, tm, issue, 0, unroll=8)
    lax.fori_loop(0, tm, drain, 0, unroll=8)
    w = w_ref[...]
    moe = w[:, 0:1] * _from_slabs(ybuf.at[0], tm) + w[:, 1:2] * _from_slabs(ybuf.at[1], tm)
    y = x_ref[...] + gate_ref[...] * moe
    if final_norm:
        ms = jnp.mean(y * y, axis=-1, keepdims=True)
        y = y * lax.rsqrt(ms + RMS_EPS) * fg_ref[...]
    o_ref[...] = y


def _combine(ys, pos, x, gate, w, final_g):
    b, l, d = x.shape
    tm = min(l, ROUTE_TM)
    nl = l // tm
    final_norm = final_g is not None
    fg = final_g if final_norm else jnp.ones((1, d), F32)
    row = lambda bi, i: (bi, i, 0)
    return pl.pallas_call(
        functools.partial(_combine_kernel, tm=tm, final_norm=final_norm),
        out_shape=jax.ShapeDtypeStruct((b, l, d), F32),
        grid=(b, nl),
        in_specs=[pl.BlockSpec(memory_space=pl.ANY),
                  pl.BlockSpec(memory_space=pl.ANY),
                  pl.BlockSpec((None, tm, d), row),
                  pl.BlockSpec((None, 1, d), lambda bi, i: (bi, 0, 0)),
                  pl.BlockSpec((tm, LANES), lambda bi, i: (bi * nl + i, 0)),
                  pl.BlockSpec((1, d), lambda bi, i: (0, 0))],
        out_specs=pl.BlockSpec((None, tm, d), row),
        scratch_shapes=[pltpu.SMEM((2 * tm,), jnp.int32), pltpu.VMEM((2, tm * SLAB, LANES), F32),
                        pltpu.SemaphoreType.DMA, pltpu.SemaphoreType.DMA],
        compiler_params=_cp(("arbitrary", "arbitrary"), VMEM_LIMIT),
        name="moe_combine",
    )(pos.reshape(-1), ys, x, gate, w, fg)


def _moe(x, g, sh, sc, gate, router_w, router_b, wg, wu, wd, final_g):
    b, l, d = x.shape
    n = b * l
    ne = router_w.shape[-1]
    rw = jnp.pad(router_w.astype(F32), ((0, 0), (0, LANES - ne)))
    rwh, rwl = _split_bf16(rw)
    rb = jnp.pad(router_b.astype(F32), (0, LANES - ne), constant_values=NEG_BIG).reshape(1, LANES)
    xs, meta, w, counts = _route(x, g, sh, sc, rwh, rwl, rb)
    counts = counts[0, :ne]
    padded = (counts + GROUP_TM - 1) // GROUP_TM * GROUP_TM
    ends = jnp.cumsum(padded)
    offs = ends - padded
    experts = jnp.arange(ne, dtype=jnp.int32)
    off_of = lambda e: jnp.sum(jnp.where(e[:, None] == experts[None, :], offs[None, :], 0), axis=1)
    pos = jnp.stack([off_of(meta[:, 0]) + meta[:, 2], off_of(meta[:, 1]) + meta[:, 3]], axis=1).astype(jnp.int32)
    n_rows = TOP_K * n + ne * GROUP_TM
    nt = n_rows // GROUP_TM
    tile_start = jnp.arange(nt, dtype=jnp.int32) * GROUP_TM
    tile_e = jnp.sum(tile_start[:, None] >= ends[None, :], axis=1)
    n_live = ends[-1] // GROUP_TM
    last_e = jnp.sum((n_live - 1) * GROUP_TM >= ends)
    tile_e = jnp.where(tile_start < ends[-1], tile_e, last_e)
    tile_expert = jnp.concatenate([tile_e, n_live[None]]).astype(jnp.int32)
    xd = _dispatch(xs, pos, n_rows)
    ys = _group_ffn(xd, tile_expert, wg, wu, wd, tf=512)
    return _combine(ys, pos, x, gate, w, final_g)


def _hyena_filters(l, fw1, fb1, freq1, fw2, fb2, freq2, fw3):
    half = l // 2
    pos = np.concatenate([np.arange(half), l - 1 - np.arange(half)]).astype(np.float32)[:, None]
    t = jnp.asarray(pos) / (l - 1)
    w = (2.0 * math.pi / l) * jnp.asarray(pos)
    bands = jnp.linspace(1e-4, HYENA_BANDS - 1, HYENA_BANDS, dtype=F32)[None, :]
    z = jnp.concatenate([t, jnp.cos(bands * w), -jnp.sin(bands * w)], axis=-1)
    h = jnp.sin(freq1.astype(F32) * (jnp.dot(z, fw1.astype(F32), precision=HI) + fb1.astype(F32)))
    h = jnp.sin(freq2.astype(F32) * (jnp.dot(h, fw2.astype(F32), precision=HI) + fb2.astype(F32)))
    h = jnp.dot(h, fw3.astype(F32), precision=HI).reshape(l, 2, HYENA_ORDER, D_HYENA)
    max_decay = math.log(HYENA_DECAY_TARGET) / HYENA_FAST_DECAY_PCT
    min_decay = math.log(HYENA_DECAY_TARGET) / HYENA_SLOW_DECAY_PCT
    deltas = jnp.abs(jnp.linspace(min_decay, max_decay, D_HYENA, dtype=F32))
    return h * jnp.exp(-t[:, :, None, None] * deltas)


def _hyena_spectra(l, mats_fwd, *filter_params):
    half = l // 2
    n2 = 2 * l
    h = _hyena_filters(l, *filter_params)
    tot = jnp.sum(jnp.abs(h), axis=0)
    hb0 = h[0, 1]
    denom = tot[0] + tot[1] - jnp.abs(hb0)
    sig = h.reshape(2, half, 2, HYENA_ORDER, D_HYENA).transpose(2, 3, 0, 1, 4)
    sig = sig.reshape(2 * HYENA_ORDER, 2, half, D_HYENA)
    fr, fi = _hy_fwd(sig, mats_fwd, None, out_dtype=F32)
    fr = fr.reshape(2, HYENA_ORDER, 2, half, D_HYENA)
    fi = fi.reshape(2, HYENA_ORDER, 2, half, D_HYENA)
    freq = 2 * np.arange(half)[None, :] + np.arange(2)[:, None]
    phi = np.pi * freq / (2.0 * l)
    cph = jnp.asarray(np.cos(phi), F32)[:, :, None]
    sph = jnp.asarray(np.sin(phi), F32)[:, :, None]
    re = fr * cph - fi * sph
    im = fr * sph + fi * cph
    dc_slot = jnp.asarray(freq == 0)[:, :, None]
    k_re = re[0] + re[1] - hb0[:, None, None, :]
    k_im = im[0] - im[1]
    k_nyq = fi[0] + fi[1] - hb0[:, None, None, :]
    wgt = jnp.where(dc_slot, 1.0, 2.0) / (n2 * denom[:, None, None, :])
    kr = k_re * wgt
    ki = jnp.where(dc_slot, 0.0, k_im * wgt)
    kd = jnp.where(dc_slot, k_nyq * wgt, kr)
    return [jnp.stack([kr[o], ki[o], kd[o]], axis=1) for o in range(HYENA_ORDER)]


def _fnet_maps(w_f, l):
    d = FNET_HEAD_DIM
    idx = np.arange(d)
    ang = 2.0 * np.pi * ((idx[:, None] * idx[None, :]) % d) / d
    scale = 1.0 / math.sqrt(d * l)
    cd = jnp.asarray(np.cos(ang) * scale, F32)
    sd = jnp.asarray(np.sin(ang) * scale, F32)
    wf = w_f.astype(F32)
    a = jnp.einsum('de,hef->hdf', cd, wf, precision=HI)
    bm = -jnp.einsum('de,hef->hdf', sd, wf, precision=HI)
    eye = jnp.eye(FNET_HEADS, dtype=F32)
    blk = lambda m: jnp.einsum('hdf,hg->hdgf', m, eye).reshape(D_FNET, D_FNET).astype(BF16)
    return blk(a), blk(bm)


def kernel(x, c, ctx, c_ctx, ada_w, ada_b, norm_mix_g, norm_ffn_g, w_in, w_out, fnet_w, hy_conv_w, hy_conv_b,
           hy_fw1, hy_fb1, hy_freq1, hy_fw2, hy_fb2, hy_freq2, hy_fw3, hy_bias, s5_a_re, s5_a_im, s5_log_dt,
           s5_b_re, s5_b_im, s5_c_re, s5_c_im, s5_d, s5_glu_w, s5_glu_b, ffn_w_gate, ffn_w_up, ffn_w_down,
           moe_router_w, moe_router_b, moe_w_gate, moe_w_up, moe_w_down, final_g):
    b, l, d = x.shape
    lc = ctx.shape[1]
    depth = ada_w.shape[0]
    assert l % LANES == 0 and lc % LANES == 0 and d % LANES == 0

    dft = {}
    for n in sorted({l, lc}):
        fwd = _hyena_dft(n)
        dft[n] = dict(hy_fwd=fwd, hy_inv={k: m.T for k, m in fwd.items()}, fn=_fnet_dft(n))

    cc = jnp.concatenate([c, c_ctx[None, :]], axis=0).astype(F32)
    rows = -(-(b + 1) // 8) * 8
    cc = jnp.pad(cc, ((0, rows - (b + 1)), (0, 0)))

    def mix(li, xs, sh, sc, grid_mode, s0f, s0b, s5m, states_only):
        n_tok = xs.shape[1]
        g_mix = norm_mix_g[li].reshape(1, d)
        if states_only:
            (ps,) = _inproj(xs, g_mix, sh, sc, w_in[li][:, S5_OFF:].astype(BF16),
                            jnp.zeros((3, LANES), F32), jnp.zeros((1, LANES), F32),
                            grid_mode=grid_mode, full=False)
            _, hf, hb = _s5(ps, s0f, s0b, s5m, s5_d[li])
            return None, hf, hb
        pf, qv, qx1, qx2, ps = _inproj(
            xs, g_mix, sh, sc, w_in[li].astype(BF16), hy_conv_w[li].astype(F32),
            hy_conv_b[li].astype(F32).reshape(1, -1), grid_mode=grid_mode, full=True)
        tabs = dft[n_tok]
        amat, bmat = _fnet_maps(fnet_w[li], n_tok)
        yf = _fnet(pf, *tabs['fn'], amat, bmat)
        spectra = _hyena_spectra(n_tok, tabs['hy_fwd'], hy_fw1[li], hy_fb1[li], hy_freq1[li], hy_fw2[li],
                                 hy_fb2[li], hy_freq2[li], hy_fw3[li])
        z = qv
        for o, xg in enumerate((qx1, qx2)):
            pr, pi = _hy_fwd(z, tabs['hy_fwd'], spectra[o])
            z = _hy_inv(pr, pi, tabs['hy_inv'], xg, z, hy_bias[li][o].astype(F32))
        ys, hf, hb = _s5(ps, s0f, s0b, s5m, s5_d[li])
        return (yf, z, ys), hf, hb

    def out_proj(li, parts, xs, gate):
        wo = w_out[li].astype(BF16)
        return _outproj(parts[0], parts[1], parts[2], xs, gate,
                        wo[:D_FNET], wo[D_FNET:D_FNET + D_HYENA], wo[D_FNET + D_HYENA:],
                        s5_glu_w[li].astype(BF16), s5_glu_b[li].astype(F32).reshape(1, -1))

    def channel_mix(li, xs, sh, sc, gate, fin):
        i = li // 2
        g_ffn = norm_ffn_g[li].reshape(1, d)
        if li % 2 == 0:
            return _ffn(xs, g_ffn, sh, sc, gate, ffn_w_gate[i].astype(BF16),
                        ffn_w_up[i].astype(BF16), ffn_w_down[i].astype(BF16), fin, tf=1408)
        return _moe(xs, g_ffn, sh, sc, gate, moe_router_w[i], moe_router_b[i], moe_w_gate[i].astype(BF16),
                    moe_w_up[i].astype(BF16), moe_w_down[i].astype(BF16), fin)

    zero_state = jnp.zeros((b, N_S5_BLOCKS, SLAB, LANES), F32)
    ctx_s = ctx
    for li in range(depth):
        last = li == depth - 1
        mod = _ada(cc, ada_w.astype(F32), li, ada_b[li].astype(F32))
        lat = [m.reshape(b, 1, d) for m in jnp.split(mod[:b], N_MOD, axis=-1)]
        cm = [jnp.broadcast_to(m.reshape(1, 1, d), (b, 1, d)) for m in jnp.split(mod[b:b + 1], N_MOD, axis=-1)]
        s5m = _s5_matrices(s5_a_re[li], s5_a_im[li], s5_log_dt[li], s5_b_re[li], s5_b_im[li],
                           s5_c_re[li], s5_c_im[li])

        parts, hf_c, hb_c = mix(li, ctx_s, cm[0], cm[1], False, zero_state, zero_state, s5m, last)
        if not last:
            ctx_s = out_proj(li, parts, ctx_s, cm[2])
            ctx_s = channel_mix(li, ctx_s, cm[3], cm[4], cm[5], None)

        parts, _, _ = mix(li, x, lat[0], lat[1], True, hf_c, hb_c, s5m, False)
        x = out_proj(li, parts, x, lat[2])
        x = channel_mix(li, x, lat[3], lat[4], lat[5], final_g.reshape(1, d) if last else None)

    if depth == 0:
        raise ValueError("depth must be positive")
    return x
```
